```python
import math
import jax, jax.numpy as jnp
from jax import lax
import numpy as np

D_MODEL = 1024
BATCH = 8
SEQ = 2048
DEPTH = 4
DEC_BATCH = 32
DEC_SEQ = 4
PAST_LEN = 8192
PAGE_SIZE = 128

HEAD_DIM = 64
NSA_HEADS = 8
NSA_KV_GROUPS = 2
NSA_GROUP_SIZE = NSA_HEADS // NSA_KV_GROUPS
CMP_STRIDE = 16
CMP_BLOCK = 2 * CMP_STRIDE
SEL_BLOCK = 64
SEL_TOPK = 16
WINDOW = 512
WIN_QBLOCK = 128
SEL_QBLOCK = 64
GLA_HEADS = 4
GLA_DK = 64
GLA_DV = 128
GLA_GATE_RANK = 16
GLA_GATE_NORM = 16.0
GLA_CHUNK = 64
FFN_DIM = 2816
CONV_W = 3
EPS = 1e-6
NEG_INF = -1e30
FORCE_SCORE = 1e9

A_WIDTH = NSA_HEADS * HEAD_DIM
B_WIDTH = GLA_HEADS * GLA_DV
KV_A_WIDTH = 2 * NSA_KV_GROUPS * HEAD_DIM
SPLITS = (A_WIDTH, KV_A_WIDTH, KV_A_WIDTH, KV_A_WIDTH, 3 * NSA_HEADS,
          GLA_HEADS * GLA_DK, GLA_HEADS * GLA_DK, B_WIDTH, B_WIDTH, GLA_GATE_RANK, 2 * D_MODEL)
SPLIT_IDX = tuple(int(v) for v in np.cumsum(SPLITS)[:-1])
N_IN_COLS = int(sum(SPLITS))

kernel_name = 'nsa_gla_gated_hybrid_step'


def rmsnorm(x, g):
    xf = x.astype(jnp.float32)
    y = xf * lax.rsqrt(jnp.mean(xf * xf, axis=-1, keepdims=True) + EPS)
    return (y * g.astype(jnp.float32)).astype(x.dtype)


def qk_norm_rows(kv, g):
    return jnp.concatenate([rmsnorm(kv[:, :, :1], g), kv[:, :, 1:]], axis=2)


def alibi_slopes():
    s = 2.0 ** (-8.0 * np.arange(1, NSA_HEADS + 1) / NSA_HEADS)
    return jnp.asarray(s, jnp.float32).reshape(NSA_KV_GROUPS, NSA_GROUP_SIZE)


def masked_softmax(s, mask):
    p = jax.nn.softmax(jnp.where(mask, s, NEG_INF), axis=-1)
    return jnp.where(mask, p, 0.0)


def compress_kv(rows, w_c1, w_c2):
    B, T = rows.shape[:2]
    t_pad = -(-T // CMP_STRIDE) * CMP_STRIDE
    rows = jnp.pad(rows, ((0, 0), (0, t_pad - T), (0, 0), (0, 0), (0, 0)))
    halves = rows.reshape(B, t_pad // CMP_STRIDE, CMP_STRIDE, 2, NSA_KV_GROUPS, HEAD_DIM)
    blocks = jnp.concatenate([halves[:, :-1], halves[:, 1:]], axis=2)
    hid = jax.nn.gelu(jnp.einsum('bnlcgd,clde->bncge', blocks, w_c1))
    out = jnp.einsum('bncge,cef->bncgf', hid, w_c2)
    ends = jnp.arange(out.shape[1], dtype=jnp.int32) * CMP_STRIDE + (CMP_BLOCK - 1)
    return out, ends


def cmp_attend(q, kv_c, ends, q_pos, g_k, slopes):
    kc = rmsnorm(kv_c[:, :, 0], g_k).astype(jnp.float32)
    vc = kv_c[:, :, 1].astype(jnp.float32)
    s = jnp.einsum('btgrd,bngd->btgrn', q, kc)
    dist = (q_pos[:, None] - ends[None, :]).astype(jnp.float32)
    s = s - slopes[None, None, :, :, None] * dist[None, :, None, None, :]
    mask = (ends[None, :] <= q_pos[:, None])[None, :, None, None, :]
    p = masked_softmax(s, mask)
    return jnp.einsum('btgrn,bngd->btgrd', p, vc), p


def select_blocks(p_cmp, q_pos, n_keys):
    n_sel = -(-n_keys // SEL_BLOCK)
    per = SEL_BLOCK // CMP_STRIDE
    imp = p_cmp.sum(axis=3)
    imp = imp + jnp.pad(imp[..., :-1], ((0, 0), (0, 0), (0, 0), (1, 0)))
    imp = jnp.pad(imp, ((0, 0), (0, 0), (0, 0), (0, n_sel * per - imp.shape[-1])))
    imp = imp.reshape(imp.shape[:3] + (n_sel, per)).sum(axis=-1)
    blk = jnp.arange(n_sel, dtype=jnp.int32)
    forced = (blk[None, :] == 0) | (blk[None, :] == (q_pos // SEL_BLOCK)[:, None])
    valid = blk[None, :] * SEL_BLOCK <= q_pos[:, None]
    score = jnp.where(forced[None, :, None, :], FORCE_SCORE,
                      jnp.where(valid[None, :, None, :], imp, NEG_INF))
    _, idx = lax.top_k(score, min(SEL_TOPK, n_sel))
    return idx


def to_sel_blocks(rows):
    B, T = rows.shape[:2]
    n_sel = -(-T // SEL_BLOCK)
    rows = jnp.pad(rows, ((0, 0), (0, n_sel * SEL_BLOCK - T), (0, 0), (0, 0), (0, 0)))
    rows = rows.reshape(B, n_sel, SEL_BLOCK, 2, NSA_KV_GROUPS, HEAD_DIM)
    return rows.transpose(0, 4, 1, 2, 3, 5).astype(jnp.float32)


def sel_attend(q, idx, q_pos, blocks, slopes):
    B = q.shape[0]
    b_ix = jnp.arange(B)[:, None, None, None]
    g_ix = jnp.arange(NSA_KV_GROUPS)[None, None, :, None]
    kv = blocks[b_ix, g_ix, idx]
    s = jnp.einsum('btgrd,btgnjd->btgrnj', q, kv[..., 0, :])
    k_pos = idx[..., None] * SEL_BLOCK + jnp.arange(SEL_BLOCK, dtype=jnp.int32)
    dist = q_pos[None, :, None, None, None] - k_pos
    s = s - slopes[None, None, :, :, None, None] * dist[:, :, :, None].astype(jnp.float32)
    mask = (dist >= 0)[:, :, :, None]
    s = jnp.where(mask, s, NEG_INF)
    p = jax.nn.softmax(s.reshape(s.shape[:4] + (-1,)), axis=-1).reshape(s.shape)
    p = jnp.where(mask, p, 0.0)
    return jnp.einsum('btgrnj,btgnjd->btgrd', p, kv[..., 1, :])


def win_attend(q, kv, q_pos, k_pos, slopes):
    kv = kv.astype(jnp.float32)
    s = jnp.einsum('bqtgrd,bqsgd->bqtgrs', q, kv[:, :, :, 0])
    dist = q_pos[:, :, None] - k_pos[:, None, :]
    mask = (dist >= 0) & (dist < WINDOW) & (k_pos[:, None, :] >= 0)
    s = s - slopes[None, None, None, :, :, None] * dist.astype(jnp.float32)[None, :, :, None, None, :]
    p = masked_softmax(s, mask[None, :, :, None, None, :])
    return jnp.einsum('bqtgrs,bqsgd->bqtgrd', p, kv[:, :, :, 1])


def gla_scan(q, k, v, log_a, S0):
    B, T, H, DK = q.shape
    DV = v.shape[-1]
    L = math.gcd(T, GLA_CHUNK)
    nc = T // L

    def to_chunks(a):
        return a.astype(jnp.float32).reshape(B, nc, L, H, a.shape[-1]).transpose(1, 0, 3, 2, 4)

    tril = jnp.tril(jnp.ones((L, L), dtype=bool))

    def step(S, inp):
        qi, ki, vi, gi = inp
        b = jnp.cumsum(gi, axis=2)
        qe = qi * jnp.exp(b)
        att = jnp.where(tril, jnp.einsum('bhid,bhjd->bhij', qe, ki * jnp.exp(-b)), 0.0)
        o = jnp.einsum('bhij,bhjv->bhiv', att, vi) + jnp.einsum('bhid,bhdv->bhiv', qe, S)
        bl = b[:, :, -1:, :]
        S = jnp.exp(bl[:, :, 0, :, None]) * S + jnp.einsum('bhjd,bhjv->bhdv', ki * jnp.exp(bl - b), vi)
        return S, o

    S, o = lax.scan(step, S0.astype(jnp.float32), (to_chunks(q), to_chunks(k), to_chunks(v), to_chunks(log_a)))
    return o.transpose(1, 0, 3, 2, 4).reshape(B, T, H, DV), S


def layer(x, t0, prm, past_cmp, past_slc, win_buf, gla_state, conv_buf):
    (w_in, w_cmp1, w_cmp2, g_q, g_k_cmp, g_k_slc, g_k_win, w_gate_up, b_gate, g_gla,
     w_pa, w_pb, w_out, g_attn, g_ffn, w_up, w_conv, b_conv, w_down) = prm
    B, T, _ = x.shape
    G, R, Dh = NSA_KV_GROUPS, NSA_GROUP_SIZE, HEAD_DIM
    q_pos = t0 + jnp.arange(T, dtype=jnp.int32)
    slopes = alibi_slopes()

    h = rmsnorm(x, g_attn)
    (q_a, kv_cmp, kv_slc, kv_win, gate_a, q_b, k_b, v_b, r_b, lr_b, merge) = jnp.split(h @ w_in, SPLIT_IDX, axis=-1)

    q = rmsnorm(q_a.reshape(B, T, G, R, Dh), g_q).astype(jnp.float32) * Dh ** -0.5
    kv_cmp = kv_cmp.reshape(B, T, 2, G, Dh)
    kv_slc = qk_norm_rows(kv_slc.reshape(B, T, 2, G, Dh), g_k_slc)
    kv_win = qk_norm_rows(kv_win.reshape(B, T, 2, G, Dh), g_k_win)
    if past_cmp is None:
        cmp_all, slc_all = kv_cmp, kv_slc
    else:
        cmp_all = jnp.concatenate([past_cmp.astype(x.dtype), kv_cmp], axis=1)
        slc_all = jnp.concatenate([past_slc.astype(x.dtype), kv_slc], axis=1)
    n_keys = cmp_all.shape[1]

    kv_c, ends = compress_kv(cmp_all, w_cmp1, w_cmp2)
    o_cmp, p_cmp = cmp_attend(q, kv_c, ends, q_pos, g_k_cmp, slopes)

    idx = select_blocks(p_cmp, q_pos, n_keys)
    blocks = to_sel_blocks(slc_all)
    if past_slc is None:
        nb = T // SEL_QBLOCK
        qb = q.reshape(B, nb, SEL_QBLOCK, G, R, Dh).swapaxes(0, 1)
        ib = idx.reshape(B, nb, SEL_QBLOCK, G, idx.shape[-1]).swapaxes(0, 1)
        pb = q_pos.reshape(nb, SEL_QBLOCK)
        ob = lax.map(lambda a: sel_attend(a[0], a[1], a[2], blocks, slopes), (qb, ib, pb))
        o_slc = ob.swapaxes(0, 1).reshape(B, T, G, R, Dh)
    else:
        o_slc = sel_attend(q, idx, q_pos, blocks, slopes)

    if win_buf is None:
        nb = T // WIN_QBLOCK
        npre = WINDOW // WIN_QBLOCK
        padded = jnp.pad(kv_win, ((0, 0), (WINDOW, 0), (0, 0), (0, 0), (0, 0)))
        padded = padded.reshape(B, nb + npre, WIN_QBLOCK, 2, G, Dh)
        bidx = jnp.arange(nb)[:, None] + jnp.arange(npre + 1)[None, :]
        band = padded[:, bidx].reshape(B, nb, (npre + 1) * WIN_QBLOCK, 2, G, Dh)
        k_pos = (bidx[..., None] * WIN_QBLOCK + jnp.arange(WIN_QBLOCK)[None, None, :] - WINDOW).reshape(nb, -1)
        o_win = win_attend(q.reshape(B, nb, WIN_QBLOCK, G, R, Dh), band,
                           q_pos.reshape(nb, WIN_QBLOCK), k_pos, slopes).reshape(B, T, G, R, Dh)
        win_state = kv_win[:, -min(WINDOW, T):]
    else:
        n_buf = win_buf.shape[1]
        keys = jnp.concatenate([win_buf.astype(x.dtype), kv_win], axis=1)
        k_pos = (t0 - n_buf + jnp.arange(n_buf + T, dtype=jnp.int32))[None]
        o_win = win_attend(q[:, None], keys[:, None], q_pos[None], k_pos, slopes)[:, 0]
        win_state = keys[:, -n_buf:]

    ga = jax.nn.sigmoid(gate_a.astype(jnp.float32)).reshape(B, T, 3, G, R)
    o_a = (ga[:, :, 0, :, :, None] * o_cmp + ga[:, :, 1, :, :, None] * o_slc
           + ga[:, :, 2, :, :, None] * o_win)
    o_a = o_a.reshape(B, T, A_WIDTH).astype(x.dtype)

    H, DK, DV = GLA_HEADS, GLA_DK, GLA_DV
    qg = q_b.reshape(B, T, H, DK).astype(jnp.float32) * DK ** -0.5
    log_a = jax.nn.log_sigmoid((lr_b @ w_gate_up + b_gate).astype(jnp.float32)) / GLA_GATE_NORM
    o_b, S_new = gla_scan(qg, k_b.reshape(B, T, H, DK), v_b.reshape(B, T, H, DV),
                          log_a.reshape(B, T, H, DK), gla_state)
    o_b = rmsnorm(o_b, g_gla) * jax.nn.silu(r_b.reshape(B, T, H, DV).astype(jnp.float32))
    o_b = o_b.reshape(B, T, B_WIDTH).astype(x.dtype)

    g_ma, g_mb = jnp.split(merge, 2, axis=-1)
    mix = jax.nn.sigmoid(g_ma) * (o_a @ w_pa) + jax.nn.sigmoid(g_mb) * (o_b @ w_pb)
    x = x + mix @ w_out

    h2 = rmsnorm(x, g_ffn)
    u, gt = jnp.split(h2 @ w_up, 2, axis=-1)
    u_ext = jnp.concatenate([conv_buf.astype(u.dtype), u], axis=1)
    acc = b_conv
    for k in range(CONV_W):
        acc = acc + w_conv[k] * u_ext[:, k:k + T]
    x = x + (jax.nn.gelu(acc) * gt) @ w_down
    conv_state = u_ext[:, -(CONV_W - 1):]

    return x, (kv_cmp, kv_slc, win_state, S_new.astype(gla_state.dtype), conv_state)


def setup_inputs(seed: int = 0) -> dict:
    key = jax.random.key(seed)
    ks = jax.random.split(key, 32)
    f32 = jnp.float32

    def nrm(k, shape, scale):
        return jax.random.normal(k, shape, f32) * scale

    def gain(k, shape):
        return 1.0 + 0.02 * jax.random.normal(k, shape, f32)

    n_pages = PAST_LEN // PAGE_SIZE
    n_pool = (5 * DEC_BATCH * n_pages + 3) // 4
    w_buf = min(WINDOW, PAST_LEN)
    kv_shape = (DEPTH, n_pool, PAGE_SIZE, 2, NSA_KV_GROUPS, HEAD_DIM)
    page_table = jax.random.permutation(ks[7], n_pool)[:DEC_BATCH * n_pages]
    page_table = page_table.reshape(DEC_BATCH, n_pages).astype(jnp.int32)
    return {
        'x_prompt': nrm(ks[0], (BATCH, SEQ, D_MODEL), 1.0),
        'x_sample': nrm(ks[1], (DEC_BATCH, DEC_SEQ, D_MODEL), 1.0),
        'cache_kv_cmp': nrm(ks[2], kv_shape, 1.0),
        'cache_kv_slc': nrm(ks[3], kv_shape, 1.0),
        'state_kv_win': nrm(ks[4], (DEPTH, DEC_BATCH, w_buf, 2, NSA_KV_GROUPS, HEAD_DIM), 1.0),
        'state_gla': nrm(ks[5], (DEPTH, DEC_BATCH, GLA_HEADS, GLA_DK, GLA_DV), 1.0),
        'state_conv': nrm(ks[6], (DEPTH, DEC_BATCH, CONV_W - 1, FFN_DIM), 1.0),
        'page_table': page_table,
        'w_in': nrm(ks[8], (DEPTH, D_MODEL, N_IN_COLS), D_MODEL ** -0.5),
        'w_cmp1': nrm(ks[9], (DEPTH, 2, CMP_BLOCK, HEAD_DIM, HEAD_DIM), (CMP_BLOCK * HEAD_DIM) ** -0.5),
        'w_cmp2': nrm(ks[10], (DEPTH, 2, HEAD_DIM, HEAD_DIM), HEAD_DIM ** -0.5),
        'g_q': gain(ks[11], (DEPTH, HEAD_DIM)),
        'g_k_cmp': gain(ks[12], (DEPTH, HEAD_DIM)),
        'g_k_slc': gain(ks[13], (DEPTH, HEAD_DIM)),
        'g_k_win': gain(ks[14], (DEPTH, HEAD_DIM)),
        'w_gate_up': nrm(ks[15], (DEPTH, GLA_GATE_RANK, GLA_HEADS * GLA_DK), GLA_GATE_RANK ** -0.5),
        'b_gate': nrm(ks[16], (DEPTH, GLA_HEADS * GLA_DK), 0.1),
        'g_gla': gain(ks[17], (DEPTH, GLA_DV)),
        'w_pa': nrm(ks[18], (DEPTH, A_WIDTH, D_MODEL), A_WIDTH ** -0.5),
        'w_pb': nrm(ks[19], (DEPTH, B_WIDTH, D_MODEL), B_WIDTH ** -0.5),
        'w_out': nrm(ks[20], (DEPTH, D_MODEL, D_MODEL), D_MODEL ** -0.5),
        'g_attn': gain(ks[21], (DEPTH, D_MODEL)),
        'g_ffn': gain(ks[22], (DEPTH, D_MODEL)),
        'w_up': nrm(ks[23], (DEPTH, D_MODEL, 2 * FFN_DIM), D_MODEL ** -0.5),
        'w_conv': nrm(ks[24], (DEPTH, CONV_W, FFN_DIM), CONV_W ** -0.5),
        'b_conv': nrm(ks[25], (DEPTH, FFN_DIM), 0.02),
        'w_down': nrm(ks[26], (DEPTH, FFN_DIM, D_MODEL), FFN_DIM ** -0.5),
    }


def reference(x_prompt, x_sample, cache_kv_cmp, cache_kv_slc, state_kv_win, state_gla, state_conv,
              page_table, w_in, w_cmp1, w_cmp2, g_q, g_k_cmp, g_k_slc, g_k_win, w_gate_up, b_gate,
              g_gla, w_pa, w_pb, w_out, g_attn, g_ffn, w_up, w_conv, b_conv, w_down):
    bp = x_prompt.shape[0]
    bs = x_sample.shape[0]
    y_prompt, y_sample = x_prompt, x_sample
    st_p, st_s = [], []
    for l in range(DEPTH):
        prm = (w_in[l], w_cmp1[l], w_cmp2[l], g_q[l], g_k_cmp[l], g_k_slc[l], g_k_win[l],
               w_gate_up[l], b_gate[l], g_gla[l], w_pa[l], w_pb[l], w_out[l], g_attn[l], g_ffn[l],
               w_up[l], w_conv[l], b_conv[l], w_down[l])
        y_prompt, sp = layer(y_prompt, 0, prm, None, None, None,
                             jnp.zeros((bp, GLA_HEADS, GLA_DK, GLA_DV), y_prompt.dtype),
                             jnp.zeros((bp, CONV_W - 1, FFN_DIM), y_prompt.dtype))
        past_cmp = cache_kv_cmp[l][page_table].reshape(bs, -1, 2, NSA_KV_GROUPS, HEAD_DIM)
        past_slc = cache_kv_slc[l][page_table].reshape(bs, -1, 2, NSA_KV_GROUPS, HEAD_DIM)
        y_sample, ss = layer(y_sample, PAST_LEN, prm, past_cmp, past_slc,
                             state_kv_win[l], state_gla[l], state_conv[l])
        st_p.append(sp)
        st_s.append(ss)
    new_kv_cmp_prompt = jnp.stack([s[0] for s in st_p])
    new_kv_slc_prompt = jnp.stack([s[1] for s in st_p])
    new_kv_win_prompt = jnp.stack([s[2] for s in st_p])
    new_gla_prompt = jnp.stack([s[3] for s in st_p])
    new_conv_prompt = jnp.stack([s[4] for s in st_p])
    new_kv_cmp_sample = jnp.stack([s[0] for s in st_s])
    new_kv_slc_sample = jnp.stack([s[1] for s in st_s])
    new_kv_win_sample = jnp.stack([s[2] for s in st_s])
    new_gla_sample = jnp.stack([s[3] for s in st_s])
    new_conv_sample = jnp.stack([s[4] for s in st_s])
    return (y_prompt, y_sample, new_kv_cmp_prompt, new_kv_slc_prompt, new_kv_win_prompt, new_gla_prompt,
            new_conv_prompt, new_kv_cmp_sample, new_kv_slc_sample, new_kv_win_sample, new_gla_sample,
            new_conv_sample)
```

```python
import functools
import math

import numpy as np
import jax
import jax.numpy as jnp
from jax import lax
from jax.experimental import pallas as pl
from jax.experimental.pallas import tpu as pltpu

F32 = jnp.float32
BF16 = jnp.bfloat16

D_MODEL = 1024
HEAD_DIM = 64
NSA_HEADS = 8
NSA_KV_GROUPS = 2
NSA_GROUP_SIZE = NSA_HEADS // NSA_KV_GROUPS
CMP_STRIDE = 16
CMP_BLOCK = 2 * CMP_STRIDE
SEL_BLOCK = 64
SEL_TOPK = 16
WINDOW = 512
GLA_HEADS = 4
GLA_DK = 64
GLA_DV = 128
GLA_GATE_RANK = 16
GLA_GATE_NORM = 16.0
GLA_CHUNK = 64
FFN_DIM = 2816
CONV_W = 3
EPS = 1e-6
NEG_INF = -1e30
FORCE_SCORE = 1e9
BELOW_ALL_SCORES = -3e38

A_WIDTH = NSA_HEADS * HEAD_DIM
B_WIDTH = GLA_HEADS * GLA_DV
KV_WIDTH = 2 * NSA_KV_GROUPS * HEAD_DIM
GK_WIDTH = GLA_HEADS * GLA_DK
LANES = 128
SUBLANES = 8
SLOT = SUBLANES
ALIBI = tuple(float(2.0 ** (-8.0 * (h + 1) / NSA_HEADS)) for h in range(NSA_HEADS))

C_Q, C_CMP, C_SLC, C_WIN, C_GATE = 0, 512, 768, 1024, 1280
C_QB, C_KB, C_VB, C_RB, C_LR, C_END = 1408, 1664, 1920, 2432, 2944, 3072
NORM_W = A_WIDTH + 2 * LANES

VMEM_LIMIT = 56 * 1024 * 1024


def _dot(a, b):
    return jnp.dot(a.astype(BF16), b.astype(BF16), preferred_element_type=F32)


def _dot_nt(a, b):
    return lax.dot_general(a.astype(BF16), b.astype(BF16), (((1,), (1,)), ((), ())),
                           preferred_element_type=F32)


def _dot_tn(a, b):
    return lax.dot_general(a.astype(BF16), b.astype(BF16), (((0,), (0,)), ((), ())),
                           preferred_element_type=F32)


def _split_bf16(x, parts):
    out = []
    for _ in range(parts):
        p = x.astype(BF16)
        out.append(p)
        x = x - p.astype(F32)
    return out


def _dot_wide(x, m, parts, dot=None):
    dot = dot or (lambda a, b: jnp.dot(a, b, preferred_element_type=F32))
    acc = None
    for p in _split_bf16(x, parts):
        t = dot(p, m)
        acc = t if acc is None else acc + t
    return acc


def _rms_rows(x, g):
    return x * lax.rsqrt(jnp.mean(x * x, axis=-1, keepdims=True) + EPS) * g


def _rms_groups(x, ones_blockdiag, g):
    ms = _dot_wide(x * x, ones_blockdiag, 2) * (1.0 / HEAD_DIM)
    return x * lax.rsqrt(ms + EPS) * g


def _sigmoid(x):
    return 1.0 / (1.0 + jnp.exp(-x))


def _gelu_tanh(x):
    return 0.5 * x * (1.0 + jnp.tanh(math.sqrt(2.0 / math.pi) * (x + 0.044715 * (x * x * x))))


def _log_sigmoid(x):
    return jnp.minimum(x, 0.0) - jnp.log1p(jnp.exp(-jnp.abs(x)))


def _topk_mask(score, lane_f, k):
    sel = jnp.zeros_like(score)
    for _ in range(k):
        m = jnp.max(score, axis=-1, keepdims=True)
        first = jnp.min(jnp.where(score == m, lane_f, 1e9), axis=-1, keepdims=True)
        hit = lane_f == first
        sel = jnp.where(hit, 1.0, sel)
        score = jnp.where(hit, BELOW_ALL_SCORES, score)
    return sel


def _masked_softmax_pv(s, mask, v):
    s = jnp.where(mask, s, NEG_INF)
    m = jnp.max(s, axis=-1, keepdims=True)
    e = jnp.where(mask, jnp.exp(s - m), 0.0)
    l = jnp.sum(e, axis=-1, keepdims=True)
    p = e * jnp.where(l > 0.0, 1.0 / l, 0.0)
    return p, _dot(p, v)


def _stack_heads(q, g):
    base = g * NSA_GROUP_SIZE
    return jnp.concatenate([q[:, (base + r) * HEAD_DIM:(base + r + 1) * HEAD_DIM]
                            for r in range(NSA_GROUP_SIZE)], axis=0)


def _head_slopes(g, n):
    row = lax.broadcasted_iota(jnp.int32, (NSA_GROUP_SIZE * n, 1), 0)
    slope = jnp.full((NSA_GROUP_SIZE * n, 1), ALIBI[g * NSA_GROUP_SIZE + NSA_GROUP_SIZE - 1], F32)
    for r in reversed(range(NSA_GROUP_SIZE - 1)):
        slope = jnp.where(row < (r + 1) * n, ALIBI[g * NSA_GROUP_SIZE + r], slope)
    return slope


def _gate_cols(ga, branch, g):
    c0 = branch * NSA_HEADS + g * NSA_GROUP_SIZE
    return jnp.concatenate([ga[:, c0 + r:c0 + r + 1] for r in range(NSA_GROUP_SIZE)], axis=0)


def _block_scores(blk_imp, q_pos, lane_i, n_sel):
    forced = (lane_i == 0) | (lane_i == lax.shift_right_logical(q_pos, int(math.log2(SEL_BLOCK))))
    valid = lane_i * SEL_BLOCK <= q_pos
    score = jnp.where(forced, FORCE_SCORE, jnp.where(valid, blk_imp, NEG_INF))
    return jnp.where(lane_i < n_sel, score, BELOW_ALL_SCORES)


def _compress_rows(load_rows, w1_ref, w2_ref, ones128_ref, gk_ref):
    out = []
    for c in range(2):
        acc = None
        for l in range(CMP_BLOCK):
            t = jnp.dot(load_rows(c, l).astype(BF16), w1_ref[l, c], preferred_element_type=F32)
            acc = t if acc is None else acc + t
        out.append(jnp.dot(_gelu_tanh(acc).astype(BF16), w2_ref[c], preferred_element_type=F32))
    return _rms_groups(out[0], ones128_ref[...], gk_ref[...]), out[1]


def _proj_in_kernel(x_ref, gattn_ref, w_ref, gnorm_ref, ones_ref, wgu_ref, bg_ref,
                    q_ref, cmp_ref, slc_ref, win_ref, ga_ref, qg_ref, kb_ref, vb_ref, rb_ref, la_ref):
    h = _rms_rows(x_ref[...], gattn_ref[...])
    p = jnp.dot(h.astype(BF16), w_ref[...], preferred_element_type=F32)
    to_norm = jnp.concatenate([p[:, C_Q:C_CMP], p[:, C_SLC:C_SLC + LANES], p[:, C_WIN:C_WIN + LANES]], axis=1)
    y = _rms_groups(to_norm, ones_ref[...], gnorm_ref[...])
    q_ref[...] = y[:, :A_WIDTH] * HEAD_DIM ** -0.5
    cmp_ref[...] = p[:, C_CMP:C_SLC]
    slc_ref[...] = jnp.concatenate([y[:, A_WIDTH:A_WIDTH + LANES], p[:, C_SLC + LANES:C_WIN]], axis=1)
    win_ref[...] = jnp.concatenate([y[:, A_WIDTH + LANES:], p[:, C_WIN + LANES:C_GATE]], axis=1)
    ga_ref[...] = _sigmoid(p[:, C_GATE:C_QB])
    qg_ref[...] = p[:, C_QB:C_KB] * GLA_DK ** -0.5
    kb_ref[...] = p[:, C_KB:C_VB]
    vb_ref[...] = p[:, C_VB:C_RB]
    rb_ref[...] = p[:, C_RB:C_LR]
    z = jnp.dot(p[:, C_LR:C_END].astype(BF16), wgu_ref[...], preferred_element_type=F32) + bg_ref[...]
    la_ref[...] = _log_sigmoid(z) * (1.0 / GLA_GATE_NORM)


def _const_spec(shape, layer=None):
    nd = len(shape)
    if layer is None:
        return pl.BlockSpec(shape, lambda *_: (0,) * nd)
    return pl.BlockSpec((None,) + shape, lambda *_: (layer,) + (0,) * nd)


def _proj_in(x, wts, layer, tm):
    m = x.shape[0]
    widths = (A_WIDTH, KV_WIDTH, KV_WIDTH, KV_WIDTH, LANES, GK_WIDTH, GK_WIDTH, B_WIDTH, B_WIDTH, GK_WIDTH)
    row = lambda w: pl.BlockSpec((tm, w), lambda i: (i, 0))
    return pl.pallas_call(
        _proj_in_kernel,
        grid=(m // tm,),
        in_specs=[row(D_MODEL),
                  _const_spec((1, D_MODEL), layer), _const_spec((D_MODEL, C_END), layer),
                  _const_spec((1, NORM_W), layer), _const_spec((NORM_W, NORM_W)),
                  _const_spec((LANES, GK_WIDTH), layer), _const_spec((1, GK_WIDTH), layer)],
        out_specs=[row(w) for w in widths],
        out_shape=[jax.ShapeDtypeStruct((m, w), F32) for w in widths],
        compiler_params=pltpu.CompilerParams(dimension_semantics=("arbitrary",), vmem_limit_bytes=VMEM_LIMIT),
        name="proj_in",
    )(x, wts["g_attn"], wts["w_a"], wts["g_norm"], wts["ones_norm"], wts["w_gate_up"], wts["b_gate"])


def _compress_kernel(kv_ref, w1_ref, w2_ref, ones_ref, gk_ref, out_ref, pad_ref, *, t):
    for c in range(2):
        pad_ref[c, 0:t, :] = kv_ref[0, :, c * LANES:(c + 1) * LANES]
        pad_ref[c, t:t + CMP_STRIDE, :] = jnp.zeros((CMP_STRIDE, LANES), F32)
    n_blk = t // CMP_STRIDE
    kc, vc = _compress_rows(lambda c, l: pad_ref[c, pl.ds(l, n_blk, stride=CMP_STRIDE), :],
                            w1_ref, w2_ref, ones_ref, gk_ref)
    out_ref[0] = jnp.concatenate([kc, vc], axis=1)


def _compress(kv_cmp, wts, layer):
    b, t, _ = kv_cmp.shape
    n_blk = t // CMP_STRIDE
    return pl.pallas_call(
        functools.partial(_compress_kernel, t=t),
        grid=(b,),
        in_specs=[pl.BlockSpec((1, t, KV_WIDTH), lambda i: (i, 0, 0)),
                  _const_spec((CMP_BLOCK, 2, LANES, LANES), layer), _const_spec((2, LANES, LANES), layer),
                  _const_spec((LANES, LANES)), _const_spec((1, LANES), layer)],
        out_specs=pl.BlockSpec((1, n_blk, KV_WIDTH), lambda i: (i, 0, 0)),
        out_shape=jax.ShapeDtypeStruct((b, n_blk, KV_WIDTH), F32),
        scratch_shapes=[pltpu.VMEM((2, t + CMP_STRIDE, LANES), F32)],
        compiler_params=pltpu.CompilerParams(dimension_semantics=("arbitrary",), vmem_limit_bytes=VMEM_LIMIT),
        name="compress",
    )(kv_cmp, wts["w_c1"], wts["w_c2"], wts["ones128"], wts["g_k_cmp"])


def _online_chunk(carry, s, mask, v):
    m, l, acc = carry
    s = jnp.where(mask, s, NEG_INF)
    m_new = jnp.maximum(m, jnp.max(s, axis=-1, keepdims=True))
    alpha = jnp.exp(m - m_new)
    p = jnp.where(mask, jnp.exp(s - m_new), 0.0)
    return m_new, alpha * l + jnp.sum(p, axis=-1, keepdims=True), alpha * acc + _dot(p, v)


def _online_init(rows):
    return (jnp.full((rows, 1), NEG_INF, F32), jnp.zeros((rows, 1), F32), jnp.zeros((rows, HEAD_DIM), F32))


def _online_finish(carry):
    _, l, acc = carry
    return acc * jnp.where(l > 0.0, 1.0 / l, 0.0)


def _nsa_prompt_kernel(q_ref, ga_ref, kvc_ref, slc_ref, win_ref, msel_ref, esel_ref, o_ref, *, tq, tk_sel, tk_win):
    i = pl.program_id(1)
    q0 = i * tq
    q = q_ref[0]
    ga = ga_ref[0]
    rows = NSA_GROUP_SIZE * tq
    n_cmp = kvc_ref.shape[1]
    n_sel = slc_ref.shape[1] // SEL_BLOCK
    q_pos = q0 + lax.broadcasted_iota(jnp.int32, (tq, 1), 0)
    q_pos4 = jnp.concatenate([q_pos] * NSA_GROUP_SIZE, axis=0)
    lane_i = lax.broadcasted_iota(jnp.int32, (1, LANES), 1)
    lane_f = lane_i.astype(F32)
    ends = lax.broadcasted_iota(jnp.int32, (1, n_cmp), 1) * CMP_STRIDE + (CMP_BLOCK - 1)
    pieces = []
    for g in range(NSA_KV_GROUPS):
        slope = _head_slopes(g, tq)
        qg = _stack_heads(q, g).astype(BF16)
        kcol = slice(g * HEAD_DIM, (g + 1) * HEAD_DIM)
        vcol = slice(LANES + g * HEAD_DIM, LANES + (g + 1) * HEAD_DIM)

        dist = q_pos4 - ends
        s = _dot_nt(qg, kvc_ref[0, :, kcol]) - slope * dist.astype(F32)
        p, o_cmp = _masked_softmax_pv(s, dist >= 0, kvc_ref[0, :, vcol])
        imp = p[0:tq]
        for r in range(1, NSA_GROUP_SIZE):
            imp = imp + p[r * tq:(r + 1) * tq]

        blk_imp = _dot_wide(imp, msel_ref[...], 3)
        sel = _topk_mask(_block_scores(blk_imp, q_pos, lane_i, n_sel), lane_f, min(SEL_TOPK, n_sel))
        sel4 = jnp.concatenate([sel] * NSA_GROUP_SIZE, axis=0).astype(BF16)

        def sel_body(c, carry):
            k0 = pl.multiple_of(c * tk_sel, tk_sel)
            dist = q_pos4 - (k0 + lax.broadcasted_iota(jnp.int32, (1, tk_sel), 1))
            s = _dot_nt(qg, slc_ref[0, pl.ds(k0, tk_sel), kcol]) - slope * dist.astype(F32)
            in_sel = _dot_nt(sel4, esel_ref[pl.ds(k0, tk_sel), :])
            return _online_chunk(carry, s, (in_sel > 0.5) & (dist >= 0), slc_ref[0, pl.ds(k0, tk_sel), vcol])

        n_chunks = (q0 + tq + tk_sel - 1) // tk_sel
        o_slc = _online_finish(lax.fori_loop(0, n_chunks, sel_body, _online_init(rows)))

        def win_body(c, carry):
            k0 = pl.multiple_of(c * tk_win, tk_win)
            dist = q_pos4 - (k0 + lax.broadcasted_iota(jnp.int32, (1, tk_win), 1))
            s = _dot_nt(qg, win_ref[0, pl.ds(k0, tk_win), kcol]) - slope * dist.astype(F32)
            return _online_chunk(carry, s, (dist >= 0) & (dist < WINDOW), win_ref[0, pl.ds(k0, tk_win), vcol])

        first = jnp.maximum(q0 - WINDOW + 1, 0) // tk_win
        o_win = _online_finish(lax.fori_loop(first, (q0 + tq + tk_win - 1) // tk_win, win_body, _online_init(rows)))

        og = _gate_cols(ga, 0, g) * o_cmp + _gate_cols(ga, 1, g) * o_slc + _gate_cols(ga, 2, g) * o_win
        pieces += [og[r * tq:(r + 1) * tq] for r in range(NSA_GROUP_SIZE)]
    o_ref[0] = jnp.concatenate(pieces, axis=1)


def _nsa_prompt(q, ga, kvc, kv_slc, kv_win, wts, tq):
    b, t, _ = q.shape
    n_cmp = kvc.shape[1]
    tile = lambda w: pl.BlockSpec((1, tq, w), lambda bi, i: (bi, i, 0))
    full = lambda n, w: pl.BlockSpec((1, n, w), lambda bi, i: (bi, 0, 0))
    return pl.pallas_call(
        functools.partial(_nsa_prompt_kernel, tq=tq, tk_sel=min(2 * tq, t), tk_win=tq),
        grid=(b, t // tq),
        in_specs=[tile(A_WIDTH), tile(LANES), full(n_cmp, KV_WIDTH), full(t, KV_WIDTH), full(t, KV_WIDTH),
                  _const_spec((n_cmp, LANES)), _const_spec((t, LANES))],
        out_specs=tile(A_WIDTH),
        out_shape=jax.ShapeDtypeStruct((b, t, A_WIDTH), F32),
        compiler_params=pltpu.CompilerParams(dimension_semantics=("arbitrary", "arbitrary"),
                                             vmem_limit_bytes=VMEM_LIMIT),
        name="nsa_prompt",
    )(q, ga, kvc, kv_slc, kv_win, wts["m_sel_prompt"], wts["e_sel_prompt"])


def _nsa_sample_kernel(pt_ref, q_ref, ga_ref, ncmp_ref, nslc_ref, nwin_ref, wbuf_ref, cmp_hbm, slc_hbm,
                       w1_ref, w2_ref, ones_ref, gk_ref, msel_ref, esel_ref,
                       o_ref, xc_ref, xs_ref, xw_ref, sem_ref, *, layer, past, n_tok, tk):
    b = pl.program_id(0)
    nb = pl.num_programs(0)
    n_pages = pt_ref.shape[1]
    page = past // n_pages
    slot = b % 2

    def page_copies(bb, sl):
        out = []
        for p in range(n_pages):
            pg = pt_ref[bb, p]
            for c in range(2):
                out.append(pltpu.make_async_copy(cmp_hbm.at[layer, pg, :, pl.ds(c * LANES, LANES)],
                                                 xc_ref.at[sl, c, pl.ds(p * page, page)], sem_ref.at[0, sl]))
            out.append(pltpu.make_async_copy(slc_hbm.at[layer, pg], xs_ref.at[sl, pl.ds(p * page, page)], sem_ref.at[1, sl]))
        return out

    @pl.when(b == 0)
    def _():
        for cp in page_copies(0, 0):
            cp.start()

    @pl.when(b + 1 < nb)
    def _():
        for cp in page_copies(b + 1, 1 - slot):
            cp.start()

    is_tok = lax.broadcasted_iota(jnp.int32, (SLOT, 1), 0) < n_tok
    for c in range(2):
        xc_ref[slot, c, past:past + SLOT, :] = jnp.where(is_tok, ncmp_ref[0, :, c * LANES:(c + 1) * LANES], 0.0)
        xc_ref[slot, c, past + SLOT:past + CMP_STRIDE, :] = jnp.zeros((CMP_STRIDE - SLOT, LANES), F32)
    xs_ref[slot, past:past + SLOT, :] = jnp.where(is_tok, nslc_ref[0], 0.0)
    xs_ref[slot, past + SLOT:past + LANES, :] = jnp.zeros((LANES - SLOT, KV_WIDTH), F32)
    n_buf = wbuf_ref.shape[1]
    xw_ref[0:n_buf, :] = wbuf_ref[0]
    xw_ref[n_buf:n_buf + SLOT, :] = jnp.where(is_tok, nwin_ref[0], 0.0)
    xw_ref[n_buf + SLOT:n_buf + LANES, :] = jnp.zeros((LANES - SLOT, KV_WIDTH), F32)

    for cp in page_copies(b, slot):
        cp.wait()

    q = q_ref[0]
    ga = ga_ref[0]
    rows = NSA_GROUP_SIZE * SLOT
    n_cmp = past // CMP_STRIDE
    n_sel = past // SEL_BLOCK + 1
    q_pos = past + lax.broadcasted_iota(jnp.int32, (SLOT, 1), 0)
    q_pos4 = jnp.concatenate([q_pos] * NSA_GROUP_SIZE, axis=0)
    sel_lanes = msel_ref.shape[1]
    lane_i = lax.broadcasted_iota(jnp.int32, (1, sel_lanes), 1)
    lane_f = lane_i.astype(F32)
    ends = lax.broadcasted_iota(jnp.int32, (1, n_cmp), 1) * CMP_STRIDE + (CMP_BLOCK - 1)

    kc, vc = _compress_rows(lambda c, l: xc_ref[slot, c, pl.ds(l, n_cmp, stride=CMP_STRIDE), :],
                            w1_ref, w2_ref, ones_ref, gk_ref)
    pieces = []
    for g in range(NSA_KV_GROUPS):
        slope = _head_slopes(g, SLOT)
        qg = _stack_heads(q, g).astype(BF16)
        kcol = slice(g * HEAD_DIM, (g + 1) * HEAD_DIM)
        vcol = slice(LANES + g * HEAD_DIM, LANES + (g + 1) * HEAD_DIM)

        dist = q_pos4 - ends
        s = _dot_nt(qg, kc[:, kcol]) - slope * dist.astype(F32)
        p, o_cmp = _masked_softmax_pv(s, dist >= 0, vc[:, kcol])
        imp = p[0:SLOT]
        for r in range(1, NSA_GROUP_SIZE):
            imp = imp + p[r * SLOT:(r + 1) * SLOT]

        blk_imp = _dot_wide(imp, msel_ref[...], 3)
        sel = _topk_mask(_block_scores(blk_imp, q_pos, lane_i, n_sel), lane_f, min(SEL_TOPK, n_sel))
        sel4 = jnp.concatenate([sel] * NSA_GROUP_SIZE, axis=0).astype(BF16)

        chunks_per_slab = esel_ref.shape[0]
        carry = _online_init(rows)
        for c in range(past // tk + 1):
            k0 = c * tk
            n_k = tk if k0 < past else LANES
            dist = q_pos4 - (k0 + lax.broadcasted_iota(jnp.int32, (1, n_k), 1))
            s = _dot_nt(qg, xs_ref[slot, k0:k0 + n_k, kcol]) - slope * dist.astype(F32)
            if k0 < past:
                slab = c // chunks_per_slab
                in_sel = _dot(sel4[:, slab * LANES:(slab + 1) * LANES], esel_ref[c % chunks_per_slab])
            else:
                in_sel = sel4[:, n_sel - 1:n_sel].astype(F32)
            carry = _online_chunk(carry, s, (in_sel > 0.5) & (dist >= 0), xs_ref[slot, k0:k0 + n_k, vcol])
        o_slc = _online_finish(carry)

        k_pos = (past - n_buf) + lax.broadcasted_iota(jnp.int32, (1, n_buf + LANES), 1)
        dist = q_pos4 - k_pos
        s = _dot_nt(qg, xw_ref[:, kcol]) - slope * dist.astype(F32)
        _, o_win = _masked_softmax_pv(s, (dist >= 0) & (dist < WINDOW), xw_ref[:, vcol])

        og = _gate_cols(ga, 0, g) * o_cmp + _gate_cols(ga, 1, g) * o_slc + _gate_cols(ga, 2, g) * o_win
        pieces += [og[r * SLOT:(r + 1) * SLOT] for r in range(NSA_GROUP_SIZE)]
    o_ref[0] = jnp.concatenate(pieces, axis=1)


def _nsa_sample(q, ga, ncmp, nslc, nwin, win_buf, cache_cmp, cache_slc, page_table, wts, layer, past, n_tok):
    bs = q.shape[0]
    n_buf = win_buf.shape[2]
    n_cmp = past // CMP_STRIDE
    tk = wts["e_sel_sample"].shape[2]
    slot_spec = lambda w: pl.BlockSpec((1, SLOT, w), lambda b, pt: (b, 0, 0))
    const = lambda shape, lyr=None: (pl.BlockSpec(shape, lambda b, pt: (0,) * len(shape)) if lyr is None else
                                     pl.BlockSpec((None,) + shape, lambda b, pt: (lyr,) + (0,) * len(shape)))
    grid_spec = pltpu.PrefetchScalarGridSpec(
        num_scalar_prefetch=1,
        grid=(bs,),
        in_specs=[slot_spec(A_WIDTH), slot_spec(LANES), slot_spec(KV_WIDTH), slot_spec(KV_WIDTH), slot_spec(KV_WIDTH),
                  pl.BlockSpec((None, 1, n_buf, KV_WIDTH), lambda b, pt: (layer, b, 0, 0)),
                  pl.BlockSpec(memory_space=pl.ANY), pl.BlockSpec(memory_space=pl.ANY),
                  const((CMP_BLOCK, 2, LANES, LANES), layer), const((2, LANES, LANES), layer),
                  const((LANES, LANES)), const((1, LANES), layer),
                  const(wts["m_sel_sample"].shape), const(wts["e_sel_sample"].shape)],
        out_specs=slot_spec(A_WIDTH),
        scratch_shapes=[pltpu.VMEM((2, 2, past + CMP_STRIDE, LANES), F32),
                        pltpu.VMEM((2, past + LANES, KV_WIDTH), F32),
                        pltpu.VMEM((n_buf + LANES, KV_WIDTH), F32),
                        pltpu.SemaphoreType.DMA((2, 2))],
    )
    return pl.pallas_call(
        functools.partial(_nsa_sample_kernel, layer=layer, past=past, n_tok=n_tok, tk=tk),
        grid_spec=grid_spec,
        out_shape=jax.ShapeDtypeStruct((bs, SLOT, A_WIDTH), F32),
        compiler_params=pltpu.CompilerParams(dimension_semantics=("arbitrary",), vmem_limit_bytes=VMEM_LIMIT),
        name="nsa_sample",
    )(page_table, q, ga, ncmp, nslc, nwin, win_buf, cache_cmp, cache_slc,
      wts["w_c1"], wts["w_c2"], wts["ones128"], wts["g_k_cmp"], wts["m_sel_sample"], wts["e_sel_sample"])


def _gla_kernel(qg_ref, kb_ref, vb_ref, la_ref, rb_ref, s0_ref, ggla_ref, o_ref, sout_ref, s_scr, *, chunk, n_valid):
    j = pl.program_id(1)

    @pl.when(j == 0)
    def _():
        s_scr[...] = s0_ref[0]

    tt = qg_ref.shape[1]
    ri = lax.broadcasted_iota(jnp.int32, (chunk, chunk), 0)
    ci = lax.broadcasted_iota(jnp.int32, (chunk, chunk), 1)
    tril = ri >= ci
    tril_b = jnp.where(tril, 1.0, 0.0).astype(BF16)
    ones_b = jnp.ones((chunk, GLA_DV), BF16)
    g_gla = ggla_ref[...]

    def body(c, _):
        r0 = pl.multiple_of(c * chunk, chunk)
        rows = pl.ds(r0, chunk)
        q = qg_ref[0, rows, :]
        k = kb_ref[0, rows, :]
        v = vb_ref[0, rows, :]
        la = la_ref[0, rows, :]
        if n_valid < chunk:
            ok = lax.broadcasted_iota(jnp.int32, (chunk, 1), 0) < n_valid
            q, k, v, la = (jnp.where(ok, a, 0.0) for a in (q, k, v, la))
        bcum = _dot_wide(la, tril_b, 3, dot=lambda a, m: jnp.dot(m, a, preferred_element_type=F32))
        btot_t = _dot_wide(la, ones_b, 3, dot=lambda a, m: lax.dot_general(
            a, m, (((0,), (0,)), ((), ())), preferred_element_type=F32))
        qe = q * jnp.exp(bcum)
        kd = k * jnp.exp(-bcum)
        kl = k * jnp.exp(bcum[chunk - 1:chunk, :] - bcum)
        for h in range(GLA_HEADS):
            kc = slice(h * GLA_DK, (h + 1) * GLA_DK)
            vc = slice(h * GLA_DV, (h + 1) * GLA_DV)
            att = jnp.where(tril, _dot_nt(qe[:, kc], kd[:, kc]), 0.0)
            s_h = s_scr[h]
            o = _dot(att, v[:, vc]) + _dot(qe[:, kc], s_h)
            s_scr[h] = jnp.exp(btot_t[kc, :]) * s_h + _dot_tn(kl[:, kc], v[:, vc])
            r = rb_ref[0, rows, vc]
            o_ref[0, rows, vc] = _rms_rows(o, g_gla) * (r * _sigmoid(r))
        return 0

    lax.fori_loop(0, tt // chunk, body, 0)
    sout_ref[0] = s_scr[...]


def _gla(qg, kb, vb, la, rb, s0, wts, layer, tt, chunk, n_valid):
    b, t, _ = qg.shape
    tile = lambda w: pl.BlockSpec((1, tt, w), lambda bi, j: (bi, j, 0))
    st = pl.BlockSpec((1, GLA_HEADS, GLA_DK, GLA_DV), lambda bi, j: (bi, 0, 0, 0))
    return pl.pallas_call(
        functools.partial(_gla_kernel, chunk=chunk, n_valid=n_valid),
        grid=(b, t // tt),
        in_specs=[tile(GK_WIDTH), tile(GK_WIDTH), tile(B_WIDTH), tile(GK_WIDTH), tile(B_WIDTH), st,
                  pl.BlockSpec((None, 1, GLA_DV), lambda bi, j: (layer, 0, 0))],
        out_specs=[tile(B_WIDTH), st],
        out_shape=[jax.ShapeDtypeStruct((b, t, B_WIDTH), F32),
                   jax.ShapeDtypeStruct((b, GLA_HEADS, GLA_DK, GLA_DV), F32)],
        scratch_shapes=[pltpu.VMEM((GLA_HEADS, GLA_DK, GLA_DV), F32)],
        compiler_params=pltpu.CompilerParams(dimension_semantics=("arbitrary", "arbitrary"),
                                             vmem_limit_bytes=VMEM_LIMIT),
        name="gla",
    )(qg, kb, vb, la, rb, s0, wts["g_gla"])


def _mix_ffn_kernel(*refs, tm, f_chunk, slots):
    if slots:
        (x_ref, oa_ref, ob_ref, gattn_ref, gffn_ref, wm_ref, wpa_ref, wpb_ref, wout_ref, wup_ref, wconv_ref,
         bconv_ref, wdown_ref, cb1_ref, cb2_ref, y_ref, u_ref, u_scr) = refs
    else:
        (x_ref, oa_ref, ob_ref, gattn_ref, gffn_ref, wm_ref, wpa_ref, wpb_ref, wout_ref, wup_ref, wconv_ref,
         bconv_ref, wdown_ref, y_ref, u_ref, u_scr) = refs
    x = x_ref[...]
    h = _rms_rows(x, gattn_ref[...])
    mg = jnp.dot(h.astype(BF16), wm_ref[...], preferred_element_type=F32)
    mix = (_sigmoid(mg[:, :D_MODEL]) * jnp.dot(oa_ref[...].astype(BF16), wpa_ref[...], preferred_element_type=F32)
           + _sigmoid(mg[:, D_MODEL:]) * jnp.dot(ob_ref[...].astype(BF16), wpb_ref[...], preferred_element_type=F32))
    x1 = x + jnp.dot(mix.astype(BF16), wout_ref[...], preferred_element_type=F32)
    h2 = _rms_rows(x1, gffn_ref[...]).astype(BF16)

    if slots:
        u_scr[0:SUBLANES, :] = jnp.zeros((SUBLANES, FFN_DIM), F32)
        row_in_slot = lax.broadcasted_iota(jnp.int32, (tm, 1), 0) & (SLOT - 1)
    else:
        @pl.when(pl.program_id(1) == 0)
        def _():
            u_scr[0:SUBLANES, :] = jnp.zeros((SUBLANES, FFN_DIM), F32)

    y = x1
    for c0 in range(0, FFN_DIM, f_chunk):
        fc = slice(c0, c0 + f_chunk)
        u = jnp.dot(h2, wup_ref[:, fc], preferred_element_type=F32)
        gt = jnp.dot(h2, wup_ref[:, FFN_DIM + c0:FFN_DIM + c0 + f_chunk], preferred_element_type=F32)
        u_scr[SUBLANES:SUBLANES + tm, fc] = u
        u1 = u_scr[SUBLANES - 1:SUBLANES - 1 + tm, fc]
        u2 = u_scr[SUBLANES - 2:SUBLANES - 2 + tm, fc]
        if slots:
            u1 = jnp.where(row_in_slot == 0, cb1_ref[:, fc], u1)
            u2 = jnp.where(row_in_slot < 2, cb2_ref[:, fc], u2)
            u_ref[:, fc] = u
        acc = bconv_ref[:, fc] + wconv_ref[0:1, fc] * u2 + wconv_ref[1:2, fc] * u1 + wconv_ref[2:3, fc] * u
        y = y + jnp.dot((_gelu_tanh(acc) * gt).astype(BF16), wdown_ref[fc, :], preferred_element_type=F32)
    y_ref[...] = y
    if not slots:
        tail = u_scr[tm:tm + SUBLANES, :]
        u_scr[0:SUBLANES, :] = tail
        u_ref[0] = tail


def _mix_ffn(x, oa, ob, wts, layer, tm, seq_tiles, cb=None):
    m = x.shape[0]
    slots = cb is not None
    n_seq = m // (tm * seq_tiles)
    if slots:
        idx = lambda i: (i, 0)
        grid = (m // tm,)
        sem = ("arbitrary",)
    else:
        idx = lambda b, i: (b * seq_tiles + i, 0)
        grid = (n_seq, seq_tiles)
        sem = ("arbitrary", "arbitrary")
    nidx = len(grid)

    def wspec(shape):
        return pl.BlockSpec((None,) + shape, lambda *_: (layer,) + (0,) * len(shape),
                            pipeline_mode=pl.Buffered(1))

    row = lambda w: pl.BlockSpec((tm, w), idx)
    in_specs = [row(D_MODEL), row(A_WIDTH), row(B_WIDTH),
                wspec((1, D_MODEL)), wspec((1, D_MODEL)), wspec((D_MODEL, 2 * D_MODEL)),
                wspec((A_WIDTH, D_MODEL)), wspec((B_WIDTH, D_MODEL)), wspec((D_MODEL, D_MODEL)),
                wspec((D_MODEL, 2 * FFN_DIM)), wspec((SUBLANES, FFN_DIM)), wspec((1, FFN_DIM)),
                wspec((FFN_DIM, D_MODEL))]
    args = [x, oa, ob, wts["g_attn"], wts["g_ffn"], wts["w_merge"], wts["w_pa"], wts["w_pb"], wts["w_out"],
            wts["w_up"], wts["w_conv"], wts["b_conv"], wts["w_down"]]
    if slots:
        in_specs += [row(FFN_DIM), row(FFN_DIM)]
        args += list(cb)
        u_spec = row(FFN_DIM)
        u_shape = jax.ShapeDtypeStruct((m, FFN_DIM), F32)
    else:
        u_spec = pl.BlockSpec((1, SUBLANES, FFN_DIM), lambda b, i: (b, 0, 0))
        u_shape = jax.ShapeDtypeStruct((n_seq, SUBLANES, FFN_DIM), F32)
    return pl.pallas_call(
        functools.partial(_mix_ffn_kernel, tm=tm, f_chunk=FFN_DIM // 2, slots=slots),
        grid=grid,
        in_specs=in_specs,
        out_specs=[row(D_MODEL), u_spec],
        out_shape=[jax.ShapeDtypeStruct((m, D_MODEL), F32), u_shape],
        scratch_shapes=[pltpu.VMEM((tm + SUBLANES, FFN_DIM), F32)],
        compiler_params=pltpu.CompilerParams(dimension_semantics=sem, vmem_limit_bytes=VMEM_LIMIT),
        name="mix_ffn",
    )(*args)


def _sel_fold_matrix(n_cmp_valid, n_cmp_rows, n_sel_lanes):
    per = SEL_BLOCK // CMP_STRIDE
    m = np.zeros((n_cmp_rows, n_sel_lanes), np.float32)
    for n in range(n_cmp_valid):
        m[n, n // per] += 1.0
        if n + 1 < n_cmp_valid:
            m[n, (n + 1) // per] += 1.0
    return jnp.asarray(m, BF16)


def _pack_weights(w_in, w_cmp1, w_cmp2, g_q, g_k_cmp, g_k_slc, g_k_win, w_gate_up, b_gate, g_gla, w_pa, w_pb,
                  w_out, g_attn, g_ffn, w_up, w_conv, b_conv, w_down):
    depth = w_in.shape[0]
    splits = (A_WIDTH, KV_WIDTH, KV_WIDTH, KV_WIDTH, 3 * NSA_HEADS, GK_WIDTH, GK_WIDTH, B_WIDTH, B_WIDTH,
              GLA_GATE_RANK, 2 * D_MODEL)
    offs = np.concatenate([[0], np.cumsum(splits)])
    seg = lambda k: w_in[:, :, int(offs[k]):int(offs[k + 1])]
    padc = lambda a, w: jnp.pad(a, ((0, 0), (0, 0), (0, w - a.shape[-1])))
    w_a = jnp.concatenate([seg(0), seg(1), seg(2), seg(3), padc(seg(4), LANES), seg(5), seg(6), seg(7), seg(8),
                           padc(seg(9), LANES)], axis=-1).astype(BF16)
    same_g = jnp.eye(NSA_KV_GROUPS, dtype=F32)
    w_c1 = jnp.einsum("Lclde,gG->LlcgdGe", w_cmp1, same_g).reshape(depth, CMP_BLOCK, 2, LANES, LANES).astype(BF16)
    w_c2 = jnp.einsum("Lcef,gG->LcgeGf", w_cmp2, same_g).reshape(depth, 2, LANES, LANES).astype(BF16)
    grp = lambda n: jnp.asarray(np.kron(np.eye(n // HEAD_DIM), np.ones((HEAD_DIM, HEAD_DIM))), BF16)
    g_norm = jnp.concatenate([jnp.tile(g_q, (1, NSA_HEADS)), jnp.tile(g_k_slc, (1, NSA_KV_GROUPS)),
                              jnp.tile(g_k_win, (1, NSA_KV_GROUPS))], axis=-1)
    return {
        "w_a": w_a,
        "w_merge": w_in[:, :, int(offs[10]):].astype(BF16),
        "g_attn": g_attn[:, None, :], "g_ffn": g_ffn[:, None, :],
        "g_norm": g_norm[:, None, :], "ones_norm": grp(NORM_W), "ones128": grp(LANES),
        "g_k_cmp": jnp.tile(g_k_cmp, (1, NSA_KV_GROUPS))[:, None, :],
        "w_gate_up": jnp.pad(w_gate_up, ((0, 0), (0, LANES - GLA_GATE_RANK), (0, 0))).astype(BF16),
        "b_gate": b_gate[:, None, :], "g_gla": g_gla[:, None, :],
        "w_c1": w_c1, "w_c2": w_c2,
        "w_pa": w_pa.astype(BF16), "w_pb": w_pb.astype(BF16), "w_out": w_out.astype(BF16),
        "w_up": w_up.astype(BF16), "w_down": w_down.astype(BF16),
        "w_conv": jnp.pad(w_conv, ((0, 0), (0, SUBLANES - CONV_W), (0, 0))), "b_conv": b_conv[:, None, :],
    }


def _expand_matrix(n_blocks, n_keys):
    return jnp.asarray(np.arange(n_keys)[None, :] // SEL_BLOCK == np.arange(n_blocks)[:, None], BF16)


def _prompt_layer(x, wts, layer, b, t):
    tm = min(256, t)
    tq = min(128, t)
    q, kv_cmp, kv_slc, kv_win, ga, qg, kb, vb, rb, la = _proj_in(x, wts, layer, tm)
    r3 = lambda a: a.reshape(b, t, a.shape[-1])
    kvc = _compress(r3(kv_cmp), wts, layer)
    o_a = _nsa_prompt(r3(q), r3(ga), kvc, r3(kv_slc), r3(kv_win), wts, tq)
    o_b, s_new = _gla(r3(qg), r3(kb), r3(vb), r3(la), r3(rb),
                      jnp.zeros((b, GLA_HEADS, GLA_DK, GLA_DV), F32), wts, layer,
                      tt=min(512, t), chunk=math.gcd(t, GLA_CHUNK), n_valid=math.gcd(t, GLA_CHUNK))
    y, u_tail = _mix_ffn(x, o_a.reshape(b * t, A_WIDTH), o_b.reshape(b * t, B_WIDTH), wts, layer, tm, t // tm)
    kv6 = lambda a: a.reshape(b, t, 2, NSA_KV_GROUPS, HEAD_DIM)
    n_win = min(WINDOW, t)
    state = (kv6(kv_cmp), kv6(kv_slc), kv6(kv_win)[:, t - n_win:], s_new, u_tail[:, SUBLANES - (CONV_W - 1):])
    return y, state


def _sample_layer(x, wts, layer, bs, n_tok, past, cache_cmp, cache_slc, win_buf, gla_state, conv_buf, page_table):
    m = bs * SLOT
    q, kv_cmp, kv_slc, kv_win, ga, qg, kb, vb, rb, la = _proj_in(x, wts, layer, m)
    r3 = lambda a: a.reshape(bs, SLOT, a.shape[-1])
    o_a = _nsa_sample(r3(q), r3(ga), r3(kv_cmp), r3(kv_slc), r3(kv_win), win_buf, cache_cmp, cache_slc,
                      page_table, wts, layer, past, n_tok)
    o_b, s_new = _gla(r3(qg), r3(kb), r3(vb), r3(la), r3(rb), gla_state[layer], wts, layer,
                      tt=SLOT, chunk=SLOT, n_valid=n_tok)
    cb = conv_buf[layer]
    zeros = lambda n: jnp.zeros((bs, n, FFN_DIM), F32)
    cb1 = jnp.concatenate([cb[:, 1:2], zeros(SLOT - 1)], axis=1).reshape(m, FFN_DIM)
    cb2 = jnp.concatenate([cb, zeros(SLOT - 2)], axis=1).reshape(m, FFN_DIM)
    y, u = _mix_ffn(x, o_a.reshape(m, A_WIDTH), o_b.reshape(m, B_WIDTH), wts, layer, m, 1, cb=(cb1, cb2))
    kv6 = lambda a: r3(a)[:, :n_tok].reshape(bs, n_tok, 2, NSA_KV_GROUPS, HEAD_DIM)
    n_buf = win_buf.shape[2]
    win_state = jnp.concatenate([win_buf[layer], r3(kv_win)[:, :n_tok]], axis=1)[:, n_tok:]
    u_ext = jnp.concatenate([cb, u.reshape(bs, SLOT, FFN_DIM)[:, :n_tok]], axis=1)
    state = (kv6(kv_cmp), kv6(kv_slc), win_state.reshape(bs, n_buf, 2, NSA_KV_GROUPS, HEAD_DIM), s_new,
             u_ext[:, -(CONV_W - 1):])
    return y, state


def kernel(x_prompt, x_sample, cache_kv_cmp, cache_kv_slc, state_kv_win, state_gla, state_conv, page_table,
           w_in, w_cmp1, w_cmp2, g_q, g_k_cmp, g_k_slc, g_k_win, w_gate_up, b_gate, g_gla, w_pa, w_pb, w_out,
           g_attn, g_ffn, w_up, w_conv, b_conv, w_down):
    depth = w_in.shape[0]
    bp, t, _ = x_prompt.shape
    bs, n_tok, _ = x_sample.shape
    page = cache_kv_cmp.shape[2]
    past = page_table.shape[1] * page
    assert n_tok <= SLOT and t % CMP_STRIDE == 0 and past % SEL_BLOCK == 0

    wts = _pack_weights(w_in, w_cmp1, w_cmp2, g_q, g_k_cmp, g_k_slc, g_k_win, w_gate_up, b_gate, g_gla, w_pa,
                        w_pb, w_out, g_attn, g_ffn, w_up, w_conv, b_conv, w_down)
    n_cmp_p = t // CMP_STRIDE
    wts["m_sel_prompt"] = _sel_fold_matrix(n_cmp_p - 1, n_cmp_p, LANES)
    wts["e_sel_prompt"] = _expand_matrix(LANES, t).T
    n_cmp_s = (past + CMP_STRIDE) // CMP_STRIDE - 1
    sel_lanes = -(-(past // SEL_BLOCK + 1) // LANES) * LANES
    tk_s = min(1024, past)
    wts["m_sel_sample"] = _sel_fold_matrix(n_cmp_s, n_cmp_s, sel_lanes)
    blk_per_chunk = tk_s // SEL_BLOCK
    wts["e_sel_sample"] = jnp.stack([_expand_matrix(LANES, LANES * SEL_BLOCK)[:, j * tk_s:(j + 1) * tk_s]
                                     for j in range(LANES // blk_per_chunk)])

    cache_cmp = cache_kv_cmp.reshape(depth, -1, page, KV_WIDTH)
    cache_slc = cache_kv_slc.reshape(depth, -1, page, KV_WIDTH)
    win_buf = state_kv_win.reshape(depth, bs, -1, KV_WIDTH)

    y_p = x_prompt.reshape(bp * t, D_MODEL)
    y_s = jnp.pad(x_sample, ((0, 0), (0, SLOT - n_tok), (0, 0))).reshape(bs * SLOT, D_MODEL)
    st_p, st_s = [], []
    for l in range(depth):
        y_p, sp = _prompt_layer(y_p, wts, l, bp, t)
        y_s, ss = _sample_layer(y_s, wts, l, bs, n_tok, past, cache_cmp, cache_slc, win_buf, state_gla,
                                state_conv, page_table)
        st_p.append(sp)
        st_s.append(ss)
    stack = lambda sts, k: jnp.stack([s[k] for s in sts])
    return ((y_p.reshape(bp, t, D_MODEL), y_s.reshape(bs, SLOT, D_MODEL)[:, :n_tok])
            + tuple(stack(st_p, k) for k in range(5)) + tuple(stack(st_s, k) for k in range(5)))
```

```python
import functools
import math

import numpy as np
import jax
import jax.numpy as jnp
from jax import lax
from jax.experimental import pallas as pl
from jax.experimental.pallas import tpu as pltpu

F32 = jnp.float32
BF16 = jnp.bfloat16

D_MODEL = 1024
HEAD_DIM = 64
NSA_HEADS = 8
NSA_KV_GROUPS = 2
NSA_GROUP_SIZE = NSA_HEADS // NSA_KV_GROUPS
CMP_STRIDE = 16
CMP_BLOCK = 2 * CMP_STRIDE
SEL_BLOCK = 64
SEL_TOPK = 16
WINDOW = 512
GLA_HEADS = 4
GLA_DK = 64
GLA_DV = 128
GLA_GATE_RANK = 16
GLA_GATE_NORM = 16.0
GLA_CHUNK = 64
FFN_DIM = 2816
CONV_W = 3
EPS = 1e-6
NEG_INF = -1e30
FORCE_SCORE = 1e9
BELOW_ALL_SCORES = -3e38

A_WIDTH = NSA_HEADS * HEAD_DIM
B_WIDTH = GLA_HEADS * GLA_DV
KV_WIDTH = 2 * NSA_KV_GROUPS * HEAD_DIM
GK_WIDTH = GLA_HEADS * GLA_DK
LANES = 128
SUBLANES = 8
SLOT = SUBLANES
ALIBI = tuple(float(2.0 ** (-8.0 * (h + 1) / NSA_HEADS)) for h in range(NSA_HEADS))

C_Q, C_CMP, C_SLC, C_WIN, C_GATE = 0, 512, 768, 1024, 1280
C_QB, C_KB, C_VB, C_RB, C_LR, C_END = 1408, 1664, 1920, 2432, 2944, 3072
NORM_W = A_WIDTH + 2 * LANES

VMEM_LIMIT = 56 * 1024 * 1024


def _dot(a, b):
    return jnp.dot(a.astype(BF16), b.astype(BF16), preferred_element_type=F32)


def _dot_nt(a, b):
    return lax.dot_general(a.astype(BF16), b.astype(BF16), (((1,), (1,)), ((), ())),
                           preferred_element_type=F32)


def _dot_tn(a, b):
    return lax.dot_general(a.astype(BF16), b.astype(BF16), (((0,), (0,)), ((), ())),
                           preferred_element_type=F32)


def _split_bf16(x, parts):
    out = []
    for _ in range(parts):
        p = x.astype(BF16)
        out.append(p)
        x = x - p.astype(F32)
    return out


def _dot_wide(x, m, parts, dot=None):
    dot = dot or (lambda a, b: jnp.dot(a, b, preferred_element_type=F32))
    acc = None
    for p in _split_bf16(x, parts):
        t = dot(p, m)
        acc = t if acc is None else acc + t
    return acc


def _rms_rows(x, g):
    return x * lax.rsqrt(jnp.mean(x * x, axis=-1, keepdims=True) + EPS) * g


def _rms_groups(x, ones_blockdiag, g):
    ms = _dot_wide(x * x, ones_blockdiag, 2) * (1.0 / HEAD_DIM)
    return x * lax.rsqrt(ms + EPS) * g


def _sigmoid(x):
    return 1.0 / (1.0 + jnp.exp(-x))


def _gelu_tanh(x):
    return 0.5 * x * (1.0 + jnp.tanh(math.sqrt(2.0 / math.pi) * (x + 0.044715 * (x * x * x))))


def _log_sigmoid(x):
    return jnp.minimum(x, 0.0) - jnp.log1p(jnp.exp(-jnp.abs(x)))


def _topk_mask(score, n_cand, k):
    lane = lax.broadcasted_iota(jnp.int32, score.shape, 1)
    ahead = jnp.zeros(score.shape, F32)
    for j in range(n_cand):
        col = score[:, j:j + 1]
        ahead = ahead + jnp.where(col > score, 1.0, jnp.where(col == score, jnp.where(lane > j, 1.0, 0.0), 0.0))
    return jnp.where(lane < n_cand, jnp.where(ahead < k, 1.0, 0.0), 0.0)


def _masked_softmax_pv(s, mask, v):
    s = jnp.where(mask, s, NEG_INF)
    m = jnp.max(s, axis=-1, keepdims=True)
    e = jnp.where(mask, jnp.exp(s - m), 0.0)
    l = jnp.sum(e, axis=-1, keepdims=True)
    p = e * jnp.where(l > 0.0, 1.0 / l, 0.0)
    return p, _dot(p, v)


def _stack_heads(q, g):
    base = g * NSA_GROUP_SIZE
    return jnp.concatenate([q[:, (base + r) * HEAD_DIM:(base + r + 1) * HEAD_DIM]
                            for r in range(NSA_GROUP_SIZE)], axis=0)


def _head_slopes(g, n):
    row = lax.broadcasted_iota(jnp.int32, (NSA_GROUP_SIZE * n, 1), 0)
    slope = jnp.full((NSA_GROUP_SIZE * n, 1), ALIBI[g * NSA_GROUP_SIZE + NSA_GROUP_SIZE - 1], F32)
    for r in reversed(range(NSA_GROUP_SIZE - 1)):
        slope = jnp.where(row < (r + 1) * n, ALIBI[g * NSA_GROUP_SIZE + r], slope)
    return slope


def _gate_cols(ga, branch, g):
    c0 = branch * NSA_HEADS + g * NSA_GROUP_SIZE
    return jnp.concatenate([ga[:, c0 + r:c0 + r + 1] for r in range(NSA_GROUP_SIZE)], axis=0)


def _block_scores(blk_imp, q_pos, lane_i, n_sel):
    forced = (lane_i == 0) | (lane_i == lax.shift_right_logical(q_pos, int(math.log2(SEL_BLOCK))))
    valid = lane_i * SEL_BLOCK <= q_pos
    score = jnp.where(forced, FORCE_SCORE, jnp.where(valid, blk_imp, NEG_INF))
    return jnp.where(lane_i < n_sel, score, BELOW_ALL_SCORES)


def _compress_rows(load_rows, w1_ref, w2_ref, ones128_ref, gk_ref):
    out = []
    for c in range(2):
        acc = None
        for l in range(CMP_BLOCK):
            t = jnp.dot(load_rows(c, l).astype(BF16), w1_ref[l, c], preferred_element_type=F32)
            acc = t if acc is None else acc + t
        out.append(jnp.dot(_gelu_tanh(acc).astype(BF16), w2_ref[c], preferred_element_type=F32))
    return _rms_groups(out[0], ones128_ref[...], gk_ref[...]), out[1]


def _proj_in_kernel(x_ref, gattn_ref, w_ref, gnorm_ref, ones_ref, wgu_ref, bg_ref,
                    q_ref, cmp_ref, slc_ref, win_ref, ga_ref, qg_ref, kb_ref, vb_ref, rb_ref, la_ref):
    h = _rms_rows(x_ref[...], gattn_ref[...])
    p = jnp.dot(h.astype(BF16), w_ref[...], preferred_element_type=F32)
    to_norm = jnp.concatenate([p[:, C_Q:C_CMP], p[:, C_SLC:C_SLC + LANES], p[:, C_WIN:C_WIN + LANES]], axis=1)
    y = _rms_groups(to_norm, ones_ref[...], gnorm_ref[...])
    q_ref[...] = y[:, :A_WIDTH] * HEAD_DIM ** -0.5
    cmp_ref[...] = p[:, C_CMP:C_SLC]
    slc_ref[...] = jnp.concatenate([y[:, A_WIDTH:A_WIDTH + LANES], p[:, C_SLC + LANES:C_WIN]], axis=1)
    win_ref[...] = jnp.concatenate([y[:, A_WIDTH + LANES:], p[:, C_WIN + LANES:C_GATE]], axis=1)
    ga_ref[...] = _sigmoid(p[:, C_GATE:C_QB])
    qg_ref[...] = p[:, C_QB:C_KB] * GLA_DK ** -0.5
    kb_ref[...] = p[:, C_KB:C_VB]
    vb_ref[...] = p[:, C_VB:C_RB]
    rb_ref[...] = p[:, C_RB:C_LR]
    z = jnp.dot(p[:, C_LR:C_END].astype(BF16), wgu_ref[...], preferred_element_type=F32) + bg_ref[...]
    la_ref[...] = _log_sigmoid(z) * (1.0 / GLA_GATE_NORM)


def _const_spec(shape, layer=None):
    nd = len(shape)
    if layer is None:
        return pl.BlockSpec(shape, lambda *_: (0,) * nd)
    return pl.BlockSpec((None,) + shape, lambda *_: (layer,) + (0,) * nd)


def _proj_in(x, wts, layer, tm):
    m = x.shape[0]
    widths = (A_WIDTH, KV_WIDTH, KV_WIDTH, KV_WIDTH, LANES, GK_WIDTH, GK_WIDTH, B_WIDTH, B_WIDTH, GK_WIDTH)
    row = lambda w: pl.BlockSpec((tm, w), lambda i: (i, 0))
    return pl.pallas_call(
        _proj_in_kernel,
        grid=(m // tm,),
        in_specs=[row(D_MODEL),
                  _const_spec((1, D_MODEL), layer), _const_spec((D_MODEL, C_END), layer),
                  _const_spec((1, NORM_W), layer), _const_spec((NORM_W, NORM_W)),
                  _const_spec((LANES, GK_WIDTH), layer), _const_spec((1, GK_WIDTH), layer)],
        out_specs=[row(w) for w in widths],
        out_shape=[jax.ShapeDtypeStruct((m, w), F32) for w in widths],
        compiler_params=pltpu.CompilerParams(dimension_semantics=("arbitrary",), vmem_limit_bytes=VMEM_LIMIT),
        name="proj_in",
    )(x, wts["g_attn"], wts["w_a"], wts["g_norm"], wts["ones_norm"], wts["w_gate_up"], wts["b_gate"])


def _compress_kernel(kv_ref, w1_ref, w2_ref, ones_ref, gk_ref, out_ref, pad_ref, *, t):
    for c in range(2):
        pad_ref[c, 0:t, :] = kv_ref[0, :, c * LANES:(c + 1) * LANES]
        pad_ref[c, t:t + CMP_STRIDE, :] = jnp.zeros((CMP_STRIDE, LANES), F32)
    n_blk = t // CMP_STRIDE
    kc, vc = _compress_rows(lambda c, l: pad_ref[c, pl.ds(l, n_blk, stride=CMP_STRIDE), :],
                            w1_ref, w2_ref, ones_ref, gk_ref)
    out_ref[0] = jnp.concatenate([kc, vc], axis=1)


def _compress(kv_cmp, wts, layer):
    b, t, _ = kv_cmp.shape
    n_blk = t // CMP_STRIDE
    return pl.pallas_call(
        functools.partial(_compress_kernel, t=t),
        grid=(b,),
        in_specs=[pl.BlockSpec((1, t, KV_WIDTH), lambda i: (i, 0, 0)),
                  _const_spec((CMP_BLOCK, 2, LANES, LANES), layer), _const_spec((2, LANES, LANES), layer),
                  _const_spec((LANES, LANES)), _const_spec((1, LANES), layer)],
        out_specs=pl.BlockSpec((1, n_blk, KV_WIDTH), lambda i: (i, 0, 0)),
        out_shape=jax.ShapeDtypeStruct((b, n_blk, KV_WIDTH), F32),
        scratch_shapes=[pltpu.VMEM((2, t + CMP_STRIDE, LANES), F32)],
        compiler_params=pltpu.CompilerParams(dimension_semantics=("arbitrary",), vmem_limit_bytes=VMEM_LIMIT),
        name="compress",
    )(kv_cmp, wts["w_c1"], wts["w_c2"], wts["ones128"], wts["g_k_cmp"])


def _online_chunk(carry, s, mask, v):
    m, l, acc = carry
    s = jnp.where(mask, s, NEG_INF)
    m_new = jnp.maximum(m, jnp.max(s, axis=-1, keepdims=True))
    alpha = jnp.exp(m - m_new)
    p = jnp.where(mask, jnp.exp(s - m_new), 0.0)
    return m_new, alpha * l + jnp.sum(p, axis=-1, keepdims=True), alpha * acc + _dot(p, v)


def _online_init(rows):
    return (jnp.full((rows, 1), NEG_INF, F32), jnp.zeros((rows, 1), F32), jnp.zeros((rows, HEAD_DIM), F32))


def _online_finish(carry):
    _, l, acc = carry
    return acc * jnp.where(l > 0.0, 1.0 / l, 0.0)


MASK_BIG = 2.0 ** 100
LANE_HI, LANE_LO, LANE_ONE, LANE_HOT = HEAD_DIM, HEAD_DIM + 1, HEAD_DIM + 2, HEAD_DIM + 3
SEL_SHIFT = int(math.log2(SEL_BLOCK))


def _group_halves(x, g):
    return x if g == 0 else pltpu.roll(x, HEAD_DIM, axis=1)


def _build_prompt_operands(kvc_ref, slc_ref, win_ref, kc_ref, vc_ref, ks_ref, vs_ref, kw_ref, vw_ref):
    t = slc_ref.shape[1]
    lane = lax.broadcasted_iota(jnp.int32, (t, LANES), 1)
    pos = lax.broadcasted_iota(jnp.int32, (t, LANES), 0)
    blk = lax.shift_right_logical(pos, SEL_SHIFT)
    pos_lanes = jnp.where(lane == LANE_HI, blk, jnp.where(lane == LANE_LO, pos & (SEL_BLOCK - 1),
                                                          jnp.where(lane == LANE_ONE, 1, 0)))
    aux_win = pos_lanes.astype(F32)
    aux_sel = jnp.where(lane - LANE_HOT == blk, 1, pos_lanes).astype(F32)
    n_cmp = kvc_ref.shape[1]
    lane_c = lax.broadcasted_iota(jnp.int32, (n_cmp, LANES), 1)
    blk_c = lax.broadcasted_iota(jnp.int32, (n_cmp, LANES), 0)
    aux_cmp = jnp.where(lane_c == LANE_HI, blk_c, jnp.where((lane_c == LANE_LO) | (lane_c == LANE_ONE), 1, 0)).astype(F32)
    lane_p = lax.broadcasted_iota(jnp.int32, (WINDOW, LANES), 1)
    before_seq = jnp.where(lane_p == LANE_HOT, -MASK_BIG, 0.0).astype(BF16)
    for g in range(NSA_KV_GROUPS):
        ks_ref[g] = jnp.where(lane < HEAD_DIM, _group_halves(slc_ref[0, :, 0:LANES], g), aux_sel).astype(BF16)
        vs_ref[g] = jnp.where(lane < HEAD_DIM, _group_halves(slc_ref[0, :, LANES:], g), 1.0).astype(BF16)
        kw_ref[g, 0:WINDOW, :] = before_seq
        vw_ref[g, 0:WINDOW, :] = jnp.zeros((WINDOW, LANES), BF16)
        kw_ref[g, WINDOW:, :] = jnp.where(lane < HEAD_DIM, _group_halves(win_ref[0, :, 0:LANES], g), aux_win).astype(BF16)
        vw_ref[g, WINDOW:, :] = jnp.where(lane < HEAD_DIM, _group_halves(win_ref[0, :, LANES:], g), 1.0).astype(BF16)
        kc_ref[g] = jnp.where(lane_c < HEAD_DIM, _group_halves(kvc_ref[0, :, 0:LANES], g), aux_cmp).astype(BF16)
        vc_ref[g] = jnp.where(lane_c < HEAD_DIM, _group_halves(kvc_ref[0, :, LANES:], g), 0.0).astype(BF16)


def _nsa_prompt_kernel(q_ref, ga_ref, kvc_ref, slc_ref, win_ref, fold_ref, gexp_ref, o_ref,
                       kc_ref, vc_ref, ks_ref, vs_ref, kw_ref, vw_ref, m_scr, acc_scr, *, tq, kw_sel):
    i = pl.program_id(1)

    @pl.when(i == 0)
    def _():
        _build_prompt_operands(kvc_ref, slc_ref, win_ref, kc_ref, vc_ref, ks_ref, vs_ref, kw_ref, vw_ref)

    q0 = i * tq
    q0f = q0.astype(F32)
    q = q_ref[0]
    rows = NSA_GROUP_SIZE * tq
    n_cmp = kvc_ref.shape[1]
    n_sel = slc_ref.shape[1] // SEL_BLOCK
    k_top = min(SEL_TOPK, n_sel)
    lane = lax.broadcasted_iota(jnp.int32, (tq, LANES), 1)
    row4 = lax.broadcasted_iota(jnp.int32, (rows, LANES), 0)
    lane4 = lax.broadcasted_iota(jnp.int32, (rows, LANES), 1)
    t4 = row4 & (tq - 1)
    causal_bias = jnp.where(lane4 <= t4, 0.0, -MASK_BIG)
    far_bias = jnp.where(lane4 > t4, 0.0, -MASK_BIG)
    cmp_valid = (lax.broadcasted_iota(jnp.int32, (rows, n_cmp), 1) * CMP_STRIDE + (CMP_BLOCK - 1)
                 <= q0 + (lax.broadcasted_iota(jnp.int32, (rows, n_cmp), 0) & (tq - 1)))
    hot_rows = slice(HEAD_DIM, HEAD_DIM + 8 * (-(-(LANE_HOT - HEAD_DIM + n_sel) // 8)))
    n_hot = hot_rows.stop - hot_rows.start
    blk_t = lax.broadcasted_iota(jnp.int32, (n_hot, tq), 0) - (LANE_HOT - HEAD_DIM)
    qpos_t = q0 + lax.broadcasted_iota(jnp.int32, (n_hot, tq), 1)
    is_blk = (blk_t >= 0) & (blk_t < n_sel)

    def lane_tiles_max(s):
        m = s[:, 0:LANES]
        for j in range(1, s.shape[1] // LANES):
            m = jnp.maximum(m, s[:, j * LANES:(j + 1) * LANES])
        return m

    def normalised(acc):
        return acc * (1.0 / pltpu.roll(acc, HEAD_DIM, axis=1))

    def attend_selected(qa, g):
        n_full = q0 // kw_sel
        k_last = pl.multiple_of(n_full * kw_sel, kw_sel)
        visible = lax.broadcasted_iota(jnp.int32, (rows, kw_sel), 1) <= (
            (q0 - k_last) + (lax.broadcasted_iota(jnp.int32, (rows, kw_sel), 0) & (tq - 1)))
        s_last = _dot_nt(qa, ks_ref[g, pl.ds(k_last, kw_sel), :]) + jnp.where(visible, 0.0, -MASK_BIG)
        m_scr[...] = lane_tiles_max(s_last)

        def group(ref, c):
            return ref[g, pl.ds(pl.multiple_of(c * kw_sel, kw_sel), kw_sel), :]

        def max_body(c, _):
            m_scr[...] = jnp.maximum(m_scr[...], lane_tiles_max(_dot_nt(qa, group(ks_ref, c))))
            return 0

        lax.fori_loop(0, n_full, max_body, 0)
        m = jnp.max(m_scr[...], axis=-1, keepdims=True)
        m_scr[...] = jnp.broadcast_to(m, (rows, LANES))
        acc_scr[...] = _dot(jnp.exp(s_last - m), vs_ref[g, pl.ds(k_last, kw_sel), :])

        def acc_body(c, _):
            m_w = jnp.concatenate([m_scr[...]] * (kw_sel // LANES), axis=1)
            acc_scr[...] += _dot(jnp.exp(_dot_nt(qa, group(ks_ref, c)) - m_w), group(vs_ref, c))
            return 0

        lax.fori_loop(0, n_full, acc_body, 0)
        return normalised(acc_scr[...])

    def attend_window(qa, g):
        k0 = pl.multiple_of(q0, tq)
        s = _dot_nt(qa, kw_ref[g, pl.ds(k0, WINDOW + tq), :])
        s = jnp.concatenate([s[:, 0:tq] + far_bias, s[:, tq:WINDOW], s[:, WINDOW:] + causal_bias], axis=1)
        m = jnp.max(lane_tiles_max(s), axis=-1, keepdims=True)
        return normalised(_dot(jnp.exp(s - m), vw_ref[g, pl.ds(k0, WINDOW + tq), :]))

    branch_out = [[], [], []]
    for g in range(NSA_KV_GROUPS):
        q_rows, base_rows, cbase_rows = [], [], []
        for r in range(NSA_GROUP_SIZE):
            h = g * NSA_GROUP_SIZE + r
            tile = q[:, (h // 2) * LANES:(h // 2 + 1) * LANES]
            q_rows.append(tile if h % 2 == 0 else pltpu.roll(tile, HEAD_DIM, axis=1))
            sl = ALIBI[h]
            base_rows.append(jnp.where(lane == LANE_HI, SEL_BLOCK * sl, jnp.where(
                lane == LANE_LO, sl, jnp.where(lane == LANE_ONE, -sl * q0f, 0.0))))
            cbase_rows.append(jnp.where(lane == LANE_HI, CMP_STRIDE * sl, jnp.where(
                lane == LANE_LO, (CMP_BLOCK - 1) * sl, jnp.where(lane == LANE_ONE, -sl * q0f, 0.0))))
        q4 = jnp.concatenate(q_rows, axis=0)
        base4 = jnp.concatenate(base_rows, axis=0)
        head_lanes = lane4 < HEAD_DIM

        qa = jnp.where(head_lanes, q4, jnp.concatenate(cbase_rows, axis=0)).astype(BF16)
        s = jnp.where(cmp_valid, _dot_nt(qa, kc_ref[g]), -MASK_BIG)
        m = jnp.max(s, axis=-1, keepdims=True)
        e = jnp.where(cmp_valid, jnp.exp(s - m), 0.0)
        l = jnp.sum(e, axis=-1, keepdims=True)
        p = e * jnp.where(l > 0.0, 1.0 / l, 0.0)
        branch_out[0].append(_dot(p, vc_ref[g]))
        imp = p[0:tq]
        for r in range(1, NSA_GROUP_SIZE):
            imp = imp + p[r * tq:(r + 1) * tq]

        blk_imp = _dot_wide(imp, fold_ref[...], 3, dot=lambda a, mt: _dot_nt(mt, a))[hot_rows]
        forced = (blk_t == 0) | (blk_t == lax.shift_right_logical(qpos_t, SEL_SHIFT))
        score = jnp.where(forced, FORCE_SCORE, jnp.where(blk_t * SEL_BLOCK <= qpos_t, blk_imp, NEG_INF))
        score = jnp.where(is_blk, score, BELOW_ALL_SCORES)
        ahead = jnp.zeros((n_hot, tq), F32)
        for j in range(n_sel):
            row_j = score[LANE_HOT - HEAD_DIM + j:LANE_HOT - HEAD_DIM + j + 1, :]
            ahead = ahead + jnp.where(row_j > score, 1.0, jnp.where(row_j == score, jnp.where(blk_t > j, 1.0, 0.0), 0.0))
        sel_bias_t = jnp.where(is_blk & (ahead >= k_top), -MASK_BIG, 0.0)
        sel_bias = jnp.transpose(jnp.concatenate(
            [jnp.zeros((hot_rows.start, tq), F32), sel_bias_t, jnp.zeros((LANES - hot_rows.stop, tq), F32)], axis=0))
        sel_bias4 = jnp.concatenate([sel_bias] * NSA_GROUP_SIZE, axis=0)

        qa = jnp.where(head_lanes, q4, base4 + sel_bias4).astype(BF16)
        branch_out[1].append(attend_selected(qa, g))
        qa = jnp.where(head_lanes, q4, jnp.where(lane4 == LANE_HOT, 1.0, base4)).astype(BF16)
        branch_out[2].append(attend_window(qa, g))

    ga = ga_ref[0]
    out = None
    for br in range(3):
        tiles = []
        for g in range(NSA_KV_GROUPS):
            o4 = branch_out[br][g]
            for r in range(0, NSA_GROUP_SIZE, 2):
                even, odd = o4[r * tq:(r + 1) * tq], o4[(r + 1) * tq:(r + 2) * tq]
                tiles.append(jnp.where(lane < HEAD_DIM, even, pltpu.roll(odd, HEAD_DIM, axis=1)))
        term = _dot_wide(ga, gexp_ref[br], 3) * jnp.concatenate(tiles, axis=1)
        out = term if out is None else out + term
    o_ref[0] = out


def _nsa_prompt(q, ga, kvc, kv_slc, kv_win, wts, tq):
    b, t, _ = q.shape
    n_cmp = kvc.shape[1]
    tile = lambda w: pl.BlockSpec((1, tq, w), lambda bi, i: (bi, i, 0))
    full = lambda n, w: pl.BlockSpec((1, n, w), lambda bi, i: (bi, 0, 0))
    kw_sel = min(4 * tq, t)
    assert tq == LANES and WINDOW % tq == 0 and t % kw_sel == 0 and LANE_HOT + t // SEL_BLOCK <= LANES
    aug = lambda n: pltpu.VMEM((NSA_KV_GROUPS, n, LANES), BF16)
    stat = pltpu.VMEM((NSA_GROUP_SIZE * tq, LANES), F32)
    return pl.pallas_call(
        functools.partial(_nsa_prompt_kernel, tq=tq, kw_sel=kw_sel),
        grid=(b, t // tq),
        in_specs=[tile(A_WIDTH), tile(LANES), full(n_cmp, KV_WIDTH), full(t, KV_WIDTH), full(t, KV_WIDTH),
                  _const_spec((LANES, n_cmp)), _const_spec((3, LANES, A_WIDTH))],
        out_specs=tile(A_WIDTH),
        out_shape=jax.ShapeDtypeStruct((b, t, A_WIDTH), F32),
        scratch_shapes=[aug(n_cmp), aug(n_cmp), aug(t), aug(t), aug(t + WINDOW), aug(t + WINDOW), stat, stat],
        compiler_params=pltpu.CompilerParams(dimension_semantics=("arbitrary", "arbitrary"),
                                             vmem_limit_bytes=VMEM_LIMIT),
        name="nsa_prompt",
    )(q, ga, kvc, kv_slc, kv_win, wts["fold_prompt"], wts["gate_expand"])


def _nsa_sample_kernel(pt_ref, q_ref, ga_ref, ncmp_ref, nslc_ref, nwin_ref, wbuf_ref, cmp_hbm, slc_hbm,
                       w1_ref, w2_ref, ones_ref, gk_ref, msel_ref, esel_ref,
                       o_ref, xc_ref, xs_ref, xw_ref, half_scr, sem_ref, *, layer, past, n_tok, tk):
    b = pl.program_id(0)
    nb = pl.num_programs(0)
    n_pages = pt_ref.shape[1]
    page = past // n_pages
    hpp = page // CMP_STRIDE
    n_cmp = past // CMP_STRIDE
    slot = b % 2

    def page_copies(bb, sl):
        out = []
        for p in range(n_pages):
            pg = pt_ref[bb, p]
            out.append(pltpu.make_async_copy(cmp_hbm.at[layer, pg], xc_ref.at[sl, :, pl.ds(p * hpp, hpp)], sem_ref.at[0, sl]))
            out.append(pltpu.make_async_copy(slc_hbm.at[layer, pg], xs_ref.at[sl, pl.ds(p * page, page)], sem_ref.at[1, sl]))
        return out

    @pl.when(b == 0)
    def _():
        for cp in page_copies(0, 0):
            cp.start()

    @pl.when(b + 1 < nb)
    def _():
        for cp in page_copies(b + 1, 1 - slot):
            cp.start()

    is_tok = lax.broadcasted_iota(jnp.int32, (SLOT, 1), 0) < n_tok
    xc_ref[slot, :, n_cmp:n_cmp + SUBLANES, :] = jnp.zeros((CMP_STRIDE, SUBLANES, KV_WIDTH), F32)
    for l in range(n_tok):
        xc_ref[slot, l, n_cmp:n_cmp + 1, :] = ncmp_ref[0, l:l + 1, :]
    xs_ref[slot, past:past + SLOT, :] = jnp.where(is_tok, nslc_ref[0], 0.0)
    xs_ref[slot, past + SLOT:past + LANES, :] = jnp.zeros((LANES - SLOT, KV_WIDTH), F32)
    n_buf = wbuf_ref.shape[1]
    xw_ref[0:n_buf, :] = wbuf_ref[0]
    xw_ref[n_buf:n_buf + SLOT, :] = jnp.where(is_tok, nwin_ref[0], 0.0)
    xw_ref[n_buf + SLOT:n_buf + LANES, :] = jnp.zeros((LANES - SLOT, KV_WIDTH), F32)

    for cp in page_copies(b, slot):
        cp.wait()

    q = q_ref[0]
    ga = ga_ref[0]
    rows = NSA_GROUP_SIZE * SLOT
    n_sel = past // SEL_BLOCK + 1
    q_pos = past + lax.broadcasted_iota(jnp.int32, (SLOT, 1), 0)
    q_pos4 = jnp.concatenate([q_pos] * NSA_GROUP_SIZE, axis=0)
    sel_lanes = msel_ref.shape[1]
    lane_i = lax.broadcasted_iota(jnp.int32, (1, sel_lanes), 1)
    ends = lax.broadcasted_iota(jnp.int32, (1, n_cmp), 1) * CMP_STRIDE + (CMP_BLOCK - 1)

    ab = None
    for l in range(CMP_STRIDE):
        t = jnp.dot(xc_ref[slot, l].astype(BF16), w1_ref[l], preferred_element_type=F32)
        ab = t if ab is None else ab + t
    half_scr[...] = ab[:, KV_WIDTH:]
    hid = ab[0:n_cmp, :KV_WIDTH] + half_scr[1:n_cmp + 1, :]
    kvc = jnp.dot(_gelu_tanh(hid).astype(BF16), w2_ref[...], preferred_element_type=F32)
    kc = _rms_groups(kvc[:, :LANES], ones_ref[...], gk_ref[...])
    vc = kvc[:, LANES:]
    pieces = []
    for g in range(NSA_KV_GROUPS):
        slope = _head_slopes(g, SLOT)
        qg = _stack_heads(q, g).astype(BF16)
        kcol = slice(g * HEAD_DIM, (g + 1) * HEAD_DIM)
        vcol = slice(LANES + g * HEAD_DIM, LANES + (g + 1) * HEAD_DIM)

        dist = q_pos4 - ends
        s = _dot_nt(qg, kc[:, kcol]) - slope * dist.astype(F32)
        p, o_cmp = _masked_softmax_pv(s, dist >= 0, vc[:, kcol])
        imp = p[0:SLOT]
        for r in range(1, NSA_GROUP_SIZE):
            imp = imp + p[r * SLOT:(r + 1) * SLOT]

        blk_imp = _dot_wide(imp, msel_ref[...], 3)
        sel = _topk_mask(_block_scores(blk_imp, q_pos, lane_i, n_sel), n_sel, min(SEL_TOPK, n_sel))
        sel4 = jnp.concatenate([sel] * NSA_GROUP_SIZE, axis=0).astype(BF16)

        chunks_per_slab = esel_ref.shape[0]
        carry = _online_init(rows)
        for c in range(past // tk + 1):
            k0 = c * tk
            n_k = tk if k0 < past else LANES
            dist = q_pos4 - (k0 + lax.broadcasted_iota(jnp.int32, (1, n_k), 1))
            s = _dot_nt(qg, xs_ref[slot, k0:k0 + n_k, kcol]) - slope * dist.astype(F32)
            if k0 < past:
                slab = c // chunks_per_slab
                in_sel = _dot(sel4[:, slab * LANES:(slab + 1) * LANES], esel_ref[c % chunks_per_slab])
            else:
                in_sel = sel4[:, n_sel - 1:n_sel].astype(F32)
            carry = _online_chunk(carry, s, (in_sel > 0.5) & (dist >= 0), xs_ref[slot, k0:k0 + n_k, vcol])
        o_slc = _online_finish(carry)

        k_pos = (past - n_buf) + lax.broadcasted_iota(jnp.int32, (1, n_buf + LANES), 1)
        dist = q_pos4 - k_pos
        s = _dot_nt(qg, xw_ref[:, kcol]) - slope * dist.astype(F32)
        _, o_win = _masked_softmax_pv(s, (dist >= 0) & (dist < WINDOW), xw_ref[:, vcol])

        og = _gate_cols(ga, 0, g) * o_cmp + _gate_cols(ga, 1, g) * o_slc + _gate_cols(ga, 2, g) * o_win
        pieces += [og[r * SLOT:(r + 1) * SLOT] for r in range(NSA_GROUP_SIZE)]
    o_ref[0] = jnp.concatenate(pieces, axis=1)


def _nsa_sample(q, ga, ncmp, nslc, nwin, win_buf, cache_cmp, cache_slc, page_table, wts, layer, past, n_tok):
    bs = q.shape[0]
    n_buf = win_buf.shape[2]
    n_cmp = past // CMP_STRIDE
    tk = wts["e_sel_sample"].shape[2]
    slot_spec = lambda w: pl.BlockSpec((1, SLOT, w), lambda b, pt: (b, 0, 0))
    const = lambda shape, lyr=None: (pl.BlockSpec(shape, lambda b, pt: (0,) * len(shape)) if lyr is None else
                                     pl.BlockSpec((None,) + shape, lambda b, pt: (lyr,) + (0,) * len(shape)))
    grid_spec = pltpu.PrefetchScalarGridSpec(
        num_scalar_prefetch=1,
        grid=(bs,),
        in_specs=[slot_spec(A_WIDTH), slot_spec(LANES), slot_spec(KV_WIDTH), slot_spec(KV_WIDTH), slot_spec(KV_WIDTH),
                  pl.BlockSpec((None, 1, n_buf, KV_WIDTH), lambda b, pt: (layer, b, 0, 0)),
                  pl.BlockSpec(memory_space=pl.ANY), pl.BlockSpec(memory_space=pl.ANY),
                  const((CMP_STRIDE, KV_WIDTH, 2 * KV_WIDTH), layer), const((KV_WIDTH, KV_WIDTH), layer),
                  const((LANES, LANES)), const((1, LANES), layer),
                  const(wts["m_sel_sample"].shape), const(wts["e_sel_sample"].shape)],
        out_specs=slot_spec(A_WIDTH),
        scratch_shapes=[pltpu.VMEM((2, CMP_STRIDE, n_cmp + SUBLANES, KV_WIDTH), F32),
                        pltpu.VMEM((2, past + LANES, KV_WIDTH), F32),
                        pltpu.VMEM((n_buf + LANES, KV_WIDTH), F32),
                        pltpu.VMEM((n_cmp + SUBLANES, KV_WIDTH), F32),
                        pltpu.SemaphoreType.DMA((2, 2))],
    )
    return pl.pallas_call(
        functools.partial(_nsa_sample_kernel, layer=layer, past=past, n_tok=n_tok, tk=tk),
        grid_spec=grid_spec,
        out_shape=jax.ShapeDtypeStruct((bs, SLOT, A_WIDTH), F32),
        compiler_params=pltpu.CompilerParams(dimension_semantics=("arbitrary",), vmem_limit_bytes=VMEM_LIMIT),
        name="nsa_sample",
    )(page_table, q, ga, ncmp, nslc, nwin, win_buf, cache_cmp, cache_slc,
      wts["w_c1_halves"], wts["w_c2_full"], wts["ones128"], wts["g_k_cmp"], wts["m_sel_sample"], wts["e_sel_sample"])


def _gla_kernel(qg_ref, kb_ref, vb_ref, la_ref, rb_ref, s0_ref, ggla_ref, o_ref, sout_ref, s_scr, *, chunk, n_valid):
    j = pl.program_id(1)

    @pl.when(j == 0)
    def _():
        s_scr[...] = s0_ref[0]

    tt = qg_ref.shape[1]
    ri = lax.broadcasted_iota(jnp.int32, (chunk, chunk), 0)
    ci = lax.broadcasted_iota(jnp.int32, (chunk, chunk), 1)
    tril = ri >= ci
    tril_b = jnp.where(tril, 1.0, 0.0).astype(BF16)
    ones_b = jnp.ones((chunk, GLA_DV), BF16)
    g_gla = ggla_ref[...]

    def body(c, _):
        r0 = pl.multiple_of(c * chunk, chunk)
        rows = pl.ds(r0, chunk)
        q = qg_ref[0, rows, :]
        k = kb_ref[0, rows, :]
        v = vb_ref[0, rows, :]
        la = la_ref[0, rows, :]
        if n_valid < chunk:
            ok = lax.broadcasted_iota(jnp.int32, (chunk, 1), 0) < n_valid
            q, k, v, la = (jnp.where(ok, a, 0.0) for a in (q, k, v, la))
        bcum = _dot_wide(la, tril_b, 3, dot=lambda a, m: jnp.dot(m, a, preferred_element_type=F32))
        btot_t = _dot_wide(la, ones_b, 3, dot=lambda a, m: lax.dot_general(
            a, m, (((0,), (0,)), ((), ())), preferred_element_type=F32))
        qe = q * jnp.exp(bcum)
        kd = k * jnp.exp(-bcum)
        kl = k * jnp.exp(bcum[chunk - 1:chunk, :] - bcum)
        for h in range(GLA_HEADS):
            kc = slice(h * GLA_DK, (h + 1) * GLA_DK)
            vc = slice(h * GLA_DV, (h + 1) * GLA_DV)
            att = jnp.where(tril, _dot_nt(qe[:, kc], kd[:, kc]), 0.0)
            s_h = s_scr[h]
            o = _dot(att, v[:, vc]) + _dot(qe[:, kc], s_h)
            s_scr[h] = jnp.exp(btot_t[kc, :]) * s_h + _dot_tn(kl[:, kc], v[:, vc])
            r = rb_ref[0, rows, vc]
            o_ref[0, rows, vc] = _rms_rows(o, g_gla) * (r * _sigmoid(r))
        return 0

    lax.fori_loop(0, tt // chunk, body, 0)
    sout_ref[0] = s_scr[...]


def _gla(qg, kb, vb, la, rb, s0, wts, layer, tt, chunk, n_valid):
    b, t, _ = qg.shape
    tile = lambda w: pl.BlockSpec((1, tt, w), lambda bi, j: (bi, j, 0))
    st = pl.BlockSpec((1, GLA_HEADS, GLA_DK, GLA_DV), lambda bi, j: (bi, 0, 0, 0))
    return pl.pallas_call(
        functools.partial(_gla_kernel, chunk=chunk, n_valid=n_valid),
        grid=(b, t // tt),
        in_specs=[tile(GK_WIDTH), tile(GK_WIDTH), tile(B_WIDTH), tile(GK_WIDTH), tile(B_WIDTH), st,
                  pl.BlockSpec((None, 1, GLA_DV), lambda bi, j: (layer, 0, 0))],
        out_specs=[tile(B_WIDTH), st],
        out_shape=[jax.ShapeDtypeStruct((b, t, B_WIDTH), F32),
                   jax.ShapeDtypeStruct((b, GLA_HEADS, GLA_DK, GLA_DV), F32)],
        scratch_shapes=[pltpu.VMEM((GLA_HEADS, GLA_DK, GLA_DV), F32)],
        compiler_params=pltpu.CompilerParams(dimension_semantics=("arbitrary", "arbitrary"),
                                             vmem_limit_bytes=VMEM_LIMIT),
        name="gla",
    )(qg, kb, vb, la, rb, s0, wts["g_gla"])


def _mix_ffn_kernel(*refs, tm, f_chunk, slots):
    if slots:
        (x_ref, oa_ref, ob_ref, gattn_ref, gffn_ref, wm_ref, wpa_ref, wpb_ref, wout_ref, wup_ref, wconv_ref,
         bconv_ref, wdown_ref, cb1_ref, cb2_ref, y_ref, u_ref, u_scr) = refs
    else:
        (x_ref, oa_ref, ob_ref, gattn_ref, gffn_ref, wm_ref, wpa_ref, wpb_ref, wout_ref, wup_ref, wconv_ref,
         bconv_ref, wdown_ref, y_ref, u_ref, u_scr) = refs
    x = x_ref[...]
    h = _rms_rows(x, gattn_ref[...])
    mg = jnp.dot(h.astype(BF16), wm_ref[...], preferred_element_type=F32)
    mix = (_sigmoid(mg[:, :D_MODEL]) * jnp.dot(oa_ref[...].astype(BF16), wpa_ref[...], preferred_element_type=F32)
           + _sigmoid(mg[:, D_MODEL:]) * jnp.dot(ob_ref[...].astype(BF16), wpb_ref[...], preferred_element_type=F32))
    x1 = x + jnp.dot(mix.astype(BF16), wout_ref[...], preferred_element_type=F32)
    h2 = _rms_rows(x1, gffn_ref[...]).astype(BF16)

    if slots:
        u_scr[0:SUBLANES, :] = jnp.zeros((SUBLANES, FFN_DIM), F32)
        row_in_slot = lax.broadcasted_iota(jnp.int32, (tm, 1), 0) & (SLOT - 1)
    else:
        @pl.when(pl.program_id(1) == 0)
        def _():
            u_scr[0:SUBLANES, :] = jnp.zeros((SUBLANES, FFN_DIM), F32)

    y = x1
    for c0 in range(0, FFN_DIM, f_chunk):
        fc = slice(c0, c0 + f_chunk)
        u = jnp.dot(h2, wup_ref[:, fc], preferred_element_type=F32)
        gt = jnp.dot(h2, wup_ref[:, FFN_DIM + c0:FFN_DIM + c0 + f_chunk], preferred_element_type=F32)
        u_scr[SUBLANES:SUBLANES + tm, fc] = u
        u1 = u_scr[SUBLANES - 1:SUBLANES - 1 + tm, fc]
        u2 = u_scr[SUBLANES - 2:SUBLANES - 2 + tm, fc]
        if slots:
            u1 = jnp.where(row_in_slot == 0, cb1_ref[:, fc], u1)
            u2 = jnp.where(row_in_slot < 2, cb2_ref[:, fc], u2)
            u_ref[:, fc] = u
        acc = bconv_ref[:, fc] + wconv_ref[0:1, fc] * u2 + wconv_ref[1:2, fc] * u1 + wconv_ref[2:3, fc] * u
        y = y + jnp.dot((_gelu_tanh(acc) * gt).astype(BF16), wdown_ref[fc, :], preferred_element_type=F32)
    y_ref[...] = y
    if not slots:
        tail = u_scr[tm:tm + SUBLANES, :]
        u_scr[0:SUBLANES, :] = tail
        u_ref[0] = tail


def _mix_ffn(x, oa, ob, wts, layer, tm, seq_tiles, cb=None):
    m = x.shape[0]
    slots = cb is not None
    n_seq = m // (tm * seq_tiles)
    if slots:
        idx = lambda i: (i, 0)
        grid = (m // tm,)
        sem = ("arbitrary",)
    else:
        idx = lambda b, i: (b * seq_tiles + i, 0)
        grid = (n_seq, seq_tiles)
        sem = ("arbitrary", "arbitrary")
    nidx = len(grid)

    def wspec(shape):
        return pl.BlockSpec((None,) + shape, lambda *_: (layer,) + (0,) * len(shape),
                            pipeline_mode=pl.Buffered(1))

    row = lambda w: pl.BlockSpec((tm, w), idx)
    in_specs = [row(D_MODEL), row(A_WIDTH), row(B_WIDTH),
                wspec((1, D_MODEL)), wspec((1, D_MODEL)), wspec((D_MODEL, 2 * D_MODEL)),
                wspec((A_WIDTH, D_MODEL)), wspec((B_WIDTH, D_MODEL)), wspec((D_MODEL, D_MODEL)),
                wspec((D_MODEL, 2 * FFN_DIM)), wspec((SUBLANES, FFN_DIM)), wspec((1, FFN_DIM)),
                wspec((FFN_DIM, D_MODEL))]
    args = [x, oa, ob, wts["g_attn"], wts["g_ffn"], wts["w_merge"], wts["w_pa"], wts["w_pb"], wts["w_out"],
            wts["w_up"], wts["w_conv"], wts["b_conv"], wts["w_down"]]
    if slots:
        in_specs += [row(FFN_DIM), row(FFN_DIM)]
        args += list(cb)
        u_spec = row(FFN_DIM)
        u_shape = jax.ShapeDtypeStruct((m, FFN_DIM), F32)
    else:
        u_spec = pl.BlockSpec((1, SUBLANES, FFN_DIM), lambda b, i: (b, 0, 0))
        u_shape = jax.ShapeDtypeStruct((n_seq, SUBLANES, FFN_DIM), F32)
    return pl.pallas_call(
        functools.partial(_mix_ffn_kernel, tm=tm, f_chunk=FFN_DIM // 2, slots=slots),
        grid=grid,
        in_specs=in_specs,
        out_specs=[row(D_MODEL), u_spec],
        out_shape=[jax.ShapeDtypeStruct((m, D_MODEL), F32), u_shape],
        scratch_shapes=[pltpu.VMEM((tm + SUBLANES, FFN_DIM), F32)],
        compiler_params=pltpu.CompilerParams(dimension_semantics=sem, vmem_limit_bytes=VMEM_LIMIT),
        name="mix_ffn",
    )(*args)


def _sel_fold_matrix(n_cmp_valid, n_cmp_rows, n_sel_lanes):
    per = SEL_BLOCK // CMP_STRIDE
    m = np.zeros((n_cmp_rows, n_sel_lanes), np.float32)
    for n in range(n_cmp_valid):
        m[n, n // per] += 1.0
        if n + 1 < n_cmp_valid:
            m[n, (n + 1) // per] += 1.0
    return jnp.asarray(m, BF16)


def _pack_weights(w_in, w_cmp1, w_cmp2, g_q, g_k_cmp, g_k_slc, g_k_win, w_gate_up, b_gate, g_gla, w_pa, w_pb,
                  w_out, g_attn, g_ffn, w_up, w_conv, b_conv, w_down):
    depth = w_in.shape[0]
    splits = (A_WIDTH, KV_WIDTH, KV_WIDTH, KV_WIDTH, 3 * NSA_HEADS, GK_WIDTH, GK_WIDTH, B_WIDTH, B_WIDTH,
              GLA_GATE_RANK, 2 * D_MODEL)
    offs = np.concatenate([[0], np.cumsum(splits)])
    seg = lambda k: w_in[:, :, int(offs[k]):int(offs[k + 1])]
    padc = lambda a, w: jnp.pad(a, ((0, 0), (0, 0), (0, w - a.shape[-1])))
    w_a = jnp.concatenate([seg(0), seg(1), seg(2), seg(3), padc(seg(4), LANES), seg(5), seg(6), seg(7), seg(8),
                           padc(seg(9), LANES)], axis=-1).astype(BF16)
    same_g = jnp.eye(NSA_KV_GROUPS, dtype=F32)
    w_c1 = jnp.einsum("Lclde,gG->LlcgdGe", w_cmp1, same_g).reshape(depth, CMP_BLOCK, 2, LANES, LANES).astype(BF16)
    w_c2 = jnp.einsum("Lcef,gG->LcgeGf", w_cmp2, same_g).reshape(depth, 2, LANES, LANES).astype(BF16)
    same_c = jnp.eye(2, dtype=F32)
    w1h = w_cmp1.reshape(depth, 2, 2, CMP_STRIDE, HEAD_DIM, HEAD_DIM)
    w_c1_halves = jnp.einsum("Lchlde,cC,gG->LlcgdhCGe", w1h, same_c, same_g)
    w_c1_halves = w_c1_halves.reshape(depth, CMP_STRIDE, KV_WIDTH, 2 * KV_WIDTH).astype(BF16)
    w_c2_full = jnp.einsum("Lcef,cC,gG->LcgeCGf", w_cmp2, same_c, same_g).reshape(depth, KV_WIDTH, KV_WIDTH).astype(BF16)
    grp = lambda n: jnp.asarray(np.kron(np.eye(n // HEAD_DIM), np.ones((HEAD_DIM, HEAD_DIM))), BF16)
    g_norm = jnp.concatenate([jnp.tile(g_q, (1, NSA_HEADS)), jnp.tile(g_k_slc, (1, NSA_KV_GROUPS)),
                              jnp.tile(g_k_win, (1, NSA_KV_GROUPS))], axis=-1)
    return {
        "w_a": w_a,
        "w_merge": w_in[:, :, int(offs[10]):].astype(BF16),
        "g_attn": g_attn[:, None, :], "g_ffn": g_ffn[:, None, :],
        "g_norm": g_norm[:, None, :], "ones_norm": grp(NORM_W), "ones128": grp(LANES),
        "g_k_cmp": jnp.tile(g_k_cmp, (1, NSA_KV_GROUPS))[:, None, :],
        "w_gate_up": jnp.pad(w_gate_up, ((0, 0), (0, LANES - GLA_GATE_RANK), (0, 0))).astype(BF16),
        "b_gate": b_gate[:, None, :], "g_gla": g_gla[:, None, :],
        "w_c1": w_c1, "w_c2": w_c2, "w_c1_halves": w_c1_halves, "w_c2_full": w_c2_full,
        "w_pa": w_pa.astype(BF16), "w_pb": w_pb.astype(BF16), "w_out": w_out.astype(BF16),
        "w_up": w_up.astype(BF16), "w_down": w_down.astype(BF16),
        "w_conv": jnp.pad(w_conv, ((0, 0), (0, SUBLANES - CONV_W), (0, 0))), "b_conv": b_conv[:, None, :],
    }


def _expand_matrix(n_blocks, n_keys):
    return jnp.asarray(np.arange(n_keys)[None, :] // SEL_BLOCK == np.arange(n_blocks)[:, None], BF16)


def _prompt_layer(x, wts, layer, b, t):
    tm = min(256, t)
    tq = min(128, t)
    q, kv_cmp, kv_slc, kv_win, ga, qg, kb, vb, rb, la = _proj_in(x, wts, layer, tm)
    r3 = lambda a: a.reshape(b, t, a.shape[-1])
    kvc = _compress(r3(kv_cmp), wts, layer)
    o_a = _nsa_prompt(r3(q), r3(ga), kvc, r3(kv_slc), r3(kv_win), wts, tq)
    o_b, s_new = _gla(r3(qg), r3(kb), r3(vb), r3(la), r3(rb),
                      jnp.zeros((b, GLA_HEADS, GLA_DK, GLA_DV), F32), wts, layer,
                      tt=min(512, t), chunk=math.gcd(t, GLA_CHUNK), n_valid=math.gcd(t, GLA_CHUNK))
    y, u_tail = _mix_ffn(x, o_a.reshape(b * t, A_WIDTH), o_b.reshape(b * t, B_WIDTH), wts, layer, tm, t // tm)
    kv6 = lambda a: a.reshape(b, t, 2, NSA_KV_GROUPS, HEAD_DIM)
    n_win = min(WINDOW, t)
    state = (kv6(kv_cmp), kv6(kv_slc), kv6(kv_win)[:, t - n_win:], s_new, u_tail[:, SUBLANES - (CONV_W - 1):])
    return y, state


def _sample_layer(x, wts, layer, bs, n_tok, past, cache_cmp, cache_slc, win_buf, gla_state, conv_buf, page_table):
    m = bs * SLOT
    q, kv_cmp, kv_slc, kv_win, ga, qg, kb, vb, rb, la = _proj_in(x, wts, layer, m)
    r3 = lambda a: a.reshape(bs, SLOT, a.shape[-1])
    o_a = _nsa_sample(r3(q), r3(ga), r3(kv_cmp), r3(kv_slc), r3(kv_win), win_buf, cache_cmp, cache_slc,
                      page_table, wts, layer, past, n_tok)
    o_b, s_new = _gla(r3(qg), r3(kb), r3(vb), r3(la), r3(rb), gla_state[layer], wts, layer,
                      tt=SLOT, chunk=SLOT, n_valid=n_tok)
    cb = conv_buf[layer]
    zeros = lambda n: jnp.zeros((bs, n, FFN_DIM), F32)
    cb1 = jnp.concatenate([cb[:, 1:2], zeros(SLOT - 1)], axis=1).reshape(m, FFN_DIM)
    cb2 = jnp.concatenate([cb, zeros(SLOT - 2)], axis=1).reshape(m, FFN_DIM)
    y, u = _mix_ffn(x, o_a.reshape(m, A_WIDTH), o_b.reshape(m, B_WIDTH), wts, layer, m, 1, cb=(cb1, cb2))
    kv6 = lambda a: r3(a)[:, :n_tok].reshape(bs, n_tok, 2, NSA_KV_GROUPS, HEAD_DIM)
    n_buf = win_buf.shape[2]
    win_state = jnp.concatenate([win_buf[layer], r3(kv_win)[:, :n_tok]], axis=1)[:, n_tok:]
    u_ext = jnp.concatenate([cb, u.reshape(bs, SLOT, FFN_DIM)[:, :n_tok]], axis=1)
    state = (kv6(kv_cmp), kv6(kv_slc), win_state.reshape(bs, n_buf, 2, NSA_KV_GROUPS, HEAD_DIM), s_new,
             u_ext[:, -(CONV_W - 1):])
    return y, state


def kernel(x_prompt, x_sample, cache_kv_cmp, cache_kv_slc, state_kv_win, state_gla, state_conv, page_table,
           w_in, w_cmp1, w_cmp2, g_q, g_k_cmp, g_k_slc, g_k_win, w_gate_up, b_gate, g_gla, w_pa, w_pb, w_out,
           g_attn, g_ffn, w_up, w_conv, b_conv, w_down):
    depth = w_in.shape[0]
    bp, t, _ = x_prompt.shape
    bs, n_tok, _ = x_sample.shape
    page = cache_kv_cmp.shape[2]
    past = page_table.shape[1] * page
    assert n_tok <= SLOT and t % CMP_STRIDE == 0 and past % SEL_BLOCK == 0

    wts = _pack_weights(w_in, w_cmp1, w_cmp2, g_q, g_k_cmp, g_k_slc, g_k_win, w_gate_up, b_gate, g_gla, w_pa,
                        w_pb, w_out, g_attn, g_ffn, w_up, w_conv, b_conv, w_down)
    n_cmp_p = t // CMP_STRIDE
    fold = _sel_fold_matrix(n_cmp_p - 1, n_cmp_p, LANES - LANE_HOT).T
    wts["fold_prompt"] = jnp.pad(fold, ((LANE_HOT, 0), (0, 0)))
    gate_expand = np.zeros((3, LANES, A_WIDTH), np.float32)
    for br in range(3):
        for h in range(NSA_HEADS):
            gate_expand[br, br * NSA_HEADS + h, h * HEAD_DIM:(h + 1) * HEAD_DIM] = 1.0
    wts["gate_expand"] = jnp.asarray(gate_expand, BF16)
    n_cmp_s = (past + CMP_STRIDE) // CMP_STRIDE - 1
    sel_lanes = -(-(past // SEL_BLOCK + 1) // LANES) * LANES
    tk_s = min(1024, past)
    wts["m_sel_sample"] = _sel_fold_matrix(n_cmp_s, n_cmp_s, sel_lanes)
    blk_per_chunk = tk_s // SEL_BLOCK
    wts["e_sel_sample"] = jnp.stack([_expand_matrix(LANES, LANES * SEL_BLOCK)[:, j * tk_s:(j + 1) * tk_s]
                                     for j in range(LANES // blk_per_chunk)])

    cache_cmp = cache_kv_cmp.reshape(depth, -1, page // CMP_STRIDE, CMP_STRIDE, KV_WIDTH).transpose(0, 1, 3, 2, 4)
    cache_slc = cache_kv_slc.reshape(depth, -1, page, KV_WIDTH)
    win_buf = state_kv_win.reshape(depth, bs, -1, KV_WIDTH)

    y_p = x_prompt.reshape(bp * t, D_MODEL)
    y_s = jnp.pad(x_sample, ((0, 0), (0, SLOT - n_tok), (0, 0))).reshape(bs * SLOT, D_MODEL)
    st_p, st_s = [], []
    for l in range(depth):
        y_p, sp = _prompt_layer(y_p, wts, l, bp, t)
        y_s, ss = _sample_layer(y_s, wts, l, bs, n_tok, past, cache_cmp, cache_slc, win_buf, state_gla,
                                state_conv, page_table)
        st_p.append(sp)
        st_s.append(ss)
    stack = lambda sts, k: jnp.stack([s[k] for s in sts])
    return ((y_p.reshape(bp, t, D_MODEL), y_s.reshape(bs, SLOT, D_MODEL)[:, :n_tok])
            + tuple(stack(st_p, k) for k in range(5)) + tuple(stack(st_s, k) for k in range(5)))
```

```python
import functools
import math

import numpy as np
import jax
import jax.numpy as jnp
from jax import lax
from jax.experimental import pallas as pl
from jax.experimental.pallas import tpu as pltpu

F32 = jnp.float32
BF16 = jnp.bfloat16

D_MODEL = 1024
HEAD_DIM = 64
NSA_HEADS = 8
NSA_KV_GROUPS = 2
NSA_GROUP_SIZE = NSA_HEADS // NSA_KV_GROUPS
CMP_STRIDE = 16
CMP_BLOCK = 2 * CMP_STRIDE
SEL_BLOCK = 64
SEL_TOPK = 16
WINDOW = 512
GLA_HEADS = 4
GLA_DK = 64
GLA_DV = 128
GLA_GATE_RANK = 16
GLA_GATE_NORM = 16.0
GLA_CHUNK = 64
FFN_DIM = 2816
CONV_W = 3
EPS = 1e-6
NEG_INF = -1e30
FORCE_SCORE = 1e9
BELOW_ALL_SCORES = -3e38

A_WIDTH = NSA_HEADS * HEAD_DIM
B_WIDTH = GLA_HEADS * GLA_DV
KV_WIDTH = 2 * NSA_KV_GROUPS * HEAD_DIM
GK_WIDTH = GLA_HEADS * GLA_DK
LANES = 128
SUBLANES = 8
SLOT = SUBLANES
ALIBI = tuple(float(2.0 ** (-8.0 * (h + 1) / NSA_HEADS)) for h in range(NSA_HEADS))

C_Q, C_CMP, C_SLC, C_WIN, C_GATE = 0, 512, 768, 1024, 1280
C_QB, C_KB, C_VB, C_RB, C_LR, C_END = 1408, 1664, 1920, 2432, 2944, 3072
NORM_W = A_WIDTH + 2 * LANES

VMEM_LIMIT = 56 * 1024 * 1024


def _dot(a, b):
    return jnp.dot(a.astype(BF16), b.astype(BF16), preferred_element_type=F32)


def _dot_nt(a, b):
    return lax.dot_general(a.astype(BF16), b.astype(BF16), (((1,), (1,)), ((), ())),
                           preferred_element_type=F32)


def _dot_tn(a, b):
    return lax.dot_general(a.astype(BF16), b.astype(BF16), (((0,), (0,)), ((), ())),
                           preferred_element_type=F32)


def _split_bf16(x, parts):
    out = []
    for _ in range(parts):
        p = x.astype(BF16)
        out.append(p)
        x = x - p.astype(F32)
    return out


def _dot_wide(x, m, parts, dot=None):
    dot = dot or (lambda a, b: jnp.dot(a, b, preferred_element_type=F32))
    acc = None
    for p in _split_bf16(x, parts):
        t = dot(p, m)
        acc = t if acc is None else acc + t
    return acc


def _rms_rows(x, g):
    return x * lax.rsqrt(jnp.mean(x * x, axis=-1, keepdims=True) + EPS) * g


def _rms_groups(x, ones_blockdiag, g):
    sq = x * x
    ms = jnp.concatenate([_dot_wide(sq[:, j:j + LANES], ones_blockdiag, 2)
                          for j in range(0, x.shape[1], LANES)], axis=1) * (1.0 / HEAD_DIM)
    return x * lax.rsqrt(ms + EPS) * g


def _sigmoid(x):
    return 1.0 / (1.0 + jnp.exp(-x))


def _gelu_tanh(x):
    return 0.5 * x * (1.0 + jnp.tanh(math.sqrt(2.0 / math.pi) * (x + 0.044715 * (x * x * x))))


def _log_sigmoid(x):
    return jnp.minimum(x, 0.0) - jnp.log1p(jnp.exp(-jnp.abs(x)))


def _topk_mask(score, n_cand, k):
    lane = lax.broadcasted_iota(jnp.int32, score.shape, 1)
    ahead = jnp.zeros(score.shape, F32)
    for j in range(n_cand):
        col = score[:, j:j + 1]
        ahead = ahead + jnp.where(col > score, 1.0, jnp.where(col == score, jnp.where(lane > j, 1.0, 0.0), 0.0))
    return jnp.where(lane < n_cand, jnp.where(ahead < k, 1.0, 0.0), 0.0)


def _masked_softmax(s, mask):
    s = jnp.where(mask, s, NEG_INF)
    m = jnp.max(s, axis=-1, keepdims=True)
    e = jnp.where(mask, jnp.exp(s - m), 0.0)
    l = jnp.sum(e, axis=-1, keepdims=True)
    return e * jnp.where(l > 0.0, 1.0 / l, 0.0)


def _stack_heads(q, g):
    base = g * NSA_GROUP_SIZE
    return jnp.concatenate([q[:, (base + r) * HEAD_DIM:(base + r + 1) * HEAD_DIM]
                            for r in range(NSA_GROUP_SIZE)], axis=0)


def _head_slopes(g, n):
    row = lax.broadcasted_iota(jnp.int32, (NSA_GROUP_SIZE * n, 1), 0)
    slope = jnp.full((NSA_GROUP_SIZE * n, 1), ALIBI[g * NSA_GROUP_SIZE + NSA_GROUP_SIZE - 1], F32)
    for r in reversed(range(NSA_GROUP_SIZE - 1)):
        slope = jnp.where(row < (r + 1) * n, ALIBI[g * NSA_GROUP_SIZE + r], slope)
    return slope


def _gate_cols(ga, branch, g):
    c0 = branch * NSA_HEADS + g * NSA_GROUP_SIZE
    return jnp.concatenate([ga[:, c0 + r:c0 + r + 1] for r in range(NSA_GROUP_SIZE)], axis=0)


def _block_scores(blk_imp, q_pos, lane_i, n_sel):
    forced = (lane_i == 0) | (lane_i == lax.shift_right_logical(q_pos, int(math.log2(SEL_BLOCK))))
    valid = lane_i * SEL_BLOCK <= q_pos
    score = jnp.where(forced, FORCE_SCORE, jnp.where(valid, blk_imp, NEG_INF))
    return jnp.where(lane_i < n_sel, score, BELOW_ALL_SCORES)


def _compress_rows(load_rows, w1_ref, w2_ref, ones128_ref, gk_ref):
    out = []
    for c in range(2):
        acc = None
        for l in range(CMP_BLOCK):
            t = jnp.dot(load_rows(c, l).astype(BF16), w1_ref[l, c], preferred_element_type=F32)
            acc = t if acc is None else acc + t
        out.append(jnp.dot(_gelu_tanh(acc).astype(BF16), w2_ref[c], preferred_element_type=F32))
    return _rms_groups(out[0], ones128_ref[...], gk_ref[...]), out[1]


def _proj_in_kernel(x_ref, gattn_ref, w_ref, gnorm_ref, ones_ref, wgu_ref, bg_ref,
                    q_ref, cmp_ref, slc_ref, win_ref, ga_ref, qg_ref, kb_ref, vb_ref, rb_ref, la_ref):
    h = _rms_rows(x_ref[...], gattn_ref[...])
    p = jnp.dot(h.astype(BF16), w_ref[...], preferred_element_type=F32)
    to_norm = jnp.concatenate([p[:, C_Q:C_CMP], p[:, C_SLC:C_SLC + LANES], p[:, C_WIN:C_WIN + LANES]], axis=1)
    y = _rms_groups(to_norm, ones_ref[...], gnorm_ref[...])
    q_ref[...] = y[:, :A_WIDTH] * HEAD_DIM ** -0.5
    cmp_ref[...] = p[:, C_CMP:C_SLC]
    slc_ref[...] = jnp.concatenate([y[:, A_WIDTH:A_WIDTH + LANES], p[:, C_SLC + LANES:C_WIN]], axis=1)
    win_ref[...] = jnp.concatenate([y[:, A_WIDTH + LANES:], p[:, C_WIN + LANES:C_GATE]], axis=1)
    ga_ref[...] = _sigmoid(p[:, C_GATE:C_QB])
    qg_ref[...] = p[:, C_QB:C_KB] * GLA_DK ** -0.5
    kb_ref[...] = p[:, C_KB:C_VB]
    vb_ref[...] = p[:, C_VB:C_RB]
    rb_ref[...] = p[:, C_RB:C_LR]
    z = jnp.dot(p[:, C_LR:C_END].astype(BF16), wgu_ref[...], preferred_element_type=F32) + bg_ref[...]
    la_ref[...] = _log_sigmoid(z) * (1.0 / GLA_GATE_NORM)


def _const_spec(shape, layer=None):
    nd = len(shape)
    if layer is None:
        return pl.BlockSpec(shape, lambda *_: (0,) * nd)
    return pl.BlockSpec((None,) + shape, lambda *_: (layer,) + (0,) * nd)


def _proj_in(x, wts, layer, tm):
    m = x.shape[0]
    widths = (A_WIDTH, KV_WIDTH, KV_WIDTH, KV_WIDTH, LANES, GK_WIDTH, GK_WIDTH, B_WIDTH, B_WIDTH, GK_WIDTH)
    row = lambda w: pl.BlockSpec((tm, w), lambda i: (i, 0))
    return pl.pallas_call(
        _proj_in_kernel,
        grid=(m // tm,),
        in_specs=[row(D_MODEL),
                  _const_spec((1, D_MODEL), layer), _const_spec((D_MODEL, C_END), layer),
                  _const_spec((1, NORM_W), layer), _const_spec((LANES, LANES)),
                  _const_spec((LANES, GK_WIDTH), layer), _const_spec((1, GK_WIDTH), layer)],
        out_specs=[row(w) for w in widths],
        out_shape=[jax.ShapeDtypeStruct((m, w), F32) for w in widths],
        compiler_params=pltpu.CompilerParams(dimension_semantics=("arbitrary",), vmem_limit_bytes=VMEM_LIMIT),
        name="proj_in",
    )(x, wts["g_attn"], wts["w_a"], wts["g_norm"], wts["ones128"], wts["w_gate_up"], wts["b_gate"])


def _compress_kernel(kv_ref, w1_ref, w2_ref, ones_ref, gk_ref, out_ref, pad_ref, *, t):
    for c in range(2):
        pad_ref[c, 0:t, :] = kv_ref[0, :, c * LANES:(c + 1) * LANES]
        pad_ref[c, t:t + CMP_STRIDE, :] = jnp.zeros((CMP_STRIDE, LANES), F32)
    n_blk = t // CMP_STRIDE
    kc, vc = _compress_rows(lambda c, l: pad_ref[c, pl.ds(l, n_blk, stride=CMP_STRIDE), :],
                            w1_ref, w2_ref, ones_ref, gk_ref)
    out_ref[0] = jnp.concatenate([kc, vc], axis=1)


def _compress(kv_cmp, wts, layer):
    b, t, _ = kv_cmp.shape
    n_blk = t // CMP_STRIDE
    return pl.pallas_call(
        functools.partial(_compress_kernel, t=t),
        grid=(b,),
        in_specs=[pl.BlockSpec((1, t, KV_WIDTH), lambda i: (i, 0, 0)),
                  _const_spec((CMP_BLOCK, 2, LANES, LANES), layer), _const_spec((2, LANES, LANES), layer),
                  _const_spec((LANES, LANES)), _const_spec((1, LANES), layer)],
        out_specs=pl.BlockSpec((1, n_blk, KV_WIDTH), lambda i: (i, 0, 0)),
        out_shape=jax.ShapeDtypeStruct((b, n_blk, KV_WIDTH), F32),
        scratch_shapes=[pltpu.VMEM((2, t + CMP_STRIDE, LANES), F32)],
        compiler_params=pltpu.CompilerParams(dimension_semantics=("arbitrary",), vmem_limit_bytes=VMEM_LIMIT),
        name="compress",
    )(kv_cmp, wts["w_c1"], wts["w_c2"], wts["ones128"], wts["g_k_cmp"])


MASK_BIG = 2.0 ** 100
LANE_HI, LANE_LO, LANE_ONE, LANE_HOT = HEAD_DIM, HEAD_DIM + 1, HEAD_DIM + 2, HEAD_DIM + 3
SEL_SHIFT = int(math.log2(SEL_BLOCK))


def _group_halves(x, g):
    return x if g == 0 else pltpu.roll(x, HEAD_DIM, axis=1)


def _build_prompt_operands(kvc_ref, slc_ref, win_ref, kc_ref, vc_ref, ks_ref, vs_ref, kw_ref, vw_ref):
    t = slc_ref.shape[1]
    lane = lax.broadcasted_iota(jnp.int32, (t, LANES), 1)
    pos = lax.broadcasted_iota(jnp.int32, (t, LANES), 0)
    blk = lax.shift_right_logical(pos, SEL_SHIFT)
    pos_lanes = jnp.where(lane == LANE_HI, blk, jnp.where(lane == LANE_LO, pos & (SEL_BLOCK - 1),
                                                          jnp.where(lane == LANE_ONE, 1, 0)))
    aux_win = pos_lanes.astype(F32)
    aux_sel = jnp.where(lane - LANE_HOT == blk, 1, pos_lanes).astype(F32)
    n_cmp = kvc_ref.shape[1]
    lane_c = lax.broadcasted_iota(jnp.int32, (n_cmp, LANES), 1)
    blk_c = lax.broadcasted_iota(jnp.int32, (n_cmp, LANES), 0)
    aux_cmp = jnp.where(lane_c == LANE_HI, blk_c, jnp.where((lane_c == LANE_LO) | (lane_c == LANE_ONE), 1, 0)).astype(F32)
    lane_p = lax.broadcasted_iota(jnp.int32, (WINDOW, LANES), 1)
    before_seq = jnp.where(lane_p == LANE_HOT, -MASK_BIG, 0.0).astype(BF16)
    for g in range(NSA_KV_GROUPS):
        ks_ref[g] = jnp.where(lane < HEAD_DIM, _group_halves(slc_ref[0, :, 0:LANES], g), aux_sel).astype(BF16)
        vs_ref[g] = jnp.where(lane < HEAD_DIM, _group_halves(slc_ref[0, :, LANES:], g), 1.0).astype(BF16)
        kw_ref[g, 0:WINDOW, :] = before_seq
        vw_ref[g, 0:WINDOW, :] = jnp.zeros((WINDOW, LANES), BF16)
        kw_ref[g, WINDOW:, :] = jnp.where(lane < HEAD_DIM, _group_halves(win_ref[0, :, 0:LANES], g), aux_win).astype(BF16)
        vw_ref[g, WINDOW:, :] = jnp.where(lane < HEAD_DIM, _group_halves(win_ref[0, :, LANES:], g), 1.0).astype(BF16)
        kc_ref[g] = jnp.where(lane_c < HEAD_DIM, _group_halves(kvc_ref[0, :, 0:LANES], g), aux_cmp).astype(BF16)
        vc_ref[g] = jnp.where(lane_c < HEAD_DIM, _group_halves(kvc_ref[0, :, LANES:], g), 0.0).astype(BF16)


def _nsa_prompt_kernel(q_ref, ga_ref, kvc_ref, slc_ref, win_ref, fold_ref, gexp_ref, o_ref,
                       kc_ref, vc_ref, ks_ref, vs_ref, kw_ref, vw_ref, m_scr, acc_scr, *, tq, kw_sel):
    i = pl.program_id(1)

    @pl.when(i == 0)
    def _():
        _build_prompt_operands(kvc_ref, slc_ref, win_ref, kc_ref, vc_ref, ks_ref, vs_ref, kw_ref, vw_ref)

    q0 = i * tq
    q0f = q0.astype(F32)
    q = q_ref[0]
    rows = NSA_GROUP_SIZE * tq
    n_cmp = kvc_ref.shape[1]
    n_sel = slc_ref.shape[1] // SEL_BLOCK
    k_top = min(SEL_TOPK, n_sel)
    lane = lax.broadcasted_iota(jnp.int32, (tq, LANES), 1)
    row4 = lax.broadcasted_iota(jnp.int32, (rows, LANES), 0)
    lane4 = lax.broadcasted_iota(jnp.int32, (rows, LANES), 1)
    t4 = row4 & (tq - 1)
    causal_bias = jnp.where(lane4 <= t4, 0.0, -MASK_BIG)
    far_bias = jnp.where(lane4 > t4, 0.0, -MASK_BIG)
    cmp_valid = (lax.broadcasted_iota(jnp.int32, (rows, n_cmp), 1) * CMP_STRIDE + (CMP_BLOCK - 1)
                 <= q0 + (lax.broadcasted_iota(jnp.int32, (rows, n_cmp), 0) & (tq - 1)))
    hot_rows = slice(HEAD_DIM, HEAD_DIM + 8 * (-(-(LANE_HOT - HEAD_DIM + n_sel) // 8)))
    n_hot = hot_rows.stop - hot_rows.start
    blk_t = lax.broadcasted_iota(jnp.int32, (n_hot, tq), 0) - (LANE_HOT - HEAD_DIM)
    qpos_t = q0 + lax.broadcasted_iota(jnp.int32, (n_hot, tq), 1)
    is_blk = (blk_t >= 0) & (blk_t < n_sel)

    def lane_tiles_max(s):
        m = s[:, 0:LANES]
        for j in range(1, s.shape[1] // LANES):
            m = jnp.maximum(m, s[:, j * LANES:(j + 1) * LANES])
        return m

    def normalised(acc):
        return acc * (1.0 / pltpu.roll(acc, HEAD_DIM, axis=1))

    def attend_selected(qa, g):
        n_full = q0 // kw_sel
        k_last = pl.multiple_of(n_full * kw_sel, kw_sel)
        visible = lax.broadcasted_iota(jnp.int32, (rows, kw_sel), 1) <= (
            (q0 - k_last) + (lax.broadcasted_iota(jnp.int32, (rows, kw_sel), 0) & (tq - 1)))
        s_last = _dot_nt(qa, ks_ref[g, pl.ds(k_last, kw_sel), :]) + jnp.where(visible, 0.0, -MASK_BIG)
        m_scr[...] = lane_tiles_max(s_last)

        def group(ref, c):
            return ref[g, pl.ds(pl.multiple_of(c * kw_sel, kw_sel), kw_sel), :]

        def max_body(c, _):
            m_scr[...] = jnp.maximum(m_scr[...], lane_tiles_max(_dot_nt(qa, group(ks_ref, c))))
            return 0

        lax.fori_loop(0, n_full, max_body, 0)
        m = jnp.max(m_scr[...], axis=-1, keepdims=True)
        m_scr[...] = jnp.broadcast_to(m, (rows, LANES))
        acc_scr[...] = _dot(jnp.exp(s_last - m), vs_ref[g, pl.ds(k_last, kw_sel), :])

        def acc_body(c, _):
            m_w = jnp.concatenate([m_scr[...]] * (kw_sel // LANES), axis=1)
            acc_scr[...] += _dot(jnp.exp(_dot_nt(qa, group(ks_ref, c)) - m_w), group(vs_ref, c))
            return 0

        lax.fori_loop(0, n_full, acc_body, 0)
        return normalised(acc_scr[...])

    def attend_window(qa, g):
        k0 = pl.multiple_of(q0, tq)
        s = _dot_nt(qa, kw_ref[g, pl.ds(k0, WINDOW + tq), :])
        s = jnp.concatenate([s[:, 0:tq] + far_bias, s[:, tq:WINDOW], s[:, WINDOW:] + causal_bias], axis=1)
        m = jnp.max(lane_tiles_max(s), axis=-1, keepdims=True)
        return normalised(_dot(jnp.exp(s - m), vw_ref[g, pl.ds(k0, WINDOW + tq), :]))

    branch_out = [[], [], []]
    for g in range(NSA_KV_GROUPS):
        q_rows, base_rows, cbase_rows = [], [], []
        for r in range(NSA_GROUP_SIZE):
            h = g * NSA_GROUP_SIZE + r
            tile = q[:, (h // 2) * LANES:(h // 2 + 1) * LANES]
            q_rows.append(tile if h % 2 == 0 else pltpu.roll(tile, HEAD_DIM, axis=1))
            sl = ALIBI[h]
            base_rows.append(jnp.where(lane == LANE_HI, SEL_BLOCK * sl, jnp.where(
                lane == LANE_LO, sl, jnp.where(lane == LANE_ONE, -sl * q0f, 0.0))))
            cbase_rows.append(jnp.where(lane == LANE_HI, CMP_STRIDE * sl, jnp.where(
                lane == LANE_LO, (CMP_BLOCK - 1) * sl, jnp.where(lane == LANE_ONE, -sl * q0f, 0.0))))
        q4 = jnp.concatenate(q_rows, axis=0)
        base4 = jnp.concatenate(base_rows, axis=0)
        head_lanes = lane4 < HEAD_DIM

        qa = jnp.where(head_lanes, q4, jnp.concatenate(cbase_rows, axis=0)).astype(BF16)
        s = jnp.where(cmp_valid, _dot_nt(qa, kc_ref[g]), -MASK_BIG)
        m = jnp.max(s, axis=-1, keepdims=True)
        e = jnp.where(cmp_valid, jnp.exp(s - m), 0.0)
        l = jnp.sum(e, axis=-1, keepdims=True)
        p = e * jnp.where(l > 0.0, 1.0 / l, 0.0)
        branch_out[0].append(_dot(p, vc_ref[g]))
        imp = p[0:tq]
        for r in range(1, NSA_GROUP_SIZE):
            imp = imp + p[r * tq:(r + 1) * tq]

        blk_imp = _dot_wide(imp, fold_ref[...], 3, dot=lambda a, mt: _dot_nt(mt, a))[hot_rows]
        forced = (blk_t == 0) | (blk_t == lax.shift_right_logical(qpos_t, SEL_SHIFT))
        score = jnp.where(forced, FORCE_SCORE, jnp.where(blk_t * SEL_BLOCK <= qpos_t, blk_imp, NEG_INF))
        score = jnp.where(is_blk, score, BELOW_ALL_SCORES)
        ahead = jnp.zeros((n_hot, tq), F32)
        for j in range(n_sel):
            row_j = score[LANE_HOT - HEAD_DIM + j:LANE_HOT - HEAD_DIM + j + 1, :]
            ahead = ahead + jnp.where(row_j > score, 1.0, jnp.where(row_j == score, jnp.where(blk_t > j, 1.0, 0.0), 0.0))
        sel_bias_t = jnp.where(is_blk & (ahead >= k_top), -MASK_BIG, 0.0)
        sel_bias = jnp.transpose(jnp.concatenate(
            [jnp.zeros((hot_rows.start, tq), F32), sel_bias_t, jnp.zeros((LANES - hot_rows.stop, tq), F32)], axis=0))
        sel_bias4 = jnp.concatenate([sel_bias] * NSA_GROUP_SIZE, axis=0)

        qa = jnp.where(head_lanes, q4, base4 + sel_bias4).astype(BF16)
        branch_out[1].append(attend_selected(qa, g))
        qa = jnp.where(head_lanes, q4, jnp.where(lane4 == LANE_HOT, 1.0, base4)).astype(BF16)
        branch_out[2].append(attend_window(qa, g))

    ga = ga_ref[0]
    out = None
    for br in range(3):
        tiles = []
        for g in range(NSA_KV_GROUPS):
            o4 = branch_out[br][g]
            for r in range(0, NSA_GROUP_SIZE, 2):
                even, odd = o4[r * tq:(r + 1) * tq], o4[(r + 1) * tq:(r + 2) * tq]
                tiles.append(jnp.where(lane < HEAD_DIM, even, pltpu.roll(odd, HEAD_DIM, axis=1)))
        term = _dot_wide(ga, gexp_ref[br], 3) * jnp.concatenate(tiles, axis=1)
        out = term if out is None else out + term
    o_ref[0] = out


def _nsa_prompt(q, ga, kvc, kv_slc, kv_win, wts, tq):
    b, t, _ = q.shape
    n_cmp = kvc.shape[1]
    tile = lambda w: pl.BlockSpec((1, tq, w), lambda bi, i: (bi, i, 0))
    full = lambda n, w: pl.BlockSpec((1, n, w), lambda bi, i: (bi, 0, 0))
    kw_sel = min(4 * tq, t)
    assert tq == LANES and WINDOW % tq == 0 and t % kw_sel == 0 and LANE_HOT + t // SEL_BLOCK <= LANES
    aug = lambda n: pltpu.VMEM((NSA_KV_GROUPS, n, LANES), BF16)
    stat = pltpu.VMEM((NSA_GROUP_SIZE * tq, LANES), F32)
    return pl.pallas_call(
        functools.partial(_nsa_prompt_kernel, tq=tq, kw_sel=kw_sel),
        grid=(b, t // tq),
        in_specs=[tile(A_WIDTH), tile(LANES), full(n_cmp, KV_WIDTH), full(t, KV_WIDTH), full(t, KV_WIDTH),
                  _const_spec((LANES, n_cmp)), _const_spec((3, LANES, A_WIDTH))],
        out_specs=tile(A_WIDTH),
        out_shape=jax.ShapeDtypeStruct((b, t, A_WIDTH), F32),
        scratch_shapes=[aug(n_cmp), aug(n_cmp), aug(t), aug(t), aug(t + WINDOW), aug(t + WINDOW), stat, stat],
        compiler_params=pltpu.CompilerParams(dimension_semantics=("arbitrary", "arbitrary"),
                                             vmem_limit_bytes=VMEM_LIMIT),
        name="nsa_prompt",
    )(q, ga, kvc, kv_slc, kv_win, wts["fold_prompt"], wts["gate_expand"])


def _nsa_sample_kernel(pt_ref, q_ref, ga_ref, ncmp_ref, nslc_ref, nwin_ref, wbuf_ref, cmp_hbm, slc_hbm,
                       w1_ref, w2_ref, ones_ref, gk_ref, msel_ref, esel_ref,
                       o_ref, xc_ref, xs_ref, half_scr, sem_ref, *, layer, past, n_tok, tk):
    b = pl.program_id(0)
    nb = pl.num_programs(0)
    n_pages = pt_ref.shape[1]
    page = past // n_pages
    hpp = page // CMP_STRIDE
    n_cmp = past // CMP_STRIDE
    slot = b % 2

    def page_copies(bb, sl):
        out = []
        for p in range(n_pages):
            pg = pt_ref[bb, p]
            out.append(pltpu.make_async_copy(cmp_hbm.at[layer, pg], xc_ref.at[sl, :, pl.ds(p * hpp, hpp)], sem_ref.at[0, sl]))
            out.append(pltpu.make_async_copy(slc_hbm.at[layer, pg], xs_ref.at[sl, :, :, :, pl.ds(p * page, page)],
                                             sem_ref.at[1, sl]))
        return out

    def rows_on_lanes(new_ref):
        new = jnp.where(is_tok, new_ref[0], 0.0)
        return jnp.transpose(jnp.concatenate([new, jnp.zeros((LANES - SLOT, KV_WIDTH), F32)], axis=0))

    @pl.when(b == 0)
    def _():
        for cp in page_copies(0, 0):
            cp.start()

    @pl.when(b + 1 < nb)
    def _():
        for cp in page_copies(b + 1, 1 - slot):
            cp.start()

    is_tok = lax.broadcasted_iota(jnp.int32, (SLOT, 1), 0) < n_tok
    xc_ref[slot, :, n_cmp:n_cmp + SUBLANES, :] = jnp.zeros((CMP_STRIDE, SUBLANES, KV_WIDTH), F32)
    for l in range(n_tok):
        xc_ref[slot, l, n_cmp:n_cmp + 1, :] = ncmp_ref[0, l:l + 1, :]
    new_slc = rows_on_lanes(nslc_ref)
    new_win = rows_on_lanes(nwin_ref)
    for c in range(2):
        for g in range(NSA_KV_GROUPS):
            r0 = (c * NSA_KV_GROUPS + g) * HEAD_DIM
            xs_ref[slot, c, g, :, past:past + LANES] = new_slc[r0:r0 + HEAD_DIM, :]
    n_buf = wbuf_ref.shape[-1]

    for cp in page_copies(b, slot):
        cp.wait()

    q = q_ref[0]
    ga = ga_ref[0]
    rows = NSA_GROUP_SIZE * SLOT
    n_sel = past // SEL_BLOCK + 1
    q_pos = past + lax.broadcasted_iota(jnp.int32, (SLOT, 1), 0)
    q_pos4 = jnp.concatenate([q_pos] * NSA_GROUP_SIZE, axis=0)
    sel_lanes = msel_ref.shape[1]
    lane_i = lax.broadcasted_iota(jnp.int32, (1, sel_lanes), 1)
    ends = lax.broadcasted_iota(jnp.int32, (1, n_cmp), 1) * CMP_STRIDE + (CMP_BLOCK - 1)

    ab = None
    for l in range(CMP_STRIDE):
        t = jnp.dot(xc_ref[slot, l].astype(BF16), w1_ref[l], preferred_element_type=F32)
        ab = t if ab is None else ab + t
    half_scr[...] = ab[:, KV_WIDTH:]
    hid = ab[0:n_cmp, :KV_WIDTH] + half_scr[1:n_cmp + 1, :]
    kvc = jnp.dot(_gelu_tanh(hid).astype(BF16), w2_ref[...], preferred_element_type=F32)
    kc = _rms_groups(kvc[:, :LANES], ones_ref[...], gk_ref[...])
    vc = kvc[:, LANES:]
    pieces = []
    for g in range(NSA_KV_GROUPS):
        slope = _head_slopes(g, SLOT)
        qg = _stack_heads(q, g).astype(BF16)
        kcol = slice(g * HEAD_DIM, (g + 1) * HEAD_DIM)
        vcol = slice(LANES + g * HEAD_DIM, LANES + (g + 1) * HEAD_DIM)

        dist = q_pos4 - ends
        s = _dot_nt(qg, kc[:, kcol]) - slope * dist.astype(F32)
        p = _masked_softmax(s, dist >= 0)
        o_cmp = _dot(p, vc[:, kcol])
        imp = p[0:SLOT]
        for r in range(1, NSA_GROUP_SIZE):
            imp = imp + p[r * SLOT:(r + 1) * SLOT]

        blk_imp = _dot_wide(imp, msel_ref[...], 3)
        sel = _topk_mask(_block_scores(blk_imp, q_pos, lane_i, n_sel), n_sel, min(SEL_TOPK, n_sel))
        sel4 = jnp.concatenate([sel] * NSA_GROUP_SIZE, axis=0).astype(BF16)

        chunks_per_slab = esel_ref.shape[0]
        bounds = [(c * tk, tk if c * tk < past else LANES) for c in range(past // tk + 1)]
        scores = []
        for c, (k0, n_k) in enumerate(bounds):
            dist = q_pos4 - (k0 + lax.broadcasted_iota(jnp.int32, (1, n_k), 1))
            s = _dot(qg, xs_ref[slot, 0, g, :, k0:k0 + n_k]) - slope * dist.astype(F32)
            if k0 < past:
                slab = c // chunks_per_slab
                in_sel = _dot(sel4[:, slab * LANES:(slab + 1) * LANES], esel_ref[c % chunks_per_slab])
            else:
                in_sel = sel4[:, n_sel - 1:n_sel].astype(F32)
            scores.append(jnp.where((in_sel > 0.5) & (dist >= 0), s, NEG_INF))
        m = functools.reduce(jnp.maximum, [jnp.max(s, axis=-1, keepdims=True) for s in scores])
        l = jnp.zeros((rows, 1), F32)
        o_slc = jnp.zeros((rows, HEAD_DIM), F32)
        for s, (k0, n_k) in zip(scores, bounds):
            p = jnp.exp(s - m)
            l = l + jnp.sum(p, axis=-1, keepdims=True)
            o_slc = o_slc + _dot_nt(p, xs_ref[slot, 1, g, :, k0:k0 + n_k])
        o_slc = o_slc * (1.0 / l)

        k_pos = (past - n_buf) + lax.broadcasted_iota(jnp.int32, (1, n_buf + LANES), 1)
        dist = q_pos4 - k_pos
        s = jnp.concatenate([_dot(qg, wbuf_ref[0, 0, g]), _dot(qg, new_win[kcol, :])], axis=1) - slope * dist.astype(F32)
        p = _masked_softmax(s, (dist >= 0) & (dist < WINDOW))
        o_win = _dot_nt(p[:, :n_buf], wbuf_ref[0, 1, g]) + _dot_nt(p[:, n_buf:], new_win[vcol, :])

        og = _gate_cols(ga, 0, g) * o_cmp + _gate_cols(ga, 1, g) * o_slc + _gate_cols(ga, 2, g) * o_win
        pieces += [og[r * SLOT:(r + 1) * SLOT] for r in range(NSA_GROUP_SIZE)]
    o_ref[0] = jnp.concatenate(pieces, axis=1)


def _nsa_sample(q, ga, ncmp, nslc, nwin, win_buf, cache_cmp, cache_slc, page_table, wts, layer, past, n_tok):
    bs = q.shape[0]
    n_buf = win_buf.shape[-1]
    n_cmp = past // CMP_STRIDE
    tk = wts["e_sel_sample"].shape[2]
    slot_spec = lambda w: pl.BlockSpec((1, SLOT, w), lambda b, pt: (b, 0, 0))
    const = lambda shape, lyr=None: (pl.BlockSpec(shape, lambda b, pt: (0,) * len(shape)) if lyr is None else
                                     pl.BlockSpec((None,) + shape, lambda b, pt: (lyr,) + (0,) * len(shape)))
    grid_spec = pltpu.PrefetchScalarGridSpec(
        num_scalar_prefetch=1,
        grid=(bs,),
        in_specs=[slot_spec(A_WIDTH), slot_spec(LANES), slot_spec(KV_WIDTH), slot_spec(KV_WIDTH), slot_spec(KV_WIDTH),
                  pl.BlockSpec((None, 1, 2, NSA_KV_GROUPS, HEAD_DIM, n_buf), lambda b, pt: (layer, b, 0, 0, 0, 0)),
                  pl.BlockSpec(memory_space=pl.ANY), pl.BlockSpec(memory_space=pl.ANY),
                  const((CMP_STRIDE, KV_WIDTH, 2 * KV_WIDTH), layer), const((KV_WIDTH, KV_WIDTH), layer),
                  const((LANES, LANES)), const((1, LANES), layer),
                  const(wts["m_sel_sample"].shape), const(wts["e_sel_sample"].shape)],
        out_specs=slot_spec(A_WIDTH),
        scratch_shapes=[pltpu.VMEM((2, CMP_STRIDE, n_cmp + SUBLANES, KV_WIDTH), F32),
                        pltpu.VMEM((2, 2, NSA_KV_GROUPS, HEAD_DIM, past + LANES), F32),
                        pltpu.VMEM((n_cmp + SUBLANES, KV_WIDTH), F32),
                        pltpu.SemaphoreType.DMA((2, 2))],
    )
    return pl.pallas_call(
        functools.partial(_nsa_sample_kernel, layer=layer, past=past, n_tok=n_tok, tk=tk),
        grid_spec=grid_spec,
        out_shape=jax.ShapeDtypeStruct((bs, SLOT, A_WIDTH), F32),
        compiler_params=pltpu.CompilerParams(dimension_semantics=("arbitrary",), vmem_limit_bytes=VMEM_LIMIT),
        name="nsa_sample",
    )(page_table, q, ga, ncmp, nslc, nwin, win_buf, cache_cmp, cache_slc,
      wts["w_c1_halves"], wts["w_c2_full"], wts["ones128"], wts["g_k_cmp"], wts["m_sel_sample"], wts["e_sel_sample"])


def _gla_kernel(qg_ref, kb_ref, vb_ref, la_ref, rb_ref, s0_ref, ggla_ref, o_ref, sout_ref, s_scr, *, chunk, n_valid):
    j = pl.program_id(1)

    @pl.when(j == 0)
    def _():
        s_scr[...] = s0_ref[0]

    tt = qg_ref.shape[1]
    n_ch = tt // chunk
    shift = int(math.log2(chunk))
    ri = lax.broadcasted_iota(jnp.int32, (tt, tt), 0)
    ci = lax.broadcasted_iota(jnp.int32, (tt, tt), 1)
    same_chunk = lax.shift_right_logical(ri, shift) == lax.shift_right_logical(ci, shift)
    causal = same_chunk & (ri >= ci)
    chunk_of_row = lax.shift_right_logical(lax.broadcasted_iota(jnp.int32, (tt, LANES), 0), shift)
    chunk_lane = lax.broadcasted_iota(jnp.int32, (tt, LANES), 1)
    as_b = lambda mask: jnp.where(mask, 1.0, 0.0).astype(BF16)
    left = lambda a, m: jnp.dot(m, a, preferred_element_type=F32)
    g_gla = ggla_ref[...]

    q, k, v, la = qg_ref[0], kb_ref[0], vb_ref[0], la_ref[0]
    if n_valid < chunk:
        ok = (lax.broadcasted_iota(jnp.int32, (tt, 1), 0) & (chunk - 1)) < n_valid
        q, k, v, la = (jnp.where(ok, a, 0.0) for a in (q, k, v, la))
    bcum = _dot_wide(la, as_b(causal), 3, dot=left)
    bend = jnp.concatenate([jnp.broadcast_to(bcum[(c + 1) * chunk - 1:(c + 1) * chunk, :], (chunk, GK_WIDTH))
                            for c in range(n_ch)], axis=0)
    bend_t = _dot_wide(la, as_b(chunk_of_row == chunk_lane), 3, dot=lambda a, m: lax.dot_general(
        a, m, (((0,), (0,)), ((), ())), preferred_element_type=F32))
    qe = q * jnp.exp(bcum)
    kd = k * jnp.exp(-bcum)
    kl = k * jnp.exp(bend - bcum)
    for h in range(GLA_HEADS):
        kc = slice(h * GLA_DK, (h + 1) * GLA_DK)
        vc = slice(h * GLA_DV, (h + 1) * GLA_DV)
        o_intra = _dot(jnp.where(causal, _dot_nt(qe[:, kc], kd[:, kc]), 0.0), v[:, vc])
        s_h = s_scr[h]
        o_inter = []
        for c in range(n_ch):
            rows = slice(c * chunk, (c + 1) * chunk)
            o_inter.append(_dot(qe[rows, kc], s_h))
            s_h = jnp.exp(bend_t[kc, c:c + 1]) * s_h + _dot_tn(kl[rows, kc], v[rows, vc])
        s_scr[h] = s_h
        o = o_intra + jnp.concatenate(o_inter, axis=0)
        r = rb_ref[0, :, vc]
        o_ref[0, :, vc] = _rms_rows(o, g_gla) * (r * _sigmoid(r))
    sout_ref[0] = s_scr[...]


def _gla(qg, kb, vb, la, rb, s0, wts, layer, tt, chunk, n_valid):
    b, t, _ = qg.shape
    tile = lambda w: pl.BlockSpec((1, tt, w), lambda bi, j: (bi, j, 0))
    st = pl.BlockSpec((1, GLA_HEADS, GLA_DK, GLA_DV), lambda bi, j: (bi, 0, 0, 0))
    return pl.pallas_call(
        functools.partial(_gla_kernel, chunk=chunk, n_valid=n_valid),
        grid=(b, t // tt),
        in_specs=[tile(GK_WIDTH), tile(GK_WIDTH), tile(B_WIDTH), tile(GK_WIDTH), tile(B_WIDTH), st,
                  pl.BlockSpec((None, 1, GLA_DV), lambda bi, j: (layer, 0, 0))],
        out_specs=[tile(B_WIDTH), st],
        out_shape=[jax.ShapeDtypeStruct((b, t, B_WIDTH), F32),
                   jax.ShapeDtypeStruct((b, GLA_HEADS, GLA_DK, GLA_DV), F32)],
        scratch_shapes=[pltpu.VMEM((GLA_HEADS, GLA_DK, GLA_DV), F32)],
        compiler_params=pltpu.CompilerParams(dimension_semantics=("arbitrary", "arbitrary"),
                                             vmem_limit_bytes=VMEM_LIMIT),
        name="gla",
    )(qg, kb, vb, la, rb, s0, wts["g_gla"])


def _mix_ffn_kernel(*refs, tm, f_chunk, slots):
    if slots:
        (x_ref, oa_ref, ob_ref, gattn_ref, gffn_ref, wm_ref, wpa_ref, wpb_ref, wout_ref, wup_ref, wconv_ref,
         bconv_ref, wdown_ref, cb1_ref, cb2_ref, y_ref, u_ref, u_scr) = refs
    else:
        (x_ref, oa_ref, ob_ref, gattn_ref, gffn_ref, wm_ref, wpa_ref, wpb_ref, wout_ref, wup_ref, wconv_ref,
         bconv_ref, wdown_ref, y_ref, u_ref, u_scr) = refs
    x = x_ref[...]
    h = _rms_rows(x, gattn_ref[...])
    mg = jnp.dot(h.astype(BF16), wm_ref[...], preferred_element_type=F32)
    mix = (_sigmoid(mg[:, :D_MODEL]) * jnp.dot(oa_ref[...].astype(BF16), wpa_ref[...], preferred_element_type=F32)
           + _sigmoid(mg[:, D_MODEL:]) * jnp.dot(ob_ref[...].astype(BF16), wpb_ref[...], preferred_element_type=F32))
    x1 = x + jnp.dot(mix.astype(BF16), wout_ref[...], preferred_element_type=F32)
    h2 = _rms_rows(x1, gffn_ref[...]).astype(BF16)

    if slots:
        u_scr[0:SUBLANES, :] = jnp.zeros((SUBLANES, FFN_DIM), F32)
        row_in_slot = lax.broadcasted_iota(jnp.int32, (tm, 1), 0) & (SLOT - 1)
    else:
        @pl.when(pl.program_id(1) == 0)
        def _():
            u_scr[0:SUBLANES, :] = jnp.zeros((SUBLANES, FFN_DIM), F32)

    y = x1
    for c0 in range(0, FFN_DIM, f_chunk):
        fc = slice(c0, c0 + f_chunk)
        u = jnp.dot(h2, wup_ref[:, fc], preferred_element_type=F32)
        gt = jnp.dot(h2, wup_ref[:, FFN_DIM + c0:FFN_DIM + c0 + f_chunk], preferred_element_type=F32)
        u_scr[SUBLANES:SUBLANES + tm, fc] = u
        u1 = u_scr[SUBLANES - 1:SUBLANES - 1 + tm, fc]
        u2 = u_scr[SUBLANES - 2:SUBLANES - 2 + tm, fc]
        if slots:
            u1 = jnp.where(row_in_slot == 0, cb1_ref[:, fc], u1)
            u2 = jnp.where(row_in_slot < 2, cb2_ref[:, fc], u2)
            u_ref[:, fc] = u
        acc = bconv_ref[:, fc] + wconv_ref[0:1, fc] * u2 + wconv_ref[1:2, fc] * u1 + wconv_ref[2:3, fc] * u
        y = y + jnp.dot((_gelu_tanh(acc) * gt).astype(BF16), wdown_ref[fc, :], preferred_element_type=F32)
    y_ref[...] = y
    if not slots:
        tail = u_scr[tm:tm + SUBLANES, :]
        u_scr[0:SUBLANES, :] = tail
        u_ref[0] = tail


def _mix_ffn(x, oa, ob, wts, layer, tm, seq_tiles, cb=None):
    m = x.shape[0]
    slots = cb is not None
    n_seq = m // (tm * seq_tiles)
    if slots:
        idx = lambda i: (i, 0)
        grid = (m // tm,)
        sem = ("arbitrary",)
    else:
        idx = lambda b, i: (b * seq_tiles + i, 0)
        grid = (n_seq, seq_tiles)
        sem = ("arbitrary", "arbitrary")
    nidx = len(grid)

    def wspec(shape):
        return pl.BlockSpec((None,) + shape, lambda *_: (layer,) + (0,) * len(shape),
                            pipeline_mode=pl.Buffered(1))

    row = lambda w: pl.BlockSpec((tm, w), idx)
    in_specs = [row(D_MODEL), row(A_WIDTH), row(B_WIDTH),
                wspec((1, D_MODEL)), wspec((1, D_MODEL)), wspec((D_MODEL, 2 * D_MODEL)),
                wspec((A_WIDTH, D_MODEL)), wspec((B_WIDTH, D_MODEL)), wspec((D_MODEL, D_MODEL)),
                wspec((D_MODEL, 2 * FFN_DIM)), wspec((SUBLANES, FFN_DIM)), wspec((1, FFN_DIM)),
                wspec((FFN_DIM, D_MODEL))]
    args = [x, oa, ob, wts["g_attn"], wts["g_ffn"], wts["w_merge"], wts["w_pa"], wts["w_pb"], wts["w_out"],
            wts["w_up"], wts["w_conv"], wts["b_conv"], wts["w_down"]]
    if slots:
        in_specs += [row(FFN_DIM), row(FFN_DIM)]
        args += list(cb)
        u_spec = row(FFN_DIM)
        u_shape = jax.ShapeDtypeStruct((m, FFN_DIM), F32)
    else:
        u_spec = pl.BlockSpec((1, SUBLANES, FFN_DIM), lambda b, i: (b, 0, 0))
        u_shape = jax.ShapeDtypeStruct((n_seq, SUBLANES, FFN_DIM), F32)
    return pl.pallas_call(
        functools.partial(_mix_ffn_kernel, tm=tm, f_chunk=FFN_DIM // 2, slots=slots),
        grid=grid,
        in_specs=in_specs,
        out_specs=[row(D_MODEL), u_spec],
        out_shape=[jax.ShapeDtypeStruct((m, D_MODEL), F32), u_shape],
        scratch_shapes=[pltpu.VMEM((tm + SUBLANES, FFN_DIM), F32)],
        compiler_params=pltpu.CompilerParams(dimension_semantics=sem, vmem_limit_bytes=VMEM_LIMIT),
        name="mix_ffn",
    )(*args)


def _sel_fold_matrix(n_cmp_valid, n_cmp_rows, n_sel_lanes):
    per = SEL_BLOCK // CMP_STRIDE
    m = np.zeros((n_cmp_rows, n_sel_lanes), np.float32)
    for n in range(n_cmp_valid):
        m[n, n // per] += 1.0
        if n + 1 < n_cmp_valid:
            m[n, (n + 1) // per] += 1.0
    return jnp.asarray(m, BF16)


def _pack_weights(w_in, w_cmp1, w_cmp2, g_q, g_k_cmp, g_k_slc, g_k_win, w_gate_up, b_gate, g_gla, w_pa, w_pb,
                  w_out, g_attn, g_ffn, w_up, w_conv, b_conv, w_down):
    depth = w_in.shape[0]
    splits = (A_WIDTH, KV_WIDTH, KV_WIDTH, KV_WIDTH, 3 * NSA_HEADS, GK_WIDTH, GK_WIDTH, B_WIDTH, B_WIDTH,
              GLA_GATE_RANK, 2 * D_MODEL)
    offs = np.concatenate([[0], np.cumsum(splits)])
    seg = lambda k: w_in[:, :, int(offs[k]):int(offs[k + 1])]
    padc = lambda a, w: jnp.pad(a, ((0, 0), (0, 0), (0, w - a.shape[-1])))
    w_a = jnp.concatenate([seg(0), seg(1), seg(2), seg(3), padc(seg(4), LANES), seg(5), seg(6), seg(7), seg(8),
                           padc(seg(9), LANES)], axis=-1).astype(BF16)
    same_g = jnp.eye(NSA_KV_GROUPS, dtype=F32)
    w_c1 = jnp.einsum("Lclde,gG->LlcgdGe", w_cmp1, same_g).reshape(depth, CMP_BLOCK, 2, LANES, LANES).astype(BF16)
    w_c2 = jnp.einsum("Lcef,gG->LcgeGf", w_cmp2, same_g).reshape(depth, 2, LANES, LANES).astype(BF16)
    w1h = w_cmp1.reshape(depth, 2, 2, CMP_STRIDE, HEAD_DIM, HEAD_DIM).astype(BF16)
    w_c1_halves = jnp.zeros((depth, CMP_STRIDE, KV_WIDTH, 2 * KV_WIDTH), BF16)
    w_c2_full = jnp.zeros((depth, KV_WIDTH, KV_WIDTH), BF16)
    for c in range(2):
        for g in range(NSA_KV_GROUPS):
            r0 = (c * NSA_KV_GROUPS + g) * HEAD_DIM
            w_c2_full = w_c2_full.at[:, r0:r0 + HEAD_DIM, r0:r0 + HEAD_DIM].set(w_cmp2[:, c].astype(BF16))
            for half in range(2):
                c0 = half * KV_WIDTH + r0
                w_c1_halves = w_c1_halves.at[:, :, r0:r0 + HEAD_DIM, c0:c0 + HEAD_DIM].set(w1h[:, c, half])
    grp = lambda n: jnp.asarray(np.kron(np.eye(n // HEAD_DIM), np.ones((HEAD_DIM, HEAD_DIM))), BF16)
    g_norm = jnp.concatenate([jnp.tile(g_q, (1, NSA_HEADS)), jnp.tile(g_k_slc, (1, NSA_KV_GROUPS)),
                              jnp.tile(g_k_win, (1, NSA_KV_GROUPS))], axis=-1)
    return {
        "w_a": w_a,
        "w_merge": w_in[:, :, int(offs[10]):].astype(BF16),
        "g_attn": g_attn[:, None, :], "g_ffn": g_ffn[:, None, :],
        "g_norm": g_norm[:, None, :], "ones128": grp(LANES),
        "g_k_cmp": jnp.tile(g_k_cmp, (1, NSA_KV_GROUPS))[:, None, :],
        "w_gate_up": jnp.pad(w_gate_up, ((0, 0), (0, LANES - GLA_GATE_RANK), (0, 0))).astype(BF16),
        "b_gate": b_gate[:, None, :], "g_gla": g_gla[:, None, :],
        "w_c1": w_c1, "w_c2": w_c2, "w_c1_halves": w_c1_halves, "w_c2_full": w_c2_full,
        "w_pa": w_pa.astype(BF16), "w_pb": w_pb.astype(BF16), "w_out": w_out.astype(BF16),
        "w_up": w_up.astype(BF16), "w_down": w_down.astype(BF16),
        "w_conv": jnp.pad(w_conv, ((0, 0), (0, SUBLANES - CONV_W), (0, 0))), "b_conv": b_conv[:, None, :],
    }


def _expand_matrix(n_blocks, n_keys):
    return jnp.asarray(np.arange(n_keys)[None, :] // SEL_BLOCK == np.arange(n_blocks)[:, None], BF16)


def _prompt_layer(x, wts, layer, b, t):
    tm = min(256, t)
    tq = min(128, t)
    q, kv_cmp, kv_slc, kv_win, ga, qg, kb, vb, rb, la = _proj_in(x, wts, layer, tm)
    r3 = lambda a: a.reshape(b, t, a.shape[-1])
    kvc = _compress(r3(kv_cmp), wts, layer)
    o_a = _nsa_prompt(r3(q), r3(ga), kvc, r3(kv_slc), r3(kv_win), wts, tq)
    o_b, s_new = _gla(r3(qg), r3(kb), r3(vb), r3(la), r3(rb),
                      jnp.zeros((b, GLA_HEADS, GLA_DK, GLA_DV), F32), wts, layer,
                      tt=min(512, t), chunk=math.gcd(t, GLA_CHUNK), n_valid=math.gcd(t, GLA_CHUNK))
    y, u_tail = _mix_ffn(x, o_a.reshape(b * t, A_WIDTH), o_b.reshape(b * t, B_WIDTH), wts, layer, tm, t // tm)
    kv6 = lambda a: a.reshape(b, t, 2, NSA_KV_GROUPS, HEAD_DIM)
    n_win = min(WINDOW, t)
    state = (kv6(kv_cmp), kv6(kv_slc), kv6(kv_win)[:, t - n_win:], s_new, u_tail[:, SUBLANES - (CONV_W - 1):])
    return y, state


def _sample_layer(x, wts, layer, bs, n_tok, past, cache_cmp, cache_slc, win_buf, win_rows, gla_state, conv_buf,
                  page_table):
    m = bs * SLOT
    q, kv_cmp, kv_slc, kv_win, ga, qg, kb, vb, rb, la = _proj_in(x, wts, layer, m)
    r3 = lambda a: a.reshape(bs, SLOT, a.shape[-1])
    o_a = _nsa_sample(r3(q), r3(ga), r3(kv_cmp), r3(kv_slc), r3(kv_win), win_buf, cache_cmp, cache_slc,
                      page_table, wts, layer, past, n_tok)
    o_b, s_new = _gla(r3(qg), r3(kb), r3(vb), r3(la), r3(rb), gla_state[layer], wts, layer,
                      tt=SLOT, chunk=SLOT, n_valid=n_tok)
    cb = conv_buf[layer]
    zeros = lambda n: jnp.zeros((bs, n, FFN_DIM), F32)
    cb1 = jnp.concatenate([cb[:, 1:2], zeros(SLOT - 1)], axis=1).reshape(m, FFN_DIM)
    cb2 = jnp.concatenate([cb, zeros(SLOT - 2)], axis=1).reshape(m, FFN_DIM)
    y, u = _mix_ffn(x, o_a.reshape(m, A_WIDTH), o_b.reshape(m, B_WIDTH), wts, layer, m, 1, cb=(cb1, cb2))
    kv6 = lambda a: r3(a)[:, :n_tok].reshape(bs, n_tok, 2, NSA_KV_GROUPS, HEAD_DIM)
    win_state = jnp.concatenate([win_rows[layer][:, n_tok:], kv6(kv_win)], axis=1)
    u_ext = jnp.concatenate([cb, u.reshape(bs, SLOT, FFN_DIM)[:, :n_tok]], axis=1)
    state = (kv6(kv_cmp), kv6(kv_slc), win_state, s_new, u_ext[:, -(CONV_W - 1):])
    return y, state


def kernel(x_prompt, x_sample, cache_kv_cmp, cache_kv_slc, state_kv_win, state_gla, state_conv, page_table,
           w_in, w_cmp1, w_cmp2, g_q, g_k_cmp, g_k_slc, g_k_win, w_gate_up, b_gate, g_gla, w_pa, w_pb, w_out,
           g_attn, g_ffn, w_up, w_conv, b_conv, w_down):
    depth = w_in.shape[0]
    bp, t, _ = x_prompt.shape
    bs, n_tok, _ = x_sample.shape
    page = cache_kv_cmp.shape[2]
    past = page_table.shape[1] * page
    assert n_tok <= SLOT and t % CMP_STRIDE == 0 and past % SEL_BLOCK == 0

    wts = _pack_weights(w_in, w_cmp1, w_cmp2, g_q, g_k_cmp, g_k_slc, g_k_win, w_gate_up, b_gate, g_gla, w_pa,
                        w_pb, w_out, g_attn, g_ffn, w_up, w_conv, b_conv, w_down)
    n_cmp_p = t // CMP_STRIDE
    fold = _sel_fold_matrix(n_cmp_p - 1, n_cmp_p, LANES - LANE_HOT).T
    wts["fold_prompt"] = jnp.pad(fold, ((LANE_HOT, 0), (0, 0)))
    gate_expand = np.zeros((3, LANES, A_WIDTH), np.float32)
    for br in range(3):
        for h in range(NSA_HEADS):
            gate_expand[br, br * NSA_HEADS + h, h * HEAD_DIM:(h + 1) * HEAD_DIM] = 1.0
    wts["gate_expand"] = jnp.asarray(gate_expand, BF16)
    n_cmp_s = (past + CMP_STRIDE) // CMP_STRIDE - 1
    sel_lanes = -(-(past // SEL_BLOCK + 1) // LANES) * LANES
    tk_s = min(1024, past)
    wts["m_sel_sample"] = _sel_fold_matrix(n_cmp_s, n_cmp_s, sel_lanes)
    blk_per_chunk = tk_s // SEL_BLOCK
    wts["e_sel_sample"] = jnp.stack([_expand_matrix(LANES, LANES * SEL_BLOCK)[:, j * tk_s:(j + 1) * tk_s]
                                     for j in range(LANES // blk_per_chunk)])

    cache_cmp = cache_kv_cmp.reshape(depth, -1, page // CMP_STRIDE, CMP_STRIDE, KV_WIDTH).transpose(0, 1, 3, 2, 4)
    cache_slc = cache_kv_slc.transpose(0, 1, 3, 4, 5, 2)
    win_buf = state_kv_win.transpose(0, 1, 3, 4, 5, 2)

    y_p = x_prompt.reshape(bp * t, D_MODEL)
    y_s = jnp.pad(x_sample, ((0, 0), (0, SLOT - n_tok), (0, 0))).reshape(bs * SLOT, D_MODEL)
    st_p, st_s = [], []
    for l in range(depth):
        y_p, sp = _prompt_layer(y_p, wts, l, bp, t)
        y_s, ss = _sample_layer(y_s, wts, l, bs, n_tok, past, cache_cmp, cache_slc, win_buf, state_kv_win,
                                state_gla, state_conv, page_table)
        st_p.append(sp)
        st_s.append(ss)
    stack = lambda sts, k: jnp.stack([s[k] for s in sts])
    return ((y_p.reshape(bp, t, D_MODEL), y_s.reshape(bs, SLOT, D_MODEL)[:, :n_tok])
            + tuple(stack(st_p, k) for k in range(5)) + tuple(stack(st_s, k) for k in range(5)))
```

```python
import functools
import math

import numpy as np
import jax
import jax.numpy as jnp
from jax import lax
from jax.experimental import pallas as pl
from jax.experimental.pallas import tpu as pltpu

F32 = jnp.float32
BF16 = jnp.bfloat16

D_MODEL = 1024
HEAD_DIM = 64
NSA_HEADS = 8
NSA_KV_GROUPS = 2
NSA_GROUP_SIZE = NSA_HEADS // NSA_KV_GROUPS
CMP_STRIDE = 16
CMP_BLOCK = 2 * CMP_STRIDE
SEL_BLOCK = 64
SEL_TOPK = 16
WINDOW = 512
GLA_HEADS = 4
GLA_DK = 64
GLA_DV = 128
GLA_GATE_RANK = 16
GLA_GATE_NORM = 16.0
GLA_CHUNK = 64
FFN_DIM = 2816
CONV_W = 3
EPS = 1e-6
NEG_INF = -1e30
FORCE_SCORE = 1e9
BELOW_ALL_SCORES = -3e38

A_WIDTH = NSA_HEADS * HEAD_DIM
B_WIDTH = GLA_HEADS * GLA_DV
KV_WIDTH = 2 * NSA_KV_GROUPS * HEAD_DIM
GK_WIDTH = GLA_HEADS * GLA_DK
LANES = 128
SUBLANES = 8
SLOT = SUBLANES
ALIBI = tuple(float(2.0 ** (-8.0 * (h + 1) / NSA_HEADS)) for h in range(NSA_HEADS))

C_Q, C_CMP, C_SLC, C_WIN, C_GATE = 0, 512, 768, 1024, 1280
C_QB, C_KB, C_VB, C_RB, C_LR, C_END = 1408, 1664, 1920, 2432, 2944, 3072
NORM_W = A_WIDTH + 2 * LANES

VMEM_LIMIT = 56 * 1024 * 1024


def _dot(a, b):
    return jnp.dot(a.astype(BF16), b.astype(BF16), preferred_element_type=F32)


def _dot_nt(a, b):
    return lax.dot_general(a.astype(BF16), b.astype(BF16), (((1,), (1,)), ((), ())),
                           preferred_element_type=F32)


def _dot_tn(a, b):
    return lax.dot_general(a.astype(BF16), b.astype(BF16), (((0,), (0,)), ((), ())),
                           preferred_element_type=F32)


def _split_bf16(x, parts):
    out = []
    for _ in range(parts):
        p = x.astype(BF16)
        out.append(p)
        x = x - p.astype(F32)
    return out


def _dot_wide(x, m, parts, dot=None):
    dot = dot or (lambda a, b: jnp.dot(a, b, preferred_element_type=F32))
    acc = None
    for p in _split_bf16(x, parts):
        t = dot(p, m)
        acc = t if acc is None else acc + t
    return acc


def _rms_rows(x, g):
    return x * lax.rsqrt(jnp.mean(x * x, axis=-1, keepdims=True) + EPS) * g


def _rms_groups(x, ones_blockdiag, g):
    sq = x * x
    ms = jnp.concatenate([_dot_wide(sq[:, j:j + LANES], ones_blockdiag, 2)
                          for j in range(0, x.shape[1], LANES)], axis=1) * (1.0 / HEAD_DIM)
    return x * lax.rsqrt(ms + EPS) * g


def _sigmoid(x):
    return 1.0 / (1.0 + jnp.exp(-x))


def _gelu_tanh(x):
    return 0.5 * x * (1.0 + jnp.tanh(math.sqrt(2.0 / math.pi) * (x + 0.044715 * (x * x * x))))


def _log_sigmoid(x):
    return jnp.minimum(x, 0.0) - jnp.log1p(jnp.exp(-jnp.abs(x)))


def _topk_mask(score, n_cand, k):
    lane = lax.broadcasted_iota(jnp.int32, score.shape, 1)
    ahead = jnp.zeros(score.shape, F32)
    for j in range(n_cand):
        col = score[:, j:j + 1]
        ahead = ahead + jnp.where(col > score, 1.0, jnp.where(col == score, jnp.where(lane > j, 1.0, 0.0), 0.0))
    return jnp.where(lane < n_cand, jnp.where(ahead < k, 1.0, 0.0), 0.0)


def _masked_softmax(s, mask):
    s = jnp.where(mask, s, NEG_INF)
    m = jnp.max(s, axis=-1, keepdims=True)
    e = jnp.where(mask, jnp.exp(s - m), 0.0)
    l = jnp.sum(e, axis=-1, keepdims=True)
    return e * jnp.where(l > 0.0, 1.0 / l, 0.0)


def _stack_heads(q, g):
    base = g * NSA_GROUP_SIZE
    return jnp.concatenate([q[:, (base + r) * HEAD_DIM:(base + r + 1) * HEAD_DIM]
                            for r in range(NSA_GROUP_SIZE)], axis=0)


def _head_slopes(g, n):
    row = lax.broadcasted_iota(jnp.int32, (NSA_GROUP_SIZE * n, 1), 0)
    slope = jnp.full((NSA_GROUP_SIZE * n, 1), ALIBI[g * NSA_GROUP_SIZE + NSA_GROUP_SIZE - 1], F32)
    for r in reversed(range(NSA_GROUP_SIZE - 1)):
        slope = jnp.where(row < (r + 1) * n, ALIBI[g * NSA_GROUP_SIZE + r], slope)
    return slope


def _gate_cols(ga, branch, g):
    c0 = branch * NSA_HEADS + g * NSA_GROUP_SIZE
    return jnp.concatenate([ga[:, c0 + r:c0 + r + 1] for r in range(NSA_GROUP_SIZE)], axis=0)


def _block_scores(blk_imp, q_pos, lane_i, n_sel):
    forced = (lane_i == 0) | (lane_i == lax.shift_right_logical(q_pos, int(math.log2(SEL_BLOCK))))
    valid = lane_i * SEL_BLOCK <= q_pos
    score = jnp.where(forced, FORCE_SCORE, jnp.where(valid, blk_imp, NEG_INF))
    return jnp.where(lane_i < n_sel, score, BELOW_ALL_SCORES)


def _compress_rows(load_rows, w1_ref, w2_ref, ones128_ref, gk_ref):
    out = []
    for c in range(2):
        acc = None
        for l in range(CMP_BLOCK):
            t = jnp.dot(load_rows(c, l).astype(BF16), w1_ref[l, c], preferred_element_type=F32)
            acc = t if acc is None else acc + t
        out.append(jnp.dot(_gelu_tanh(acc).astype(BF16), w2_ref[c], preferred_element_type=F32))
    return _rms_groups(out[0], ones128_ref[...], gk_ref[...]), out[1]


def _proj_in_kernel(x_ref, gattn_ref, w_ref, gnorm_ref, ones_ref, wgu_ref, bg_ref,
                    q_ref, cmp_ref, slc_ref, win_ref, ga_ref, qg_ref, kb_ref, vb_ref, rb_ref, la_ref):
    h = _rms_rows(x_ref[...], gattn_ref[...])
    p = jnp.dot(h.astype(BF16), w_ref[...], preferred_element_type=F32)
    to_norm = jnp.concatenate([p[:, C_Q:C_CMP], p[:, C_SLC:C_SLC + LANES], p[:, C_WIN:C_WIN + LANES]], axis=1)
    y = _rms_groups(to_norm, ones_ref[...], gnorm_ref[...])
    q_ref[...] = y[:, :A_WIDTH] * HEAD_DIM ** -0.5
    cmp_ref[...] = p[:, C_CMP:C_SLC]
    slc_ref[...] = jnp.concatenate([y[:, A_WIDTH:A_WIDTH + LANES], p[:, C_SLC + LANES:C_WIN]], axis=1)
    win_ref[...] = jnp.concatenate([y[:, A_WIDTH + LANES:], p[:, C_WIN + LANES:C_GATE]], axis=1)
    ga_ref[...] = _sigmoid(p[:, C_GATE:C_QB])
    qg_ref[...] = p[:, C_QB:C_KB] * GLA_DK ** -0.5
    kb_ref[...] = p[:, C_KB:C_VB]
    vb_ref[...] = p[:, C_VB:C_RB]
    rb_ref[...] = p[:, C_RB:C_LR]
    z = jnp.dot(p[:, C_LR:C_END].astype(BF16), wgu_ref[...], preferred_element_type=F32) + bg_ref[...]
    la_ref[...] = _log_sigmoid(z) * (1.0 / GLA_GATE_NORM)


def _const_spec(shape, layer=None):
    nd = len(shape)
    if layer is None:
        return pl.BlockSpec(shape, lambda *_: (0,) * nd)
    return pl.BlockSpec((None,) + shape, lambda *_: (layer,) + (0,) * nd)


def _proj_in(x, wts, layer, tm):
    m = x.shape[0]
    widths = (A_WIDTH, KV_WIDTH, KV_WIDTH, KV_WIDTH, LANES, GK_WIDTH, GK_WIDTH, B_WIDTH, B_WIDTH, GK_WIDTH)
    row = lambda w: pl.BlockSpec((tm, w), lambda i: (i, 0))
    return pl.pallas_call(
        _proj_in_kernel,
        grid=(m // tm,),
        in_specs=[row(D_MODEL),
                  _const_spec((1, D_MODEL), layer), _const_spec((D_MODEL, C_END), layer),
                  _const_spec((1, NORM_W), layer), _const_spec((LANES, LANES)),
                  _const_spec((LANES, GK_WIDTH), layer), _const_spec((1, GK_WIDTH), layer)],
        out_specs=[row(w) for w in widths],
        out_shape=[jax.ShapeDtypeStruct((m, w), F32) for w in widths],
        compiler_params=pltpu.CompilerParams(dimension_semantics=("arbitrary",), vmem_limit_bytes=VMEM_LIMIT),
        name="proj_in",
    )(x, wts["g_attn"], wts["w_a"], wts["g_norm"], wts["ones128"], wts["w_gate_up"], wts["b_gate"])


def _compress_kernel(kv_ref, w1_ref, w2_ref, ones_ref, gk_ref, out_ref, pad_ref, *, t):
    for c in range(2):
        pad_ref[c, 0:t, :] = kv_ref[0, :, c * LANES:(c + 1) * LANES]
        pad_ref[c, t:t + CMP_STRIDE, :] = jnp.zeros((CMP_STRIDE, LANES), F32)
    n_blk = t // CMP_STRIDE
    kc, vc = _compress_rows(lambda c, l: pad_ref[c, pl.ds(l, n_blk, stride=CMP_STRIDE), :],
                            w1_ref, w2_ref, ones_ref, gk_ref)
    out_ref[0] = jnp.concatenate([kc, vc], axis=1)


def _compress(kv_cmp, wts, layer):
    b, t, _ = kv_cmp.shape
    n_blk = t // CMP_STRIDE
    return pl.pallas_call(
        functools.partial(_compress_kernel, t=t),
        grid=(b,),
        in_specs=[pl.BlockSpec((1, t, KV_WIDTH), lambda i: (i, 0, 0)),
                  _const_spec((CMP_BLOCK, 2, LANES, LANES), layer), _const_spec((2, LANES, LANES), layer),
                  _const_spec((LANES, LANES)), _const_spec((1, LANES), layer)],
        out_specs=pl.BlockSpec((1, n_blk, KV_WIDTH), lambda i: (i, 0, 0)),
        out_shape=jax.ShapeDtypeStruct((b, n_blk, KV_WIDTH), F32),
        scratch_shapes=[pltpu.VMEM((2, t + CMP_STRIDE, LANES), F32)],
        compiler_params=pltpu.CompilerParams(dimension_semantics=("arbitrary",), vmem_limit_bytes=VMEM_LIMIT),
        name="compress",
    )(kv_cmp, wts["w_c1"], wts["w_c2"], wts["ones128"], wts["g_k_cmp"])


MASK_BIG = 2.0 ** 100
LANE_HI, LANE_LO, LANE_ONE, LANE_HOT = HEAD_DIM, HEAD_DIM + 1, HEAD_DIM + 2, HEAD_DIM + 3
SEL_SHIFT = int(math.log2(SEL_BLOCK))


def _group_halves(x, g):
    return x if g == 0 else pltpu.roll(x, HEAD_DIM, axis=1)


def _build_prompt_operands(kvc_ref, slc_ref, win_ref, kc_ref, vc_ref, ks_ref, vs_ref, kw_ref, vw_ref):
    t = slc_ref.shape[1]
    lane = lax.broadcasted_iota(jnp.int32, (t, LANES), 1)
    pos = lax.broadcasted_iota(jnp.int32, (t, LANES), 0)
    blk = lax.shift_right_logical(pos, SEL_SHIFT)
    pos_lanes = jnp.where(lane == LANE_HI, blk, jnp.where(lane == LANE_LO, pos & (SEL_BLOCK - 1),
                                                          jnp.where(lane == LANE_ONE, 1, 0)))
    aux_win = pos_lanes.astype(F32)
    aux_sel = jnp.where(lane - LANE_HOT == blk, 1, pos_lanes).astype(F32)
    n_cmp = kvc_ref.shape[1]
    lane_c = lax.broadcasted_iota(jnp.int32, (n_cmp, LANES), 1)
    blk_c = lax.broadcasted_iota(jnp.int32, (n_cmp, LANES), 0)
    aux_cmp = jnp.where(lane_c == LANE_HI, blk_c, jnp.where((lane_c == LANE_LO) | (lane_c == LANE_ONE), 1, 0)).astype(F32)
    lane_p = lax.broadcasted_iota(jnp.int32, (WINDOW, LANES), 1)
    before_seq = jnp.where(lane_p == LANE_HOT, -MASK_BIG, 0.0).astype(BF16)
    for g in range(NSA_KV_GROUPS):
        ks_ref[g] = jnp.where(lane < HEAD_DIM, _group_halves(slc_ref[0, :, 0:LANES], g), aux_sel).astype(BF16)
        vs_ref[g] = jnp.where(lane < HEAD_DIM, _group_halves(slc_ref[0, :, LANES:], g), 1.0).astype(BF16)
        kw_ref[g, 0:WINDOW, :] = before_seq
        vw_ref[g, 0:WINDOW, :] = jnp.zeros((WINDOW, LANES), BF16)
        kw_ref[g, WINDOW:, :] = jnp.where(lane < HEAD_DIM, _group_halves(win_ref[0, :, 0:LANES], g), aux_win).astype(BF16)
        vw_ref[g, WINDOW:, :] = jnp.where(lane < HEAD_DIM, _group_halves(win_ref[0, :, LANES:], g), 1.0).astype(BF16)
        kc_ref[g] = jnp.where(lane_c < HEAD_DIM, _group_halves(kvc_ref[0, :, 0:LANES], g), aux_cmp).astype(BF16)
        vc_ref[g] = jnp.where(lane_c < HEAD_DIM, _group_halves(kvc_ref[0, :, LANES:], g), 0.0).astype(BF16)


def _nsa_prompt_kernel(q_ref, ga_ref, kvc_ref, slc_ref, win_ref, fold_ref, gexp_ref, o_ref,
                       kc_ref, vc_ref, ks_ref, vs_ref, kw_ref, vw_ref, m_scr, acc_scr, *, tq, kw_sel):
    i = pl.program_id(1)

    @pl.when(i == 0)
    def _():
        _build_prompt_operands(kvc_ref, slc_ref, win_ref, kc_ref, vc_ref, ks_ref, vs_ref, kw_ref, vw_ref)

    q0 = i * tq
    q0f = q0.astype(F32)
    q = q_ref[0]
    rows = NSA_GROUP_SIZE * tq
    n_cmp = kvc_ref.shape[1]
    n_sel = slc_ref.shape[1] // SEL_BLOCK
    k_top = min(SEL_TOPK, n_sel)
    lane = lax.broadcasted_iota(jnp.int32, (tq, LANES), 1)
    lane4 = lax.broadcasted_iota(jnp.int32, (rows, LANES), 1)
    t_w = lax.broadcasted_iota(jnp.int32, (rows, tq), 0) & (tq - 1)
    lane_w = lax.broadcasted_iota(jnp.int32, (rows, tq), 1)
    causal_bias = jnp.where(lane_w <= t_w, 0.0, -MASK_BIG)
    far_bias = jnp.where(lane_w > t_w, 0.0, -MASK_BIG)
    cmp_valid = (lax.broadcasted_iota(jnp.int32, (rows, n_cmp), 1) * CMP_STRIDE + (CMP_BLOCK - 1)
                 <= q0 + (lax.broadcasted_iota(jnp.int32, (rows, n_cmp), 0) & (tq - 1)))
    hot_rows = slice(HEAD_DIM, HEAD_DIM + 8 * (-(-(LANE_HOT - HEAD_DIM + n_sel) // 8)))
    n_hot = hot_rows.stop - hot_rows.start
    blk_t = lax.broadcasted_iota(jnp.int32, (n_hot, tq), 0) - (LANE_HOT - HEAD_DIM)
    qpos_t = q0 + lax.broadcasted_iota(jnp.int32, (n_hot, tq), 1)
    is_blk = (blk_t >= 0) & (blk_t < n_sel)

    def lane_tiles_max(s):
        m = s[:, 0:LANES]
        for j in range(1, s.shape[1] // LANES):
            m = jnp.maximum(m, s[:, j * LANES:(j + 1) * LANES])
        return m

    def normalised(acc):
        return acc * (1.0 / pltpu.roll(acc, HEAD_DIM, axis=1))

    def attend_selected(qa, g):
        n_full = q0 // kw_sel
        k_last = pl.multiple_of(n_full * kw_sel, kw_sel)
        visible = lax.broadcasted_iota(jnp.int32, (rows, kw_sel), 1) <= (
            (q0 - k_last) + (lax.broadcasted_iota(jnp.int32, (rows, kw_sel), 0) & (tq - 1)))
        s_last = _dot_nt(qa, ks_ref[g, pl.ds(k_last, kw_sel), :]) + jnp.where(visible, 0.0, -MASK_BIG)
        m_scr[...] = lane_tiles_max(s_last)

        def group(ref, c):
            return ref[g, pl.ds(pl.multiple_of(c * kw_sel, kw_sel), kw_sel), :]

        def max_body(c, _):
            m_scr[...] = jnp.maximum(m_scr[...], lane_tiles_max(_dot_nt(qa, group(ks_ref, c))))
            return 0

        lax.fori_loop(0, n_full, max_body, 0)
        m = jnp.max(m_scr[...], axis=-1, keepdims=True)
        m_scr[...] = jnp.broadcast_to(m, (rows, LANES))
        acc_scr[...] = _dot(jnp.exp(s_last - m), vs_ref[g, pl.ds(k_last, kw_sel), :])

        def acc_body(c, _):
            m_w = jnp.concatenate([m_scr[...]] * (kw_sel // LANES), axis=1)
            acc_scr[...] += _dot(jnp.exp(_dot_nt(qa, group(ks_ref, c)) - m_w), group(vs_ref, c))
            return 0

        lax.fori_loop(0, n_full, acc_body, 0)
        return normalised(acc_scr[...])

    def attend_window(qa, g):
        k0 = pl.multiple_of(q0, tq)
        s = _dot_nt(qa, kw_ref[g, pl.ds(k0, WINDOW + tq), :])
        s = jnp.concatenate([s[:, 0:tq] + far_bias, s[:, tq:WINDOW], s[:, WINDOW:] + causal_bias], axis=1)
        m = jnp.max(lane_tiles_max(s), axis=-1, keepdims=True)
        return normalised(_dot(jnp.exp(s - m), vw_ref[g, pl.ds(k0, WINDOW + tq), :]))

    branch_out = [[], [], []]
    for g in range(NSA_KV_GROUPS):
        q_rows, base_rows, cbase_rows = [], [], []
        for r in range(NSA_GROUP_SIZE):
            h = g * NSA_GROUP_SIZE + r
            tile = q[:, (h // 2) * LANES:(h // 2 + 1) * LANES]
            q_rows.append(tile if h % 2 == 0 else pltpu.roll(tile, HEAD_DIM, axis=1))
            sl = ALIBI[h]
            base_rows.append(jnp.where(lane == LANE_HI, SEL_BLOCK * sl, jnp.where(
                lane == LANE_LO, sl, jnp.where(lane == LANE_ONE, -sl * q0f, 0.0))))
            cbase_rows.append(jnp.where(lane == LANE_HI, CMP_STRIDE * sl, jnp.where(
                lane == LANE_LO, (CMP_BLOCK - 1) * sl, jnp.where(lane == LANE_ONE, -sl * q0f, 0.0))))
        q4 = jnp.concatenate(q_rows, axis=0)
        base4 = jnp.concatenate(base_rows, axis=0)
        head_lanes = lane4 < HEAD_DIM

        qa = jnp.where(head_lanes, q4, jnp.concatenate(cbase_rows, axis=0)).astype(BF16)
        s = jnp.where(cmp_valid, _dot_nt(qa, kc_ref[g]), -MASK_BIG)
        m = jnp.max(s, axis=-1, keepdims=True)
        e = jnp.where(cmp_valid, jnp.exp(s - m), 0.0)
        l = jnp.sum(e, axis=-1, keepdims=True)
        p = e * jnp.where(l > 0.0, 1.0 / l, 0.0)
        branch_out[0].append(_dot(p, vc_ref[g]))
        imp = p[0:tq]
        for r in range(1, NSA_GROUP_SIZE):
            imp = imp + p[r * tq:(r + 1) * tq]

        blk_imp = _dot_wide(imp, fold_ref[...], 3, dot=lambda a, mt: _dot_nt(mt, a))[hot_rows]
        forced = (blk_t == 0) | (blk_t == lax.shift_right_logical(qpos_t, SEL_SHIFT))
        score = jnp.where(forced, FORCE_SCORE, jnp.where(blk_t * SEL_BLOCK <= qpos_t, blk_imp, NEG_INF))
        score = jnp.where(is_blk, score, BELOW_ALL_SCORES)
        ahead = jnp.zeros((n_hot, tq), F32)
        for j in range(n_sel):
            row_j = score[LANE_HOT - HEAD_DIM + j:LANE_HOT - HEAD_DIM + j + 1, :]
            ahead = ahead + jnp.where(row_j > score, 1.0, jnp.where(row_j == score, jnp.where(blk_t > j, 1.0, 0.0), 0.0))
        sel_bias_t = jnp.where(is_blk & (ahead >= k_top), -MASK_BIG, 0.0)
        sel_bias = jnp.transpose(jnp.concatenate(
            [jnp.zeros((hot_rows.start, tq), F32), sel_bias_t, jnp.zeros((LANES - hot_rows.stop, tq), F32)], axis=0))
        sel_bias4 = jnp.concatenate([sel_bias] * NSA_GROUP_SIZE, axis=0)

        qa = jnp.where(head_lanes, q4, base4 + sel_bias4).astype(BF16)
        branch_out[1].append(attend_selected(qa, g))
        qa = jnp.where(head_lanes, q4, jnp.where(lane4 == LANE_HOT, 1.0, base4)).astype(BF16)
        branch_out[2].append(attend_window(qa, g))

    ga = ga_ref[0]
    out = None
    for br in range(3):
        tiles = []
        for g in range(NSA_KV_GROUPS):
            o4 = branch_out[br][g]
            for r in range(0, NSA_GROUP_SIZE, 2):
                even, odd = o4[r * tq:(r + 1) * tq], o4[(r + 1) * tq:(r + 2) * tq]
                tiles.append(jnp.where(lane < HEAD_DIM, even, pltpu.roll(odd, HEAD_DIM, axis=1)))
        term = _dot_wide(ga, gexp_ref[br], 3) * jnp.concatenate(tiles, axis=1)
        out = term if out is None else out + term
    o_ref[0] = out


def _nsa_prompt(q, ga, kvc, kv_slc, kv_win, wts, tq):
    b, t, _ = q.shape
    n_cmp = kvc.shape[1]
    tile = lambda w: pl.BlockSpec((1, tq, w), lambda bi, i: (bi, i, 0))
    full = lambda n, w: pl.BlockSpec((1, n, w), lambda bi, i: (bi, 0, 0))
    kw_sel = min(max(512, tq), t)
    assert tq % LANES == 0 and WINDOW % tq == 0 and tq < WINDOW and kw_sel % tq == 0 and t % kw_sel == 0
    assert LANE_HOT + t // SEL_BLOCK <= LANES
    aug = lambda n: pltpu.VMEM((NSA_KV_GROUPS, n, LANES), BF16)
    stat = pltpu.VMEM((NSA_GROUP_SIZE * tq, LANES), F32)
    return pl.pallas_call(
        functools.partial(_nsa_prompt_kernel, tq=tq, kw_sel=kw_sel),
        grid=(b, t // tq),
        in_specs=[tile(A_WIDTH), tile(LANES), full(n_cmp, KV_WIDTH), full(t, KV_WIDTH), full(t, KV_WIDTH),
                  _const_spec((LANES, n_cmp)), _const_spec((3, LANES, A_WIDTH))],
        out_specs=tile(A_WIDTH),
        out_shape=jax.ShapeDtypeStruct((b, t, A_WIDTH), F32),
        scratch_shapes=[aug(n_cmp), aug(n_cmp), aug(t), aug(t), aug(t + WINDOW), aug(t + WINDOW), stat, stat],
        compiler_params=pltpu.CompilerParams(dimension_semantics=("arbitrary", "arbitrary"),
                                             vmem_limit_bytes=VMEM_LIMIT),
        name="nsa_prompt",
    )(q, ga, kvc, kv_slc, kv_win, wts["fold_prompt"], wts["gate_expand"])


def _nsa_sample_kernel(pt_ref, q_ref, ga_ref, ncmp_ref, nslc_ref, nwin_ref, wbuf_ref, cmp_hbm, slc_hbm,
                       w1_ref, w2_ref, ones_ref, gk_ref, msel_ref, esel_ref, regroup_ref,
                       o_ref, xcn_ref, xc_ref, xs_ref, half_scr, sem_ref, *, layer, past, n_tok, tk):
    b = pl.program_id(0)
    nb = pl.num_programs(0)
    n_pages = pt_ref.shape[1]
    page = past // n_pages
    hpp = page // CMP_STRIDE
    n_cmp = past // CMP_STRIDE
    slot = b % 2

    def page_copies(bb, sl):
        out = []
        for p in range(n_pages):
            pg = pt_ref[bb, p]
            out.append(pltpu.make_async_copy(cmp_hbm.at[layer, pg], xcn_ref.at[sl, :, pl.ds(p * page, page)],
                                             sem_ref.at[0, sl]))
            out.append(pltpu.make_async_copy(slc_hbm.at[layer, pg], xs_ref.at[sl, :, :, :, pl.ds(p * page, page)],
                                             sem_ref.at[1, sl]))
        return out

    def rows_on_lanes(new_ref):
        new = jnp.where(is_tok, new_ref[0], 0.0)
        return jnp.transpose(jnp.concatenate([new, jnp.zeros((LANES - SLOT, KV_WIDTH), F32)], axis=0))

    @pl.when(b == 0)
    def _():
        for cp in page_copies(0, 0):
            cp.start()

    @pl.when(b + 1 < nb)
    def _():
        for cp in page_copies(b + 1, 1 - slot):
            cp.start()

    is_tok = lax.broadcasted_iota(jnp.int32, (SLOT, 1), 0) < n_tok
    xc_ref[:, n_cmp:n_cmp + SUBLANES, :] = jnp.zeros((CMP_STRIDE, SUBLANES, KV_WIDTH), F32)
    for l in range(n_tok):
        xc_ref[l, n_cmp:n_cmp + 1, :] = ncmp_ref[0, l:l + 1, :]
    new_slc = rows_on_lanes(nslc_ref)
    new_win = rows_on_lanes(nwin_ref)
    for c in range(2):
        for g in range(NSA_KV_GROUPS):
            r0 = (c * NSA_KV_GROUPS + g) * HEAD_DIM
            xs_ref[slot, c, g, :, past:past + LANES] = new_slc[r0:r0 + HEAD_DIM, :]
    n_buf = wbuf_ref.shape[-1]

    for cp in page_copies(b, slot):
        cp.wait()

    q = q_ref[0]
    ga = ga_ref[0]
    rows = NSA_GROUP_SIZE * SLOT
    n_sel = past // SEL_BLOCK + 1
    q_pos = past + lax.broadcasted_iota(jnp.int32, (SLOT, 1), 0)
    q_pos4 = jnp.concatenate([q_pos] * NSA_GROUP_SIZE, axis=0)
    sel_lanes = msel_ref.shape[1]
    lane_i = lax.broadcasted_iota(jnp.int32, (1, sel_lanes), 1)
    ends = lax.broadcasted_iota(jnp.int32, (1, n_cmp), 1) * CMP_STRIDE + (CMP_BLOCK - 1)

    for p in range(n_pages):
        regrouped = _dot_nt(regroup_ref[...], xcn_ref[slot, :, p * page:(p + 1) * page])
        for l in range(CMP_STRIDE):
            xc_ref[l, p * hpp:(p + 1) * hpp, :] = regrouped[l * hpp:(l + 1) * hpp, :]

    ab = None
    for l in range(CMP_STRIDE):
        t = jnp.dot(xc_ref[l].astype(BF16), w1_ref[l], preferred_element_type=F32)
        ab = t if ab is None else ab + t
    half_scr[...] = ab[:, KV_WIDTH:]
    hid = ab[0:n_cmp, :KV_WIDTH] + half_scr[1:n_cmp + 1, :]
    kvc = jnp.dot(_gelu_tanh(hid).astype(BF16), w2_ref[...], preferred_element_type=F32)
    kc = _rms_groups(kvc[:, :LANES], ones_ref[...], gk_ref[...])
    vc = kvc[:, LANES:]
    pieces = []
    for g in range(NSA_KV_GROUPS):
        slope = _head_slopes(g, SLOT)
        qg = _stack_heads(q, g).astype(BF16)
        kcol = slice(g * HEAD_DIM, (g + 1) * HEAD_DIM)
        vcol = slice(LANES + g * HEAD_DIM, LANES + (g + 1) * HEAD_DIM)

        dist = q_pos4 - ends
        s = _dot_nt(qg, kc[:, kcol]) - slope * dist.astype(F32)
        p = _masked_softmax(s, dist >= 0)
        o_cmp = _dot(p, vc[:, kcol])
        imp = p[0:SLOT]
        for r in range(1, NSA_GROUP_SIZE):
            imp = imp + p[r * SLOT:(r + 1) * SLOT]

        blk_imp = _dot_wide(imp, msel_ref[...], 3)
        sel = _topk_mask(_block_scores(blk_imp, q_pos, lane_i, n_sel), n_sel, min(SEL_TOPK, n_sel))
        sel4 = jnp.concatenate([sel] * NSA_GROUP_SIZE, axis=0).astype(BF16)

        chunks_per_slab = esel_ref.shape[0]
        bounds = [(c * tk, tk if c * tk < past else LANES) for c in range(past // tk + 1)]
        scores = []
        for c, (k0, n_k) in enumerate(bounds):
            dist = q_pos4 - (k0 + lax.broadcasted_iota(jnp.int32, (1, n_k), 1))
            s = _dot(qg, xs_ref[slot, 0, g, :, k0:k0 + n_k]) - slope * dist.astype(F32)
            if k0 < past:
                slab = c // chunks_per_slab
                in_sel = _dot(sel4[:, slab * LANES:(slab + 1) * LANES], esel_ref[c % chunks_per_slab])
            else:
                in_sel = sel4[:, n_sel - 1:n_sel].astype(F32)
            scores.append(jnp.where((in_sel > 0.5) & (dist >= 0), s, NEG_INF))
        m = functools.reduce(jnp.maximum, [jnp.max(s, axis=-1, keepdims=True) for s in scores])
        l = jnp.zeros((rows, 1), F32)
        o_slc = jnp.zeros((rows, HEAD_DIM), F32)
        for s, (k0, n_k) in zip(scores, bounds):
            p = jnp.exp(s - m)
            l = l + jnp.sum(p, axis=-1, keepdims=True)
            o_slc = o_slc + _dot_nt(p, xs_ref[slot, 1, g, :, k0:k0 + n_k])
        o_slc = o_slc * (1.0 / l)

        k_pos = (past - n_buf) + lax.broadcasted_iota(jnp.int32, (1, n_buf + LANES), 1)
        dist = q_pos4 - k_pos
        s = jnp.concatenate([_dot(qg, wbuf_ref[0, 0, g]), _dot(qg, new_win[kcol, :])], axis=1) - slope * dist.astype(F32)
        p = _masked_softmax(s, (dist >= 0) & (dist < WINDOW))
        o_win = _dot_nt(p[:, :n_buf], wbuf_ref[0, 1, g]) + _dot_nt(p[:, n_buf:], new_win[vcol, :])

        og = _gate_cols(ga, 0, g) * o_cmp + _gate_cols(ga, 1, g) * o_slc + _gate_cols(ga, 2, g) * o_win
        pieces += [og[r * SLOT:(r + 1) * SLOT] for r in range(NSA_GROUP_SIZE)]
    o_ref[0] = jnp.concatenate(pieces, axis=1)


def _nsa_sample(q, ga, ncmp, nslc, nwin, win_buf, cache_cmp, cache_slc, page_table, wts, layer, past, n_tok):
    bs = q.shape[0]
    n_buf = win_buf.shape[-1]
    n_cmp = past // CMP_STRIDE
    tk = wts["e_sel_sample"].shape[2]
    slot_spec = lambda w: pl.BlockSpec((1, SLOT, w), lambda b, pt: (b, 0, 0))
    const = lambda shape, lyr=None: (
        pl.BlockSpec(shape, lambda b, pt: (0,) * len(shape), pipeline_mode=pl.Buffered(1)) if lyr is None else
        pl.BlockSpec((None,) + shape, lambda b, pt: (lyr,) + (0,) * len(shape), pipeline_mode=pl.Buffered(1)))
    page = cache_cmp.shape[-1]
    grid_spec = pltpu.PrefetchScalarGridSpec(
        num_scalar_prefetch=1,
        grid=(bs,),
        in_specs=[slot_spec(A_WIDTH), slot_spec(LANES), slot_spec(KV_WIDTH), slot_spec(KV_WIDTH), slot_spec(KV_WIDTH),
                  pl.BlockSpec((None, 1, 2, NSA_KV_GROUPS, HEAD_DIM, n_buf), lambda b, pt: (layer, b, 0, 0, 0, 0)),
                  pl.BlockSpec(memory_space=pl.ANY), pl.BlockSpec(memory_space=pl.ANY),
                  const((CMP_STRIDE, KV_WIDTH, 2 * KV_WIDTH), layer), const((KV_WIDTH, KV_WIDTH), layer),
                  const((LANES, LANES)), const((1, LANES), layer),
                  const(wts["m_sel_sample"].shape), const(wts["e_sel_sample"].shape), const((page, page))],
        out_specs=slot_spec(A_WIDTH),
        scratch_shapes=[pltpu.VMEM((2, KV_WIDTH, past), F32),
                        pltpu.VMEM((CMP_STRIDE, n_cmp + SUBLANES, KV_WIDTH), F32),
                        pltpu.VMEM((2, 2, NSA_KV_GROUPS, HEAD_DIM, past + LANES), F32),
                        pltpu.VMEM((n_cmp + SUBLANES, KV_WIDTH), F32),
                        pltpu.SemaphoreType.DMA((2, 2))],
    )
    return pl.pallas_call(
        functools.partial(_nsa_sample_kernel, layer=layer, past=past, n_tok=n_tok, tk=tk),
        grid_spec=grid_spec,
        out_shape=jax.ShapeDtypeStruct((bs, SLOT, A_WIDTH), F32),
        compiler_params=pltpu.CompilerParams(dimension_semantics=("arbitrary",), vmem_limit_bytes=VMEM_LIMIT),
        name="nsa_sample",
    )(page_table, q, ga, ncmp, nslc, nwin, win_buf, cache_cmp, cache_slc,
      wts["w_c1_halves"], wts["w_c2_full"], wts["ones128"], wts["g_k_cmp"], wts["m_sel_sample"], wts["e_sel_sample"],
      wts["regroup"])


def _gla_kernel(qg_ref, kb_ref, vb_ref, la_ref, rb_ref, s0_ref, ggla_ref, o_ref, sout_ref, s_scr, *, chunk, n_valid):
    j = pl.program_id(1)

    @pl.when(j == 0)
    def _():
        s_scr[...] = s0_ref[0]

    tt = qg_ref.shape[1]
    n_ch = tt // chunk
    shift = int(math.log2(chunk))
    ri = lax.broadcasted_iota(jnp.int32, (tt, tt), 0)
    ci = lax.broadcasted_iota(jnp.int32, (tt, tt), 1)
    same_chunk = lax.shift_right_logical(ri, shift) == lax.shift_right_logical(ci, shift)
    causal = same_chunk & (ri >= ci)
    chunk_of_row = lax.shift_right_logical(lax.broadcasted_iota(jnp.int32, (tt, LANES), 0), shift)
    chunk_lane = lax.broadcasted_iota(jnp.int32, (tt, LANES), 1)
    as_b = lambda mask: jnp.where(mask, 1.0, 0.0).astype(BF16)
    left = lambda a, m: jnp.dot(m, a, preferred_element_type=F32)
    g_gla = ggla_ref[...]

    q, k, v, la = qg_ref[0], kb_ref[0], vb_ref[0], la_ref[0]
    if n_valid < chunk:
        ok = (lax.broadcasted_iota(jnp.int32, (tt, 1), 0) & (chunk - 1)) < n_valid
        q, k, v, la = (jnp.where(ok, a, 0.0) for a in (q, k, v, la))
    bcum = _dot_wide(la, as_b(causal), 3, dot=left)
    bend = jnp.concatenate([jnp.broadcast_to(bcum[(c + 1) * chunk - 1:(c + 1) * chunk, :], (chunk, GK_WIDTH))
                            for c in range(n_ch)], axis=0)
    bend_t = _dot_wide(la, as_b(chunk_of_row == chunk_lane), 3, dot=lambda a, m: lax.dot_general(
        a, m, (((0,), (0,)), ((), ())), preferred_element_type=F32))
    qe = q * jnp.exp(bcum)
    kd = k * jnp.exp(-bcum)
    kl = k * jnp.exp(bend - bcum)
    for h in range(GLA_HEADS):
        kc = slice(h * GLA_DK, (h + 1) * GLA_DK)
        vc = slice(h * GLA_DV, (h + 1) * GLA_DV)
        o_intra = _dot(jnp.where(causal, _dot_nt(qe[:, kc], kd[:, kc]), 0.0), v[:, vc])
        s_h = s_scr[h]
        o_inter = []
        for c in range(n_ch):
            rows = slice(c * chunk, (c + 1) * chunk)
            o_inter.append(_dot(qe[rows, kc], s_h))
            s_h = jnp.exp(bend_t[kc, c:c + 1]) * s_h + _dot_tn(kl[rows, kc], v[rows, vc])
        s_scr[h] = s_h
        o = o_intra + jnp.concatenate(o_inter, axis=0)
        r = rb_ref[0, :, vc]
        o_ref[0, :, vc] = _rms_rows(o, g_gla) * (r * _sigmoid(r))
    sout_ref[0] = s_scr[...]


def _gla(qg, kb, vb, la, rb, s0, wts, layer, tt, chunk, n_valid):
    b, t, _ = qg.shape
    tile = lambda w: pl.BlockSpec((1, tt, w), lambda bi, j: (bi, j, 0))
    st = pl.BlockSpec((1, GLA_HEADS, GLA_DK, GLA_DV), lambda bi, j: (bi, 0, 0, 0))
    return pl.pallas_call(
        functools.partial(_gla_kernel, chunk=chunk, n_valid=n_valid),
        grid=(b, t // tt),
        in_specs=[tile(GK_WIDTH), tile(GK_WIDTH), tile(B_WIDTH), tile(GK_WIDTH), tile(B_WIDTH), st,
                  pl.BlockSpec((None, 1, GLA_DV), lambda bi, j: (layer, 0, 0))],
        out_specs=[tile(B_WIDTH), st],
        out_shape=[jax.ShapeDtypeStruct((b, t, B_WIDTH), F32),
                   jax.ShapeDtypeStruct((b, GLA_HEADS, GLA_DK, GLA_DV), F32)],
        scratch_shapes=[pltpu.VMEM((GLA_HEADS, GLA_DK, GLA_DV), F32)],
        compiler_params=pltpu.CompilerParams(dimension_semantics=("arbitrary", "arbitrary"),
                                             vmem_limit_bytes=VMEM_LIMIT),
        name="gla",
    )(qg, kb, vb, la, rb, s0, wts["g_gla"])


def _mix_ffn_kernel(*refs, tm, f_chunk, slots):
    if slots:
        (x_ref, oa_ref, ob_ref, gattn_ref, gffn_ref, wm_ref, wpa_ref, wpb_ref, wout_ref, wup_ref, wconv_ref,
         bconv_ref, wdown_ref, cb1_ref, cb2_ref, y_ref, u_ref, u_scr) = refs
    else:
        (x_ref, oa_ref, ob_ref, gattn_ref, gffn_ref, wm_ref, wpa_ref, wpb_ref, wout_ref, wup_ref, wconv_ref,
         bconv_ref, wdown_ref, y_ref, u_ref, u_scr) = refs
    x = x_ref[...]
    h = _rms_rows(x, gattn_ref[...])
    mg = jnp.dot(h.astype(BF16), wm_ref[...], preferred_element_type=F32)
    mix = (_sigmoid(mg[:, :D_MODEL]) * jnp.dot(oa_ref[...].astype(BF16), wpa_ref[...], preferred_element_type=F32)
           + _sigmoid(mg[:, D_MODEL:]) * jnp.dot(ob_ref[...].astype(BF16), wpb_ref[...], preferred_element_type=F32))
    x1 = x + jnp.dot(mix.astype(BF16), wout_ref[...], preferred_element_type=F32)
    h2 = _rms_rows(x1, gffn_ref[...]).astype(BF16)

    if slots:
        u_scr[0:SUBLANES, :] = jnp.zeros((SUBLANES, FFN_DIM), F32)
        row_in_slot = lax.broadcasted_iota(jnp.int32, (tm, 1), 0) & (SLOT - 1)
    else:
        @pl.when(pl.program_id(1) == 0)
        def _():
            u_scr[0:SUBLANES, :] = jnp.zeros((SUBLANES, FFN_DIM), F32)

    y = x1
    for c0 in range(0, FFN_DIM, f_chunk):
        fc = slice(c0, c0 + f_chunk)
        u = jnp.dot(h2, wup_ref[:, fc], preferred_element_type=F32)
        gt = jnp.dot(h2, wup_ref[:, FFN_DIM + c0:FFN_DIM + c0 + f_chunk], preferred_element_type=F32)
        u_scr[SUBLANES:SUBLANES + tm, fc] = u
        u1 = u_scr[SUBLANES - 1:SUBLANES - 1 + tm, fc]
        u2 = u_scr[SUBLANES - 2:SUBLANES - 2 + tm, fc]
        if slots:
            u1 = jnp.where(row_in_slot == 0, cb1_ref[:, fc], u1)
            u2 = jnp.where(row_in_slot < 2, cb2_ref[:, fc], u2)
            u_ref[:, fc] = u
        acc = bconv_ref[:, fc] + wconv_ref[0:1, fc] * u2 + wconv_ref[1:2, fc] * u1 + wconv_ref[2:3, fc] * u
        y = y + jnp.dot((_gelu_tanh(acc) * gt).astype(BF16), wdown_ref[fc, :], preferred_element_type=F32)
    y_ref[...] = y
    if not slots:
        tail = u_scr[tm:tm + SUBLANES, :]
        u_scr[0:SUBLANES, :] = tail
        u_ref[0] = tail


def _mix_ffn(x, oa, ob, wts, layer, tm, seq_tiles, cb=None):
    m = x.shape[0]
    slots = cb is not None
    n_seq = m // (tm * seq_tiles)
    if slots:
        idx = lambda i: (i, 0)
        grid = (m // tm,)
        sem = ("arbitrary",)
    else:
        idx = lambda b, i: (b * seq_tiles + i, 0)
        grid = (n_seq, seq_tiles)
        sem = ("arbitrary", "arbitrary")
    nidx = len(grid)

    def wspec(shape):
        return pl.BlockSpec((None,) + shape, lambda *_: (layer,) + (0,) * len(shape),
                            pipeline_mode=pl.Buffered(1))

    row = lambda w: pl.BlockSpec((tm, w), idx)
    in_specs = [row(D_MODEL), row(A_WIDTH), row(B_WIDTH),
                wspec((1, D_MODEL)), wspec((1, D_MODEL)), wspec((D_MODEL, 2 * D_MODEL)),
                wspec((A_WIDTH, D_MODEL)), wspec((B_WIDTH, D_MODEL)), wspec((D_MODEL, D_MODEL)),
                wspec((D_MODEL, 2 * FFN_DIM)), wspec((SUBLANES, FFN_DIM)), wspec((1, FFN_DIM)),
                wspec((FFN_DIM, D_MODEL))]
    args = [x, oa, ob, wts["g_attn"], wts["g_ffn"], wts["w_merge"], wts["w_pa"], wts["w_pb"], wts["w_out"],
            wts["w_up"], wts["w_conv"], wts["b_conv"], wts["w_down"]]
    if slots:
        in_specs += [row(FFN_DIM), row(FFN_DIM)]
        args += list(cb)
        u_spec = row(FFN_DIM)
        u_shape = jax.ShapeDtypeStruct((m, FFN_DIM), F32)
    else:
        u_spec = pl.BlockSpec((1, SUBLANES, FFN_DIM), lambda b, i: (b, 0, 0))
        u_shape = jax.ShapeDtypeStruct((n_seq, SUBLANES, FFN_DIM), F32)
    return pl.pallas_call(
        functools.partial(_mix_ffn_kernel, tm=tm, f_chunk=FFN_DIM // 2, slots=slots),
        grid=grid,
        in_specs=in_specs,
        out_specs=[row(D_MODEL), u_spec],
        out_shape=[jax.ShapeDtypeStruct((m, D_MODEL), F32), u_shape],
        scratch_shapes=[pltpu.VMEM((tm + SUBLANES, FFN_DIM), F32)],
        compiler_params=pltpu.CompilerParams(dimension_semantics=sem, vmem_limit_bytes=VMEM_LIMIT),
        name="mix_ffn",
    )(*args)


def _sel_fold_matrix(n_cmp_valid, n_cmp_rows, n_sel_lanes):
    per = SEL_BLOCK // CMP_STRIDE
    m = np.zeros((n_cmp_rows, n_sel_lanes), np.float32)
    for n in range(n_cmp_valid):
        m[n, n // per] += 1.0
        if n + 1 < n_cmp_valid:
            m[n, (n + 1) // per] += 1.0
    return jnp.asarray(m, BF16)


def _pack_weights(w_in, w_cmp1, w_cmp2, g_q, g_k_cmp, g_k_slc, g_k_win, w_gate_up, b_gate, g_gla, w_pa, w_pb,
                  w_out, g_attn, g_ffn, w_up, w_conv, b_conv, w_down):
    depth = w_in.shape[0]
    splits = (A_WIDTH, KV_WIDTH, KV_WIDTH, KV_WIDTH, 3 * NSA_HEADS, GK_WIDTH, GK_WIDTH, B_WIDTH, B_WIDTH,
              GLA_GATE_RANK, 2 * D_MODEL)
    offs = np.concatenate([[0], np.cumsum(splits)])
    seg = lambda k: w_in[:, :, int(offs[k]):int(offs[k + 1])]
    padc = lambda a, w: jnp.pad(a, ((0, 0), (0, 0), (0, w - a.shape[-1])))
    w_a = jnp.concatenate([seg(0), seg(1), seg(2), seg(3), padc(seg(4), LANES), seg(5), seg(6), seg(7), seg(8),
                           padc(seg(9), LANES)], axis=-1).astype(BF16)
    same_g = jnp.eye(NSA_KV_GROUPS, dtype=F32)
    w_c1 = jnp.einsum("Lclde,gG->LlcgdGe", w_cmp1, same_g).reshape(depth, CMP_BLOCK, 2, LANES, LANES).astype(BF16)
    w_c2 = jnp.einsum("Lcef,gG->LcgeGf", w_cmp2, same_g).reshape(depth, 2, LANES, LANES).astype(BF16)
    w1h = w_cmp1.reshape(depth, 2, 2, CMP_STRIDE, HEAD_DIM, HEAD_DIM).astype(BF16)
    n_cg = 2 * NSA_KV_GROUPS
    zero1 = jnp.zeros((depth, CMP_STRIDE, HEAD_DIM, HEAD_DIM), BF16)
    zero2 = jnp.zeros((depth, HEAD_DIM, HEAD_DIM), BF16)
    w_c1_halves = jnp.concatenate([jnp.concatenate(
        [w1h[:, cg // NSA_KV_GROUPS, half] if cg == row_cg else zero1 for half in range(2) for cg in range(n_cg)],
        axis=-1) for row_cg in range(n_cg)], axis=-2)
    w_c2_full = jnp.concatenate([jnp.concatenate(
        [w_cmp2[:, cg // NSA_KV_GROUPS].astype(BF16) if cg == row_cg else zero2 for cg in range(n_cg)],
        axis=-1) for row_cg in range(n_cg)], axis=-2)
    grp = lambda n: jnp.asarray(np.kron(np.eye(n // HEAD_DIM), np.ones((HEAD_DIM, HEAD_DIM))), BF16)
    g_norm = jnp.concatenate([jnp.tile(g_q, (1, NSA_HEADS)), jnp.tile(g_k_slc, (1, NSA_KV_GROUPS)),
                              jnp.tile(g_k_win, (1, NSA_KV_GROUPS))], axis=-1)
    return {
        "w_a": w_a,
        "w_merge": w_in[:, :, int(offs[10]):].astype(BF16),
        "g_attn": g_attn[:, None, :], "g_ffn": g_ffn[:, None, :],
        "g_norm": g_norm[:, None, :], "ones128": grp(LANES),
        "g_k_cmp": jnp.tile(g_k_cmp, (1, NSA_KV_GROUPS))[:, None, :],
        "w_gate_up": jnp.pad(w_gate_up, ((0, 0), (0, LANES - GLA_GATE_RANK), (0, 0))).astype(BF16),
        "b_gate": b_gate[:, None, :], "g_gla": g_gla[:, None, :],
        "w_c1": w_c1, "w_c2": w_c2, "w_c1_halves": w_c1_halves, "w_c2_full": w_c2_full,
        "w_pa": w_pa.astype(BF16), "w_pb": w_pb.astype(BF16), "w_out": w_out.astype(BF16),
        "w_up": w_up.astype(BF16), "w_down": w_down.astype(BF16),
        "w_conv": jnp.pad(w_conv, ((0, 0), (0, SUBLANES - CONV_W), (0, 0))), "b_conv": b_conv[:, None, :],
    }


def _expand_matrix(n_blocks, n_keys):
    return jnp.asarray(np.arange(n_keys)[None, :] // SEL_BLOCK == np.arange(n_blocks)[:, None], BF16)


def _prompt_layer(x, wts, layer, b, t):
    tm = min(256, t)
    tq = min(256, t)
    q, kv_cmp, kv_slc, kv_win, ga, qg, kb, vb, rb, la = _proj_in(x, wts, layer, tm)
    r3 = lambda a: a.reshape(b, t, a.shape[-1])
    kvc = _compress(r3(kv_cmp), wts, layer)
    o_a = _nsa_prompt(r3(q), r3(ga), kvc, r3(kv_slc), r3(kv_win), wts, tq)
    o_b, s_new = _gla(r3(qg), r3(kb), r3(vb), r3(la), r3(rb),
                      jnp.zeros((b, GLA_HEADS, GLA_DK, GLA_DV), F32), wts, layer,
                      tt=min(512, t), chunk=math.gcd(t, GLA_CHUNK), n_valid=math.gcd(t, GLA_CHUNK))
    y, u_tail = _mix_ffn(x, o_a.reshape(b * t, A_WIDTH), o_b.reshape(b * t, B_WIDTH), wts, layer, tm, t // tm)
    kv6 = lambda a: a.reshape(b, t, 2, NSA_KV_GROUPS, HEAD_DIM)
    n_win = min(WINDOW, t)
    state = (kv6(kv_cmp), kv6(kv_slc), kv6(kv_win)[:, t - n_win:], s_new, u_tail[:, SUBLANES - (CONV_W - 1):])
    return y, state


def _sample_layer(x, wts, layer, bs, n_tok, past, cache_cmp, cache_slc, win_buf, win_rows, gla_state, conv_buf,
                  page_table):
    m = bs * SLOT
    q, kv_cmp, kv_slc, kv_win, ga, qg, kb, vb, rb, la = _proj_in(x, wts, layer, m)
    r3 = lambda a: a.reshape(bs, SLOT, a.shape[-1])
    o_a = _nsa_sample(r3(q), r3(ga), r3(kv_cmp), r3(kv_slc), r3(kv_win), win_buf, cache_cmp, cache_slc,
                      page_table, wts, layer, past, n_tok)
    o_b, s_new = _gla(r3(qg), r3(kb), r3(vb), r3(la), r3(rb), gla_state[layer], wts, layer,
                      tt=SLOT, chunk=SLOT, n_valid=n_tok)
    cb = conv_buf[layer]
    zeros = lambda n: jnp.zeros((bs, n, FFN_DIM), F32)
    cb1 = jnp.concatenate([cb[:, 1:2], zeros(SLOT - 1)], axis=1).reshape(m, FFN_DIM)
    cb2 = jnp.concatenate([cb, zeros(SLOT - 2)], axis=1).reshape(m, FFN_DIM)
    y, u = _mix_ffn(x, o_a.reshape(m, A_WIDTH), o_b.reshape(m, B_WIDTH), wts, layer, m, 1, cb=(cb1, cb2))
    kv6 = lambda a: r3(a)[:, :n_tok].reshape(bs, n_tok, 2, NSA_KV_GROUPS, HEAD_DIM)
    win_state = jnp.concatenate([win_rows[layer][:, n_tok:], kv6(kv_win)], axis=1)
    u_ext = jnp.concatenate([cb, u.reshape(bs, SLOT, FFN_DIM)[:, :n_tok]], axis=1)
    state = (kv6(kv_cmp), kv6(kv_slc), win_state, s_new, u_ext[:, -(CONV_W - 1):])
    return y, state


def kernel(x_prompt, x_sample, cache_kv_cmp, cache_kv_slc, state_kv_win, state_gla, state_conv, page_table,
           w_in, w_cmp1, w_cmp2, g_q, g_k_cmp, g_k_slc, g_k_win, w_gate_up, b_gate, g_gla, w_pa, w_pb, w_out,
           g_attn, g_ffn, w_up, w_conv, b_conv, w_down):
    depth = w_in.shape[0]
    bp, t, _ = x_prompt.shape
    bs, n_tok, _ = x_sample.shape
    page = cache_kv_cmp.shape[2]
    past = page_table.shape[1] * page
    assert n_tok <= SLOT and t % CMP_STRIDE == 0 and past % SEL_BLOCK == 0

    wts = _pack_weights(w_in, w_cmp1, w_cmp2, g_q, g_k_cmp, g_k_slc, g_k_win, w_gate_up, b_gate, g_gla, w_pa,
                        w_pb, w_out, g_attn, g_ffn, w_up, w_conv, b_conv, w_down)
    n_cmp_p = t // CMP_STRIDE
    fold = _sel_fold_matrix(n_cmp_p - 1, n_cmp_p, LANES - LANE_HOT).T
    wts["fold_prompt"] = jnp.pad(fold, ((LANE_HOT, 0), (0, 0)))
    gate_expand = np.zeros((3, LANES, A_WIDTH), np.float32)
    for br in range(3):
        for h in range(NSA_HEADS):
            gate_expand[br, br * NSA_HEADS + h, h * HEAD_DIM:(h + 1) * HEAD_DIM] = 1.0
    wts["gate_expand"] = jnp.asarray(gate_expand, BF16)
    n_cmp_s = (past + CMP_STRIDE) // CMP_STRIDE - 1
    sel_lanes = -(-(past // SEL_BLOCK + 1) // LANES) * LANES
    tk_s = min(1024, past)
    wts["m_sel_sample"] = _sel_fold_matrix(n_cmp_s, n_cmp_s, sel_lanes)
    blk_per_chunk = tk_s // SEL_BLOCK
    wts["e_sel_sample"] = jnp.stack([_expand_matrix(LANES, LANES * SEL_BLOCK)[:, j * tk_s:(j + 1) * tk_s]
                                     for j in range(LANES // blk_per_chunk)])

    cache_cmp = cache_kv_cmp.transpose(0, 1, 3, 4, 5, 2).reshape(depth, -1, KV_WIDTH, page)
    regroup = np.zeros((page, page), np.float32)
    hpp = page // CMP_STRIDE
    for l in range(CMP_STRIDE):
        for h in range(hpp):
            regroup[l * hpp + h, h * CMP_STRIDE + l] = 1.0
    wts["regroup"] = jnp.asarray(regroup, BF16)
    cache_slc = cache_kv_slc.transpose(0, 1, 3, 4, 5, 2)
    win_buf = state_kv_win.transpose(0, 1, 3, 4, 5, 2)

    y_p = x_prompt.reshape(bp * t, D_MODEL)
    y_s = jnp.pad(x_sample, ((0, 0), (0, SLOT - n_tok), (0, 0))).reshape(bs * SLOT, D_MODEL)
    st_p, st_s = [], []
    for l in range(depth):
        y_p, sp = _prompt_layer(y_p, wts, l, bp, t)
        y_s, ss = _sample_layer(y_s, wts, l, bs, n_tok, past, cache_cmp, cache_slc, win_buf, state_kv_win,
                                state_gla, state_conv, page_table)
        st_p.append(sp)
        st_s.append(ss)
    stack = lambda sts, k: jnp.stack([s[k] for s in sts])
    return ((y_p.reshape(bp, t, D_MODEL), y_s.reshape(bs, SLOT, D_MODEL)[:, :n_tok])
            + tuple(stack(st_p, k) for k in range(5)) + tuple(stack(st_s, k) for k in range(5)))
```

```python
import functools
import math

import numpy as np
import jax
import jax.numpy as jnp
from jax import lax
from jax.experimental import pallas as pl
from jax.experimental.pallas import tpu as pltpu

F32 = jnp.float32
BF16 = jnp.bfloat16

D_MODEL = 1024
HEAD_DIM = 64
NSA_HEADS = 8
NSA_KV_GROUPS = 2
NSA_GROUP_SIZE = NSA_HEADS // NSA_KV_GROUPS
CMP_STRIDE = 16
CMP_BLOCK = 2 * CMP_STRIDE
SEL_BLOCK = 64
SEL_TOPK = 16
WINDOW = 512
GLA_HEADS = 4
GLA_DK = 64
GLA_DV = 128
GLA_GATE_RANK = 16
GLA_GATE_NORM = 16.0
GLA_CHUNK = 64
FFN_DIM = 2816
CONV_W = 3
EPS = 1e-6
NEG_INF = -1e30
FORCE_SCORE = 1e9
BELOW_ALL_SCORES = -3e38

A_WIDTH = NSA_HEADS * HEAD_DIM
B_WIDTH = GLA_HEADS * GLA_DV
KV_WIDTH = 2 * NSA_KV_GROUPS * HEAD_DIM
GK_WIDTH = GLA_HEADS * GLA_DK
LANES = 128
SUBLANES = 8
SLOT = SUBLANES
ALIBI = tuple(float(2.0 ** (-8.0 * (h + 1) / NSA_HEADS)) for h in range(NSA_HEADS))

C_Q, C_CMP, C_SLC, C_WIN, C_GATE = 0, 512, 768, 1024, 1280
C_QB, C_KB, C_VB, C_RB, C_LR, C_END = 1408, 1664, 1920, 2432, 2944, 3072
NORM_W = A_WIDTH + 2 * LANES

VMEM_LIMIT = 56 * 1024 * 1024


def _dot(a, b):
    return jnp.dot(a.astype(BF16), b.astype(BF16), preferred_element_type=F32)


def _dot_nt(a, b):
    return lax.dot_general(a.astype(BF16), b.astype(BF16), (((1,), (1,)), ((), ())),
                           preferred_element_type=F32)


def _dot_tn(a, b):
    return lax.dot_general(a.astype(BF16), b.astype(BF16), (((0,), (0,)), ((), ())),
                           preferred_element_type=F32)


def _split_bf16(x, parts):
    out = []
    for _ in range(parts):
        p = x.astype(BF16)
        out.append(p)
        x = x - p.astype(F32)
    return out


def _dot_wide(x, m, parts, dot=None):
    dot = dot or (lambda a, b: jnp.dot(a, b, preferred_element_type=F32))
    acc = None
    for p in _split_bf16(x, parts):
        t = dot(p, m)
        acc = t if acc is None else acc + t
    return acc


def _rms_rows(x, g):
    return x * lax.rsqrt(jnp.mean(x * x, axis=-1, keepdims=True) + EPS) * g


def _rms_groups(x, ones_blockdiag, g):
    sq = x * x
    ms = jnp.concatenate([_dot_wide(sq[:, j:j + LANES], ones_blockdiag, 2)
                          for j in range(0, x.shape[1], LANES)], axis=1) * (1.0 / HEAD_DIM)
    return x * lax.rsqrt(ms + EPS) * g


def _sigmoid(x):
    return 1.0 / (1.0 + jnp.exp(-x))


def _gelu_tanh(x):
    return 0.5 * x * (1.0 + jnp.tanh(math.sqrt(2.0 / math.pi) * (x + 0.044715 * (x * x * x))))


def _log_sigmoid(x):
    return jnp.minimum(x, 0.0) - jnp.log1p(jnp.exp(-jnp.abs(x)))


def _topk_mask(score, n_cand, k):
    lane = lax.broadcasted_iota(jnp.int32, score.shape, 1)
    ahead = jnp.zeros(score.shape, F32)
    for j in range(n_cand):
        col = score[:, j:j + 1]
        ahead = ahead + jnp.where(col > score, 1.0, jnp.where(col == score, jnp.where(lane > j, 1.0, 0.0), 0.0))
    return jnp.where(lane < n_cand, jnp.where(ahead < k, 1.0, 0.0), 0.0)


def _masked_softmax(s, mask):
    s = jnp.where(mask, s, NEG_INF)
    m = jnp.max(s, axis=-1, keepdims=True)
    e = jnp.where(mask, jnp.exp(s - m), 0.0)
    l = jnp.sum(e, axis=-1, keepdims=True)
    return e * jnp.where(l > 0.0, 1.0 / l, 0.0)


def _stack_heads(q, g):
    base = g * NSA_GROUP_SIZE
    return jnp.concatenate([q[:, (base + r) * HEAD_DIM:(base + r + 1) * HEAD_DIM]
                            for r in range(NSA_GROUP_SIZE)], axis=0)


def _head_slopes(g, n):
    row = lax.broadcasted_iota(jnp.int32, (NSA_GROUP_SIZE * n, 1), 0)
    slope = jnp.full((NSA_GROUP_SIZE * n, 1), ALIBI[g * NSA_GROUP_SIZE + NSA_GROUP_SIZE - 1], F32)
    for r in reversed(range(NSA_GROUP_SIZE - 1)):
        slope = jnp.where(row < (r + 1) * n, ALIBI[g * NSA_GROUP_SIZE + r], slope)
    return slope


def _gate_cols(ga, branch, g):
    c0 = branch * NSA_HEADS + g * NSA_GROUP_SIZE
    return jnp.concatenate([ga[:, c0 + r:c0 + r + 1] for r in range(NSA_GROUP_SIZE)], axis=0)


def _block_scores(blk_imp, q_pos, lane_i, n_sel):
    forced = (lane_i == 0) | (lane_i == lax.shift_right_logical(q_pos, int(math.log2(SEL_BLOCK))))
    valid = lane_i * SEL_BLOCK <= q_pos
    score = jnp.where(forced, FORCE_SCORE, jnp.where(valid, blk_imp, NEG_INF))
    return jnp.where(lane_i < n_sel, score, BELOW_ALL_SCORES)


def _compress_rows(load_rows, w1_ref, w2_ref, ones128_ref, gk_ref):
    out = []
    for c in range(2):
        acc = None
        for l in range(CMP_BLOCK):
            t = jnp.dot(load_rows(c, l).astype(BF16), w1_ref[l, c], preferred_element_type=F32)
            acc = t if acc is None else acc + t
        out.append(jnp.dot(_gelu_tanh(acc).astype(BF16), w2_ref[c], preferred_element_type=F32))
    return _rms_groups(out[0], ones128_ref[...], gk_ref[...]), out[1]


def _proj_in_kernel(x_ref, gattn_ref, w_ref, gnorm_ref, ones_ref, wgu_ref, bg_ref,
                    q_ref, cmp_ref, slc_ref, win_ref, ga_ref, qg_ref, kb_ref, vb_ref, rb_ref, la_ref):
    h = _rms_rows(x_ref[...], gattn_ref[...])
    p = jnp.dot(h.astype(BF16), w_ref[...], preferred_element_type=F32)
    to_norm = jnp.concatenate([p[:, C_Q:C_CMP], p[:, C_SLC:C_SLC + LANES], p[:, C_WIN:C_WIN + LANES]], axis=1)
    y = _rms_groups(to_norm, ones_ref[...], gnorm_ref[...])
    q_ref[...] = y[:, :A_WIDTH] * HEAD_DIM ** -0.5
    cmp_ref[...] = p[:, C_CMP:C_SLC]
    slc_ref[...] = jnp.concatenate([y[:, A_WIDTH:A_WIDTH + LANES], p[:, C_SLC + LANES:C_WIN]], axis=1)
    win_ref[...] = jnp.concatenate([y[:, A_WIDTH + LANES:], p[:, C_WIN + LANES:C_GATE]], axis=1)
    ga_ref[...] = _sigmoid(p[:, C_GATE:C_QB])
    qg_ref[...] = p[:, C_QB:C_KB] * GLA_DK ** -0.5
    kb_ref[...] = p[:, C_KB:C_VB]
    vb_ref[...] = p[:, C_VB:C_RB]
    rb_ref[...] = p[:, C_RB:C_LR]
    z = jnp.dot(p[:, C_LR:C_END].astype(BF16), wgu_ref[...], preferred_element_type=F32) + bg_ref[...]
    la_ref[...] = _log_sigmoid(z) * (1.0 / GLA_GATE_NORM)


def _const_spec(shape, layer=None):
    nd = len(shape)
    if layer is None:
        return pl.BlockSpec(shape, lambda *_: (0,) * nd)
    return pl.BlockSpec((None,) + shape, lambda *_: (layer,) + (0,) * nd)


def _proj_in(x, wts, layer, tm):
    m = x.shape[0]
    widths = (A_WIDTH, KV_WIDTH, KV_WIDTH, KV_WIDTH, LANES, GK_WIDTH, GK_WIDTH, B_WIDTH, B_WIDTH, GK_WIDTH)
    row = lambda w: pl.BlockSpec((tm, w), lambda i: (i, 0))
    return pl.pallas_call(
        _proj_in_kernel,
        grid=(m // tm,),
        in_specs=[row(D_MODEL),
                  _const_spec((1, D_MODEL), layer), _const_spec((D_MODEL, C_END), layer),
                  _const_spec((1, NORM_W), layer), _const_spec((LANES, LANES)),
                  _const_spec((LANES, GK_WIDTH), layer), _const_spec((1, GK_WIDTH), layer)],
        out_specs=[row(w) for w in widths],
        out_shape=[jax.ShapeDtypeStruct((m, w), F32) for w in widths],
        compiler_params=pltpu.CompilerParams(dimension_semantics=("arbitrary",), vmem_limit_bytes=VMEM_LIMIT),
        name="proj_in",
    )(x, wts["g_attn"], wts["w_a"], wts["g_norm"], wts["ones128"], wts["w_gate_up"], wts["b_gate"])


def _compress_kernel(kv_ref, w1_ref, w2_ref, ones_ref, gk_ref, out_ref, pad_ref, *, t):
    for c in range(2):
        pad_ref[c, 0:t, :] = kv_ref[0, :, c * LANES:(c + 1) * LANES]
        pad_ref[c, t:t + CMP_STRIDE, :] = jnp.zeros((CMP_STRIDE, LANES), F32)
    n_blk = t // CMP_STRIDE
    kc, vc = _compress_rows(lambda c, l: pad_ref[c, pl.ds(l, n_blk, stride=CMP_STRIDE), :],
                            w1_ref, w2_ref, ones_ref, gk_ref)
    out_ref[0] = jnp.concatenate([kc, vc], axis=1)


def _compress(kv_cmp, wts, layer):
    b, t, _ = kv_cmp.shape
    n_blk = t // CMP_STRIDE
    return pl.pallas_call(
        functools.partial(_compress_kernel, t=t),
        grid=(b,),
        in_specs=[pl.BlockSpec((1, t, KV_WIDTH), lambda i: (i, 0, 0)),
                  _const_spec((CMP_BLOCK, 2, LANES, LANES), layer), _const_spec((2, LANES, LANES), layer),
                  _const_spec((LANES, LANES)), _const_spec((1, LANES), layer)],
        out_specs=pl.BlockSpec((1, n_blk, KV_WIDTH), lambda i: (i, 0, 0)),
        out_shape=jax.ShapeDtypeStruct((b, n_blk, KV_WIDTH), F32),
        scratch_shapes=[pltpu.VMEM((2, t + CMP_STRIDE, LANES), F32)],
        compiler_params=pltpu.CompilerParams(dimension_semantics=("arbitrary",), vmem_limit_bytes=VMEM_LIMIT),
        name="compress",
    )(kv_cmp, wts["w_c1"], wts["w_c2"], wts["ones128"], wts["g_k_cmp"])


MASK_BIG = 2.0 ** 100
LANE_HI, LANE_LO, LANE_ONE, LANE_HOT = HEAD_DIM, HEAD_DIM + 1, HEAD_DIM + 2, HEAD_DIM + 3
SEL_SHIFT = int(math.log2(SEL_BLOCK))


def _group_halves(x, g):
    return x if g == 0 else pltpu.roll(x, HEAD_DIM, axis=1)


def _build_prompt_operands(kvc_ref, slc_ref, win_ref, kc_ref, vc_ref, ks_ref, vs_ref, kw_ref, vw_ref):
    t = slc_ref.shape[1]
    lane = lax.broadcasted_iota(jnp.int32, (t, LANES), 1)
    pos = lax.broadcasted_iota(jnp.int32, (t, LANES), 0)
    blk = lax.shift_right_logical(pos, SEL_SHIFT)
    pos_lanes = jnp.where(lane == LANE_HI, blk, jnp.where(lane == LANE_LO, pos & (SEL_BLOCK - 1),
                                                          jnp.where(lane == LANE_ONE, 1, 0)))
    aux_win = pos_lanes.astype(F32)
    aux_sel = jnp.where(lane - LANE_HOT == blk, 1, pos_lanes).astype(F32)
    n_cmp = kvc_ref.shape[1]
    lane_c = lax.broadcasted_iota(jnp.int32, (n_cmp, LANES), 1)
    blk_c = lax.broadcasted_iota(jnp.int32, (n_cmp, LANES), 0)
    aux_cmp = jnp.where(lane_c == LANE_HI, blk_c, jnp.where((lane_c == LANE_LO) | (lane_c == LANE_ONE), 1, 0)).astype(F32)
    lane_p = lax.broadcasted_iota(jnp.int32, (WINDOW, LANES), 1)
    before_seq = jnp.where(lane_p == LANE_HOT, -MASK_BIG, 0.0).astype(BF16)
    for g in range(NSA_KV_GROUPS):
        ks_ref[g] = jnp.where(lane < HEAD_DIM, _group_halves(slc_ref[0, :, 0:LANES], g), aux_sel).astype(BF16)
        vs_ref[g] = jnp.where(lane < HEAD_DIM, _group_halves(slc_ref[0, :, LANES:], g), 1.0).astype(BF16)
        kw_ref[g, 0:WINDOW, :] = before_seq
        vw_ref[g, 0:WINDOW, :] = jnp.zeros((WINDOW, LANES), BF16)
        kw_ref[g, WINDOW:, :] = jnp.where(lane < HEAD_DIM, _group_halves(win_ref[0, :, 0:LANES], g), aux_win).astype(BF16)
        vw_ref[g, WINDOW:, :] = jnp.where(lane < HEAD_DIM, _group_halves(win_ref[0, :, LANES:], g), 1.0).astype(BF16)
        kc_ref[g] = jnp.where(lane_c < HEAD_DIM, _group_halves(kvc_ref[0, :, 0:LANES], g), aux_cmp).astype(BF16)
        vc_ref[g] = jnp.where(lane_c < HEAD_DIM, _group_halves(kvc_ref[0, :, LANES:], g), 0.0).astype(BF16)


def _nsa_prompt_kernel(q_ref, ga_ref, kvc_ref, slc_ref, win_ref, fold_ref, gexp_ref, o_ref,
                       kc_ref, vc_ref, ks_ref, vs_ref, kw_ref, vw_ref, m_scr, acc_scr, *, tq, kw_sel):
    i = pl.program_id(1)

    @pl.when(i == 0)
    def _():
        _build_prompt_operands(kvc_ref, slc_ref, win_ref, kc_ref, vc_ref, ks_ref, vs_ref, kw_ref, vw_ref)

    q0 = i * tq
    q0f = q0.astype(F32)
    q = q_ref[0]
    rows = NSA_GROUP_SIZE * tq
    n_cmp = kvc_ref.shape[1]
    n_sel = slc_ref.shape[1] // SEL_BLOCK
    k_top = min(SEL_TOPK, n_sel)
    lane = lax.broadcasted_iota(jnp.int32, (tq, LANES), 1)
    lane4 = lax.broadcasted_iota(jnp.int32, (rows, LANES), 1)
    t_w = lax.broadcasted_iota(jnp.int32, (rows, tq), 0) & (tq - 1)
    lane_w = lax.broadcasted_iota(jnp.int32, (rows, tq), 1)
    causal_bias = jnp.where(lane_w <= t_w, 0.0, -MASK_BIG)
    far_bias = jnp.where(lane_w > t_w, 0.0, -MASK_BIG)
    cmp_valid = (lax.broadcasted_iota(jnp.int32, (rows, n_cmp), 1) * CMP_STRIDE + (CMP_BLOCK - 1)
                 <= q0 + (lax.broadcasted_iota(jnp.int32, (rows, n_cmp), 0) & (tq - 1)))
    hot_rows = slice(HEAD_DIM, HEAD_DIM + 8 * (-(-(LANE_HOT - HEAD_DIM + n_sel) // 8)))
    n_hot = hot_rows.stop - hot_rows.start
    blk_t = lax.broadcasted_iota(jnp.int32, (n_hot, tq), 0) - (LANE_HOT - HEAD_DIM)
    qpos_t = q0 + lax.broadcasted_iota(jnp.int32, (n_hot, tq), 1)
    is_blk = (blk_t >= 0) & (blk_t < n_sel)

    def lane_tiles_max(s):
        m = s[:, 0:LANES]
        for j in range(1, s.shape[1] // LANES):
            m = jnp.maximum(m, s[:, j * LANES:(j + 1) * LANES])
        return m

    def normalised(acc):
        return acc * (1.0 / pltpu.roll(acc, HEAD_DIM, axis=1))

    def attend_selected(qa, g):
        n_full = q0 // kw_sel
        k_last = pl.multiple_of(n_full * kw_sel, kw_sel)
        visible = lax.broadcasted_iota(jnp.int32, (rows, kw_sel), 1) <= (
            (q0 - k_last) + (lax.broadcasted_iota(jnp.int32, (rows, kw_sel), 0) & (tq - 1)))
        s = _dot_nt(qa, ks_ref[g, pl.ds(k_last, kw_sel), :]) + jnp.where(visible, 0.0, -MASK_BIG)
        m = jnp.max(lane_tiles_max(s), axis=-1, keepdims=True)
        m_scr[...] = jnp.broadcast_to(m, (rows, LANES))
        acc_scr[...] = _dot(jnp.exp(s - m), vs_ref[g, pl.ds(k_last, kw_sel), :])

        def body(c, _):
            rows_c = pl.ds(pl.multiple_of(c * kw_sel, kw_sel), kw_sel)
            s = _dot_nt(qa, ks_ref[g, rows_c, :])
            m_old = m_scr[...]
            m_new = jnp.maximum(m_old, jnp.max(lane_tiles_max(s), axis=-1, keepdims=True))
            p = jnp.exp(s - jnp.concatenate([m_new] * (kw_sel // LANES), axis=1))
            acc_scr[...] = jnp.exp(m_old - m_new) * acc_scr[...] + _dot(p, vs_ref[g, rows_c, :])
            m_scr[...] = m_new
            return 0

        lax.fori_loop(0, n_full, body, 0)
        return normalised(acc_scr[...])

    def attend_window(qa, g):
        k0 = pl.multiple_of(q0, tq)
        s = _dot_nt(qa, kw_ref[g, pl.ds(k0, WINDOW + tq), :])
        s = jnp.concatenate([s[:, 0:tq] + far_bias, s[:, tq:WINDOW], s[:, WINDOW:] + causal_bias], axis=1)
        m = jnp.max(lane_tiles_max(s), axis=-1, keepdims=True)
        return normalised(_dot(jnp.exp(s - m), vw_ref[g, pl.ds(k0, WINDOW + tq), :]))

    branch_out = [[], [], []]
    for g in range(NSA_KV_GROUPS):
        q_rows, base_rows, cbase_rows = [], [], []
        for r in range(NSA_GROUP_SIZE):
            h = g * NSA_GROUP_SIZE + r
            tile = q[:, (h // 2) * LANES:(h // 2 + 1) * LANES]
            q_rows.append(tile if h % 2 == 0 else pltpu.roll(tile, HEAD_DIM, axis=1))
            sl = ALIBI[h]
            base_rows.append(jnp.where(lane == LANE_HI, SEL_BLOCK * sl, jnp.where(
                lane == LANE_LO, sl, jnp.where(lane == LANE_ONE, -sl * q0f, 0.0))))
            cbase_rows.append(jnp.where(lane == LANE_HI, CMP_STRIDE * sl, jnp.where(
                lane == LANE_LO, (CMP_BLOCK - 1) * sl, jnp.where(lane == LANE_ONE, -sl * q0f, 0.0))))
        q4 = jnp.concatenate(q_rows, axis=0)
        base4 = jnp.concatenate(base_rows, axis=0)
        head_lanes = lane4 < HEAD_DIM

        qa = jnp.where(head_lanes, q4, jnp.concatenate(cbase_rows, axis=0)).astype(BF16)
        s = jnp.where(cmp_valid, _dot_nt(qa, kc_ref[g]), -MASK_BIG)
        m = jnp.max(s, axis=-1, keepdims=True)
        e = jnp.where(cmp_valid, jnp.exp(s - m), 0.0)
        l = jnp.sum(e, axis=-1, keepdims=True)
        p = e * jnp.where(l > 0.0, 1.0 / l, 0.0)
        branch_out[0].append(_dot(p, vc_ref[g]))
        imp = p[0:tq]
        for r in range(1, NSA_GROUP_SIZE):
            imp = imp + p[r * tq:(r + 1) * tq]

        blk_imp = _dot_wide(imp, fold_ref[...], 3, dot=lambda a, mt: _dot_nt(mt, a))[hot_rows]
        forced = (blk_t == 0) | (blk_t == lax.shift_right_logical(qpos_t, SEL_SHIFT))
        score = jnp.where(forced, FORCE_SCORE, jnp.where(blk_t * SEL_BLOCK <= qpos_t, blk_imp, NEG_INF))
        score = jnp.where(is_blk, score, BELOW_ALL_SCORES)
        ahead = jnp.zeros((n_hot, tq), F32)
        for j in range(n_sel):
            row_j = score[LANE_HOT - HEAD_DIM + j:LANE_HOT - HEAD_DIM + j + 1, :]
            ahead = ahead + jnp.where(row_j > score, 1.0, jnp.where(row_j == score, jnp.where(blk_t > j, 1.0, 0.0), 0.0))
        sel_bias_t = jnp.where(is_blk & (ahead >= k_top), -MASK_BIG, 0.0)
        sel_bias = jnp.transpose(jnp.concatenate(
            [jnp.zeros((hot_rows.start, tq), F32), sel_bias_t, jnp.zeros((LANES - hot_rows.stop, tq), F32)], axis=0))
        sel_bias4 = jnp.concatenate([sel_bias] * NSA_GROUP_SIZE, axis=0)

        qa = jnp.where(head_lanes, q4, base4 + sel_bias4).astype(BF16)
        branch_out[1].append(attend_selected(qa, g))
        qa = jnp.where(head_lanes, q4, jnp.where(lane4 == LANE_HOT, 1.0, base4)).astype(BF16)
        branch_out[2].append(attend_window(qa, g))

    ga = ga_ref[0]
    out = None
    for br in range(3):
        tiles = []
        for g in range(NSA_KV_GROUPS):
            o4 = branch_out[br][g]
            for r in range(0, NSA_GROUP_SIZE, 2):
                even, odd = o4[r * tq:(r + 1) * tq], o4[(r + 1) * tq:(r + 2) * tq]
                tiles.append(jnp.where(lane < HEAD_DIM, even, pltpu.roll(odd, HEAD_DIM, axis=1)))
        term = _dot_wide(ga, gexp_ref[br], 3) * jnp.concatenate(tiles, axis=1)
        out = term if out is None else out + term
    o_ref[0] = out


def _nsa_prompt(q, ga, kvc, kv_slc, kv_win, wts, tq):
    b, t, _ = q.shape
    n_cmp = kvc.shape[1]
    tile = lambda w: pl.BlockSpec((1, tq, w), lambda bi, i: (bi, i, 0))
    full = lambda n, w: pl.BlockSpec((1, n, w), lambda bi, i: (bi, 0, 0))
    kw_sel = min(max(512, tq), t)
    assert tq % LANES == 0 and WINDOW % tq == 0 and tq < WINDOW and kw_sel % tq == 0 and t % kw_sel == 0
    assert LANE_HOT + t // SEL_BLOCK <= LANES
    aug = lambda n: pltpu.VMEM((NSA_KV_GROUPS, n, LANES), BF16)
    stat = pltpu.VMEM((NSA_GROUP_SIZE * tq, LANES), F32)
    return pl.pallas_call(
        functools.partial(_nsa_prompt_kernel, tq=tq, kw_sel=kw_sel),
        grid=(b, t // tq),
        in_specs=[tile(A_WIDTH), tile(LANES), full(n_cmp, KV_WIDTH), full(t, KV_WIDTH), full(t, KV_WIDTH),
                  _const_spec((LANES, n_cmp)), _const_spec((3, LANES, A_WIDTH))],
        out_specs=tile(A_WIDTH),
        out_shape=jax.ShapeDtypeStruct((b, t, A_WIDTH), F32),
        scratch_shapes=[aug(n_cmp), aug(n_cmp), aug(t), aug(t), aug(t + WINDOW), aug(t + WINDOW), stat, stat],
        compiler_params=pltpu.CompilerParams(dimension_semantics=("arbitrary", "arbitrary"),
                                             vmem_limit_bytes=VMEM_LIMIT),
        name="nsa_prompt",
    )(q, ga, kvc, kv_slc, kv_win, wts["fold_prompt"], wts["gate_expand"])


def _nsa_sample_kernel(pt_ref, q_ref, ga_ref, ncmp_ref, nslc_ref, nwin_ref, wbuf_ref, cmp_hbm, slc_hbm,
                       w1_ref, w2_ref, ones_ref, gk_ref, msel_ref, esel_ref, regroup_ref,
                       o_ref, xcn_ref, xc_ref, xs_ref, half_scr, sem_ref, *, layer, past, n_tok, tk):
    b = pl.program_id(0)
    nb = pl.num_programs(0)
    n_pages = pt_ref.shape[1]
    page = past // n_pages
    hpp = page // CMP_STRIDE
    n_cmp = past // CMP_STRIDE
    slot = b % 2

    def page_copies(bb, sl):
        out = []
        for p in range(n_pages):
            pg = pt_ref[bb, p]
            out.append(pltpu.make_async_copy(cmp_hbm.at[layer, pg], xcn_ref.at[sl, :, pl.ds(p * page, page)],
                                             sem_ref.at[0, sl]))
            out.append(pltpu.make_async_copy(slc_hbm.at[layer, pg], xs_ref.at[sl, :, :, :, pl.ds(p * page, page)],
                                             sem_ref.at[1, sl]))
        return out

    def rows_on_lanes(new_ref):
        new = jnp.where(is_tok, new_ref[0], 0.0)
        return jnp.transpose(jnp.concatenate([new, jnp.zeros((LANES - SLOT, KV_WIDTH), F32)], axis=0))

    @pl.when(b == 0)
    def _():
        for cp in page_copies(0, 0):
            cp.start()

    @pl.when(b + 1 < nb)
    def _():
        for cp in page_copies(b + 1, 1 - slot):
            cp.start()

    is_tok = lax.broadcasted_iota(jnp.int32, (SLOT, 1), 0) < n_tok
    xc_ref[:, n_cmp:n_cmp + SUBLANES, :] = jnp.zeros((CMP_STRIDE, SUBLANES, KV_WIDTH), F32)
    for l in range(n_tok):
        xc_ref[l, n_cmp:n_cmp + 1, :] = ncmp_ref[0, l:l + 1, :]
    new_slc = rows_on_lanes(nslc_ref)
    new_win = rows_on_lanes(nwin_ref)
    for c in range(2):
        for g in range(NSA_KV_GROUPS):
            r0 = (c * NSA_KV_GROUPS + g) * HEAD_DIM
            xs_ref[slot, c, g, :, past:past + LANES] = new_slc[r0:r0 + HEAD_DIM, :]
    n_buf = wbuf_ref.shape[-1]

    for cp in page_copies(b, slot):
        cp.wait()

    q = q_ref[0]
    ga = ga_ref[0]
    rows = NSA_GROUP_SIZE * SLOT
    n_sel = past // SEL_BLOCK + 1
    q_pos = past + lax.broadcasted_iota(jnp.int32, (SLOT, 1), 0)
    q_pos4 = jnp.concatenate([q_pos] * NSA_GROUP_SIZE, axis=0)
    sel_lanes = msel_ref.shape[1]
    lane_i = lax.broadcasted_iota(jnp.int32, (1, sel_lanes), 1)
    ends = lax.broadcasted_iota(jnp.int32, (1, n_cmp), 1) * CMP_STRIDE + (CMP_BLOCK - 1)

    for p in range(n_pages):
        regrouped = _dot_nt(regroup_ref[...], xcn_ref[slot, :, p * page:(p + 1) * page])
        for l in range(CMP_STRIDE):
            xc_ref[l, p * hpp:(p + 1) * hpp, :] = regrouped[l * hpp:(l + 1) * hpp, :]

    ab = None
    for l in range(CMP_STRIDE):
        t = jnp.dot(xc_ref[l].astype(BF16), w1_ref[l], preferred_element_type=F32)
        ab = t if ab is None else ab + t
    half_scr[...] = ab[:, KV_WIDTH:]
    hid = ab[0:n_cmp, :KV_WIDTH] + half_scr[1:n_cmp + 1, :]
    kvc = jnp.dot(_gelu_tanh(hid).astype(BF16), w2_ref[...], preferred_element_type=F32)
    kc = _rms_groups(kvc[:, :LANES], ones_ref[...], gk_ref[...])
    vc = kvc[:, LANES:]
    pieces = []
    for g in range(NSA_KV_GROUPS):
        slope = _head_slopes(g, SLOT)
        qg = _stack_heads(q, g).astype(BF16)
        kcol = slice(g * HEAD_DIM, (g + 1) * HEAD_DIM)
        vcol = slice(LANES + g * HEAD_DIM, LANES + (g + 1) * HEAD_DIM)

        dist = q_pos4 - ends
        s = _dot_nt(qg, kc[:, kcol]) - slope * dist.astype(F32)
        p = _masked_softmax(s, dist >= 0)
        o_cmp = _dot(p, vc[:, kcol])
        imp = p[0:SLOT]
        for r in range(1, NSA_GROUP_SIZE):
            imp = imp + p[r * SLOT:(r + 1) * SLOT]

        blk_imp = _dot_wide(imp, msel_ref[...], 3)
        sel = _topk_mask(_block_scores(blk_imp, q_pos, lane_i, n_sel), n_sel, min(SEL_TOPK, n_sel))
        sel4 = jnp.concatenate([sel] * NSA_GROUP_SIZE, axis=0).astype(BF16)

        chunks_per_slab = esel_ref.shape[0]
        bounds = [(c * tk, tk if c * tk < past else LANES) for c in range(past // tk + 1)]
        scores = []
        for c, (k0, n_k) in enumerate(bounds):
            dist = q_pos4 - (k0 + lax.broadcasted_iota(jnp.int32, (1, n_k), 1))
            s = _dot(qg, xs_ref[slot, 0, g, :, k0:k0 + n_k]) - slope * dist.astype(F32)
            if k0 < past:
                slab = c // chunks_per_slab
                in_sel = _dot(sel4[:, slab * LANES:(slab + 1) * LANES], esel_ref[c % chunks_per_slab])
            else:
                in_sel = sel4[:, n_sel - 1:n_sel].astype(F32)
            scores.append(jnp.where((in_sel > 0.5) & (dist >= 0), s, NEG_INF))
        m = functools.reduce(jnp.maximum, [jnp.max(s, axis=-1, keepdims=True) for s in scores])
        l = jnp.zeros((rows, 1), F32)
        o_slc = jnp.zeros((rows, HEAD_DIM), F32)
        for s, (k0, n_k) in zip(scores, bounds):
            p = jnp.exp(s - m)
            l = l + jnp.sum(p, axis=-1, keepdims=True)
            o_slc = o_slc + _dot_nt(p, xs_ref[slot, 1, g, :, k0:k0 + n_k])
        o_slc = o_slc * (1.0 / l)

        k_pos = (past - n_buf) + lax.broadcasted_iota(jnp.int32, (1, n_buf + LANES), 1)
        dist = q_pos4 - k_pos
        s = jnp.concatenate([_dot(qg, wbuf_ref[0, 0, g]), _dot(qg, new_win[kcol, :])], axis=1) - slope * dist.astype(F32)
        p = _masked_softmax(s, (dist >= 0) & (dist < WINDOW))
        o_win = _dot_nt(p[:, :n_buf], wbuf_ref[0, 1, g]) + _dot_nt(p[:, n_buf:], new_win[vcol, :])

        og = _gate_cols(ga, 0, g) * o_cmp + _gate_cols(ga, 1, g) * o_slc + _gate_cols(ga, 2, g) * o_win
        pieces += [og[r * SLOT:(r + 1) * SLOT] for r in range(NSA_GROUP_SIZE)]
    o_ref[0] = jnp.concatenate(pieces, axis=1)


def _nsa_sample(q, ga, ncmp, nslc, nwin, win_buf, cache_cmp, cache_slc, page_table, wts, layer, past, n_tok):
    bs = q.shape[0]
    n_buf = win_buf.shape[-1]
    n_cmp = past // CMP_STRIDE
    tk = wts["e_sel_sample"].shape[2]
    slot_spec = lambda w: pl.BlockSpec((1, SLOT, w), lambda b, pt: (b, 0, 0))
    const = lambda shape, lyr=None: (
        pl.BlockSpec(shape, lambda b, pt: (0,) * len(shape), pipeline_mode=pl.Buffered(1)) if lyr is None else
        pl.BlockSpec((None,) + shape, lambda b, pt: (lyr,) + (0,) * len(shape), pipeline_mode=pl.Buffered(1)))
    page = cache_cmp.shape[-1]
    grid_spec = pltpu.PrefetchScalarGridSpec(
        num_scalar_prefetch=1,
        grid=(bs,),
        in_specs=[slot_spec(A_WIDTH), slot_spec(LANES), slot_spec(KV_WIDTH), slot_spec(KV_WIDTH), slot_spec(KV_WIDTH),
                  pl.BlockSpec((None, 1, 2, NSA_KV_GROUPS, HEAD_DIM, n_buf), lambda b, pt: (layer, b, 0, 0, 0, 0)),
                  pl.BlockSpec(memory_space=pl.ANY), pl.BlockSpec(memory_space=pl.ANY),
                  const((CMP_STRIDE, KV_WIDTH, 2 * KV_WIDTH), layer), const((KV_WIDTH, KV_WIDTH), layer),
                  const((LANES, LANES)), const((1, LANES), layer),
                  const(wts["m_sel_sample"].shape), const(wts["e_sel_sample"].shape), const((page, page))],
        out_specs=slot_spec(A_WIDTH),
        scratch_shapes=[pltpu.VMEM((2, KV_WIDTH, past), F32),
                        pltpu.VMEM((CMP_STRIDE, n_cmp + SUBLANES, KV_WIDTH), F32),
                        pltpu.VMEM((2, 2, NSA_KV_GROUPS, HEAD_DIM, past + LANES), F32),
                        pltpu.VMEM((n_cmp + SUBLANES, KV_WIDTH), F32),
                        pltpu.SemaphoreType.DMA((2, 2))],
    )
    return pl.pallas_call(
        functools.partial(_nsa_sample_kernel, layer=layer, past=past, n_tok=n_tok, tk=tk),
        grid_spec=grid_spec,
        out_shape=jax.ShapeDtypeStruct((bs, SLOT, A_WIDTH), F32),
        compiler_params=pltpu.CompilerParams(dimension_semantics=("arbitrary",), vmem_limit_bytes=VMEM_LIMIT),
        name="nsa_sample",
    )(page_table, q, ga, ncmp, nslc, nwin, win_buf, cache_cmp, cache_slc,
      wts["w_c1_halves"], wts["w_c2_full"], wts["ones128"], wts["g_k_cmp"], wts["m_sel_sample"], wts["e_sel_sample"],
      wts["regroup"])


def _gla_kernel(qg_ref, kb_ref, vb_ref, la_ref, rb_ref, s0_ref, ggla_ref, o_ref, sout_ref, s_scr, *, chunk, n_valid):
    j = pl.program_id(1)

    @pl.when(j == 0)
    def _():
        s_scr[...] = s0_ref[0]

    tt = qg_ref.shape[1]
    n_ch = tt // chunk
    shift = int(math.log2(chunk))
    ri = lax.broadcasted_iota(jnp.int32, (tt, tt), 0)
    ci = lax.broadcasted_iota(jnp.int32, (tt, tt), 1)
    same_chunk = lax.shift_right_logical(ri, shift) == lax.shift_right_logical(ci, shift)
    causal = same_chunk & (ri >= ci)
    chunk_of_row = lax.shift_right_logical(lax.broadcasted_iota(jnp.int32, (tt, LANES), 0), shift)
    chunk_lane = lax.broadcasted_iota(jnp.int32, (tt, LANES), 1)
    as_b = lambda mask: jnp.where(mask, 1.0, 0.0).astype(BF16)
    left = lambda a, m: jnp.dot(m, a, preferred_element_type=F32)
    g_gla = ggla_ref[...]

    q, k, v, la = qg_ref[0], kb_ref[0], vb_ref[0], la_ref[0]
    if n_valid < chunk:
        ok = (lax.broadcasted_iota(jnp.int32, (tt, 1), 0) & (chunk - 1)) < n_valid
        q, k, v, la = (jnp.where(ok, a, 0.0) for a in (q, k, v, la))
    bcum = _dot_wide(la, as_b(causal), 3, dot=left)
    bend = jnp.concatenate([jnp.broadcast_to(bcum[(c + 1) * chunk - 1:(c + 1) * chunk, :], (chunk, GK_WIDTH))
                            for c in range(n_ch)], axis=0)
    bend_t = _dot_wide(la, as_b(chunk_of_row == chunk_lane), 3, dot=lambda a, m: lax.dot_general(
        a, m, (((0,), (0,)), ((), ())), preferred_element_type=F32))
    qe = q * jnp.exp(bcum)
    kd = k * jnp.exp(-bcum)
    kl = k * jnp.exp(bend - bcum)
    for h in range(GLA_HEADS):
        kc = slice(h * GLA_DK, (h + 1) * GLA_DK)
        vc = slice(h * GLA_DV, (h + 1) * GLA_DV)
        o_intra = _dot(jnp.where(causal, _dot_nt(qe[:, kc], kd[:, kc]), 0.0), v[:, vc])
        s_h = s_scr[h]
        o_inter = []
        for c in range(n_ch):
            rows = slice(c * chunk, (c + 1) * chunk)
            o_inter.append(_dot(qe[rows, kc], s_h))
            s_h = jnp.exp(bend_t[kc, c:c + 1]) * s_h + _dot_tn(kl[rows, kc], v[rows, vc])
        s_scr[h] = s_h
        o = o_intra + jnp.concatenate(o_inter, axis=0)
        r = rb_ref[0, :, vc]
        o_ref[0, :, vc] = _rms_rows(o, g_gla) * (r * _sigmoid(r))
    sout_ref[0] = s_scr[...]


def _gla(qg, kb, vb, la, rb, s0, wts, layer, tt, chunk, n_valid):
    b, t, _ = qg.shape
    tile = lambda w: pl.BlockSpec((1, tt, w), lambda bi, j: (bi, j, 0))
    st = pl.BlockSpec((1, GLA_HEADS, GLA_DK, GLA_DV), lambda bi, j: (bi, 0, 0, 0))
    return pl.pallas_call(
        functools.partial(_gla_kernel, chunk=chunk, n_valid=n_valid),
        grid=(b, t // tt),
        in_specs=[tile(GK_WIDTH), tile(GK_WIDTH), tile(B_WIDTH), tile(GK_WIDTH), tile(B_WIDTH), st,
                  pl.BlockSpec((None, 1, GLA_DV), lambda bi, j: (layer, 0, 0))],
        out_specs=[tile(B_WIDTH), st],
        out_shape=[jax.ShapeDtypeStruct((b, t, B_WIDTH), F32),
                   jax.ShapeDtypeStruct((b, GLA_HEADS, GLA_DK, GLA_DV), F32)],
        scratch_shapes=[pltpu.VMEM((GLA_HEADS, GLA_DK, GLA_DV), F32)],
        compiler_params=pltpu.CompilerParams(dimension_semantics=("arbitrary", "arbitrary"),
                                             vmem_limit_bytes=VMEM_LIMIT),
        name="gla",
    )(qg, kb, vb, la, rb, s0, wts["g_gla"])


def _mix_ffn_kernel(*refs, tm, f_chunk, slots):
    if slots:
        (x_ref, oa_ref, ob_ref, gattn_ref, gffn_ref, wm_ref, wpa_ref, wpb_ref, wout_ref, wup_ref, wconv_ref,
         bconv_ref, wdown_ref, cb1_ref, cb2_ref, y_ref, u_ref, u_scr) = refs
    else:
        (x_ref, oa_ref, ob_ref, gattn_ref, gffn_ref, wm_ref, wpa_ref, wpb_ref, wout_ref, wup_ref, wconv_ref,
         bconv_ref, wdown_ref, y_ref, u_ref, u_scr) = refs
    x = x_ref[...]
    h = _rms_rows(x, gattn_ref[...])
    mg = jnp.dot(h.astype(BF16), wm_ref[...], preferred_element_type=F32)
    mix = (_sigmoid(mg[:, :D_MODEL]) * jnp.dot(oa_ref[...].astype(BF16), wpa_ref[...], preferred_element_type=F32)
           + _sigmoid(mg[:, D_MODEL:]) * jnp.dot(ob_ref[...].astype(BF16), wpb_ref[...], preferred_element_type=F32))
    x1 = x + jnp.dot(mix.astype(BF16), wout_ref[...], preferred_element_type=F32)
    h2 = _rms_rows(x1, gffn_ref[...]).astype(BF16)

    if slots:
        u_scr[0:SUBLANES, :] = jnp.zeros((SUBLANES, FFN_DIM), F32)
        row_in_slot = lax.broadcasted_iota(jnp.int32, (tm, 1), 0) & (SLOT - 1)
    else:
        @pl.when(pl.program_id(1) == 0)
        def _():
            u_scr[0:SUBLANES, :] = jnp.zeros((SUBLANES, FFN_DIM), F32)

    y = x1
    for c0 in range(0, FFN_DIM, f_chunk):
        fc = slice(c0, c0 + f_chunk)
        u = jnp.dot(h2, wup_ref[:, fc], preferred_element_type=F32)
        gt = jnp.dot(h2, wup_ref[:, FFN_DIM + c0:FFN_DIM + c0 + f_chunk], preferred_element_type=F32)
        u_scr[SUBLANES:SUBLANES + tm, fc] = u
        u1 = u_scr[SUBLANES - 1:SUBLANES - 1 + tm, fc]
        u2 = u_scr[SUBLANES - 2:SUBLANES - 2 + tm, fc]
        if slots:
            u1 = jnp.where(row_in_slot == 0, cb1_ref[:, fc], u1)
            u2 = jnp.where(row_in_slot < 2, cb2_ref[:, fc], u2)
            u_ref[:, fc] = u
        acc = bconv_ref[:, fc] + wconv_ref[0:1, fc] * u2 + wconv_ref[1:2, fc] * u1 + wconv_ref[2:3, fc] * u
        y = y + jnp.dot((_gelu_tanh(acc) * gt).astype(BF16), wdown_ref[fc, :], preferred_element_type=F32)
    y_ref[...] = y
    if not slots:
        tail = u_scr[tm:tm + SUBLANES, :]
        u_scr[0:SUBLANES, :] = tail
        u_ref[0] = tail


def _mix_ffn(x, oa, ob, wts, layer, tm, seq_tiles, cb=None):
    m = x.shape[0]
    slots = cb is not None
    n_seq = m // (tm * seq_tiles)
    if slots:
        idx = lambda i: (i, 0)
        grid = (m // tm,)
        sem = ("arbitrary",)
    else:
        idx = lambda b, i: (b * seq_tiles + i, 0)
        grid = (n_seq, seq_tiles)
        sem = ("arbitrary", "arbitrary")
    nidx = len(grid)

    def wspec(shape):
        return pl.BlockSpec((None,) + shape, lambda *_: (layer,) + (0,) * len(shape),
                            pipeline_mode=pl.Buffered(1))

    row = lambda w: pl.BlockSpec((tm, w), idx)
    in_specs = [row(D_MODEL), row(A_WIDTH), row(B_WIDTH),
                wspec((1, D_MODEL)), wspec((1, D_MODEL)), wspec((D_MODEL, 2 * D_MODEL)),
                wspec((A_WIDTH, D_MODEL)), wspec((B_WIDTH, D_MODEL)), wspec((D_MODEL, D_MODEL)),
                wspec((D_MODEL, 2 * FFN_DIM)), wspec((SUBLANES, FFN_DIM)), wspec((1, FFN_DIM)),
                wspec((FFN_DIM, D_MODEL))]
    args = [x, oa, ob, wts["g_attn"], wts["g_ffn"], wts["w_merge"], wts["w_pa"], wts["w_pb"], wts["w_out"],
            wts["w_up"], wts["w_conv"], wts["b_conv"], wts["w_down"]]
    if slots:
        in_specs += [row(FFN_DIM), row(FFN_DIM)]
        args += list(cb)
        u_spec = row(FFN_DIM)
        u_shape = jax.ShapeDtypeStruct((m, FFN_DIM), F32)
    else:
        u_spec = pl.BlockSpec((1, SUBLANES, FFN_DIM), lambda b, i: (b, 0, 0))
        u_shape = jax.ShapeDtypeStruct((n_seq, SUBLANES, FFN_DIM), F32)
    return pl.pallas_call(
        functools.partial(_mix_ffn_kernel, tm=tm, f_chunk=FFN_DIM // 2, slots=slots),
        grid=grid,
        in_specs=in_specs,
        out_specs=[row(D_MODEL), u_spec],
        out_shape=[jax.ShapeDtypeStruct((m, D_MODEL), F32), u_shape],
        scratch_shapes=[pltpu.VMEM((tm + SUBLANES, FFN_DIM), F32)],
        compiler_params=pltpu.CompilerParams(dimension_semantics=sem, vmem_limit_bytes=VMEM_LIMIT),
        name="mix_ffn",
    )(*args)


def _sel_fold_matrix(n_cmp_valid, n_cmp_rows, n_sel_lanes):
    per = SEL_BLOCK // CMP_STRIDE
    m = np.zeros((n_cmp_rows, n_sel_lanes), np.float32)
    for n in range(n_cmp_valid):
        m[n, n // per] += 1.0
        if n + 1 < n_cmp_valid:
            m[n, (n + 1) // per] += 1.0
    return jnp.asarray(m, BF16)


def _pack_weights(w_in, w_cmp1, w_cmp2, g_q, g_k_cmp, g_k_slc, g_k_win, w_gate_up, b_gate, g_gla, w_pa, w_pb,
                  w_out, g_attn, g_ffn, w_up, w_conv, b_conv, w_down):
    depth = w_in.shape[0]
    splits = (A_WIDTH, KV_WIDTH, KV_WIDTH, KV_WIDTH, 3 * NSA_HEADS, GK_WIDTH, GK_WIDTH, B_WIDTH, B_WIDTH,
              GLA_GATE_RANK, 2 * D_MODEL)
    offs = np.concatenate([[0], np.cumsum(splits)])
    seg = lambda k: w_in[:, :, int(offs[k]):int(offs[k + 1])]
    padc = lambda a, w: jnp.pad(a, ((0, 0), (0, 0), (0, w - a.shape[-1])))
    w_a = jnp.concatenate([seg(0), seg(1), seg(2), seg(3), padc(seg(4), LANES), seg(5), seg(6), seg(7), seg(8),
                           padc(seg(9), LANES)], axis=-1).astype(BF16)
    same_g = jnp.eye(NSA_KV_GROUPS, dtype=F32)
    w_c1 = jnp.einsum("Lclde,gG->LlcgdGe", w_cmp1, same_g).reshape(depth, CMP_BLOCK, 2, LANES, LANES).astype(BF16)
    w_c2 = jnp.einsum("Lcef,gG->LcgeGf", w_cmp2, same_g).reshape(depth, 2, LANES, LANES).astype(BF16)
    w1h = w_cmp1.reshape(depth, 2, 2, CMP_STRIDE, HEAD_DIM, HEAD_DIM).astype(BF16)
    n_cg = 2 * NSA_KV_GROUPS
    zero1 = jnp.zeros((depth, CMP_STRIDE, HEAD_DIM, HEAD_DIM), BF16)
    zero2 = jnp.zeros((depth, HEAD_DIM, HEAD_DIM), BF16)
    w_c1_halves = jnp.concatenate([jnp.concatenate(
        [w1h[:, cg // NSA_KV_GROUPS, half] if cg == row_cg else zero1 for half in range(2) for cg in range(n_cg)],
        axis=-1) for row_cg in range(n_cg)], axis=-2)
    w_c2_full = jnp.concatenate([jnp.concatenate(
        [w_cmp2[:, cg // NSA_KV_GROUPS].astype(BF16) if cg == row_cg else zero2 for cg in range(n_cg)],
        axis=-1) for row_cg in range(n_cg)], axis=-2)
    grp = lambda n: jnp.asarray(np.kron(np.eye(n // HEAD_DIM), np.ones((HEAD_DIM, HEAD_DIM))), BF16)
    g_norm = jnp.concatenate([jnp.tile(g_q, (1, NSA_HEADS)), jnp.tile(g_k_slc, (1, NSA_KV_GROUPS)),
                              jnp.tile(g_k_win, (1, NSA_KV_GROUPS))], axis=-1)
    return {
        "w_a": w_a,
        "w_merge": w_in[:, :, int(offs[10]):].astype(BF16),
        "g_attn": g_attn[:, None, :], "g_ffn": g_ffn[:, None, :],
        "g_norm": g_norm[:, None, :], "ones128": grp(LANES),
        "g_k_cmp": jnp.tile(g_k_cmp, (1, NSA_KV_GROUPS))[:, None, :],
        "w_gate_up": jnp.pad(w_gate_up, ((0, 0), (0, LANES - GLA_GATE_RANK), (0, 0))).astype(BF16),
        "b_gate": b_gate[:, None, :], "g_gla": g_gla[:, None, :],
        "w_c1": w_c1, "w_c2": w_c2, "w_c1_halves": w_c1_halves, "w_c2_full": w_c2_full,
        "w_pa": w_pa.astype(BF16), "w_pb": w_pb.astype(BF16), "w_out": w_out.astype(BF16),
        "w_up": w_up.astype(BF16), "w_down": w_down.astype(BF16),
        "w_conv": jnp.pad(w_conv, ((0, 0), (0, SUBLANES - CONV_W), (0, 0))), "b_conv": b_conv[:, None, :],
    }


def _expand_matrix(n_blocks, n_keys):
    return jnp.asarray(np.arange(n_keys)[None, :] // SEL_BLOCK == np.arange(n_blocks)[:, None], BF16)


def _prompt_layer(x, wts, layer, b, t):
    tm = min(256, t)
    tq = min(256, t)
    q, kv_cmp, kv_slc, kv_win, ga, qg, kb, vb, rb, la = _proj_in(x, wts, layer, tm)
    r3 = lambda a: a.reshape(b, t, a.shape[-1])
    kvc = _compress(r3(kv_cmp), wts, layer)
    o_a = _nsa_prompt(r3(q), r3(ga), kvc, r3(kv_slc), r3(kv_win), wts, tq)
    o_b, s_new = _gla(r3(qg), r3(kb), r3(vb), r3(la), r3(rb),
                      jnp.zeros((b, GLA_HEADS, GLA_DK, GLA_DV), F32), wts, layer,
                      tt=min(512, t), chunk=math.gcd(t, GLA_CHUNK), n_valid=math.gcd(t, GLA_CHUNK))
    y, u_tail = _mix_ffn(x, o_a.reshape(b * t, A_WIDTH), o_b.reshape(b * t, B_WIDTH), wts, layer, tm, t // tm)
    kv6 = lambda a: a.reshape(b, t, 2, NSA_KV_GROUPS, HEAD_DIM)
    n_win = min(WINDOW, t)
    state = (kv6(kv_cmp), kv6(kv_slc), kv6(kv_win)[:, t - n_win:], s_new, u_tail[:, SUBLANES - (CONV_W - 1):])
    return y, state


def _sample_layer(x, wts, layer, bs, n_tok, past, cache_cmp, cache_slc, win_buf, gla_state, conv_buf, page_table):
    m = bs * SLOT
    q, kv_cmp, kv_slc, kv_win, ga, qg, kb, vb, rb, la = _proj_in(x, wts, layer, m)
    r3 = lambda a: a.reshape(bs, SLOT, a.shape[-1])
    o_a = _nsa_sample(r3(q), r3(ga), r3(kv_cmp), r3(kv_slc), r3(kv_win), win_buf, cache_cmp, cache_slc,
                      page_table, wts, layer, past, n_tok)
    o_b, s_new = _gla(r3(qg), r3(kb), r3(vb), r3(la), r3(rb), gla_state[layer], wts, layer,
                      tt=SLOT, chunk=SLOT, n_valid=n_tok)
    cb = conv_buf[layer]
    zeros = lambda n: jnp.zeros((bs, n, FFN_DIM), F32)
    cb1 = jnp.concatenate([cb[:, 1:2], zeros(SLOT - 1)], axis=1).reshape(m, FFN_DIM)
    cb2 = jnp.concatenate([cb, zeros(SLOT - 2)], axis=1).reshape(m, FFN_DIM)
    y, u = _mix_ffn(x, o_a.reshape(m, A_WIDTH), o_b.reshape(m, B_WIDTH), wts, layer, m, 1, cb=(cb1, cb2))
    kv6 = lambda a: r3(a)[:, :n_tok].reshape(bs, n_tok, 2, NSA_KV_GROUPS, HEAD_DIM)
    u_ext = jnp.concatenate([cb, u.reshape(bs, SLOT, FFN_DIM)[:, :n_tok]], axis=1)
    state = (kv6(kv_cmp), kv6(kv_slc), kv6(kv_win), s_new, u_ext[:, -(CONV_W - 1):])
    return y, state


def kernel(x_prompt, x_sample, cache_kv_cmp, cache_kv_slc, state_kv_win, state_gla, state_conv, page_table,
           w_in, w_cmp1, w_cmp2, g_q, g_k_cmp, g_k_slc, g_k_win, w_gate_up, b_gate, g_gla, w_pa, w_pb, w_out,
           g_attn, g_ffn, w_up, w_conv, b_conv, w_down):
    depth = w_in.shape[0]
    bp, t, _ = x_prompt.shape
    bs, n_tok, _ = x_sample.shape
    page = cache_kv_cmp.shape[2]
    past = page_table.shape[1] * page
    assert n_tok <= SLOT and t % CMP_STRIDE == 0 and past % SEL_BLOCK == 0

    wts = _pack_weights(w_in, w_cmp1, w_cmp2, g_q, g_k_cmp, g_k_slc, g_k_win, w_gate_up, b_gate, g_gla, w_pa,
                        w_pb, w_out, g_attn, g_ffn, w_up, w_conv, b_conv, w_down)
    n_cmp_p = t // CMP_STRIDE
    fold = _sel_fold_matrix(n_cmp_p - 1, n_cmp_p, LANES - LANE_HOT).T
    wts["fold_prompt"] = jnp.pad(fold, ((LANE_HOT, 0), (0, 0)))
    gate_expand = np.zeros((3, LANES, A_WIDTH), np.float32)
    for br in range(3):
        for h in range(NSA_HEADS):
            gate_expand[br, br * NSA_HEADS + h, h * HEAD_DIM:(h + 1) * HEAD_DIM] = 1.0
    wts["gate_expand"] = jnp.asarray(gate_expand, BF16)
    n_cmp_s = (past + CMP_STRIDE) // CMP_STRIDE - 1
    sel_lanes = -(-(past // SEL_BLOCK + 1) // LANES) * LANES
    tk_s = min(1024, past)
    wts["m_sel_sample"] = _sel_fold_matrix(n_cmp_s, n_cmp_s, sel_lanes)
    blk_per_chunk = tk_s // SEL_BLOCK
    wts["e_sel_sample"] = jnp.stack([_expand_matrix(LANES, LANES * SEL_BLOCK)[:, j * tk_s:(j + 1) * tk_s]
                                     for j in range(LANES // blk_per_chunk)])

    cache_cmp = cache_kv_cmp.transpose(0, 1, 3, 4, 5, 2).reshape(depth, -1, KV_WIDTH, page)
    regroup = np.zeros((page, page), np.float32)
    hpp = page // CMP_STRIDE
    for l in range(CMP_STRIDE):
        for h in range(hpp):
            regroup[l * hpp + h, h * CMP_STRIDE + l] = 1.0
    wts["regroup"] = jnp.asarray(regroup, BF16)
    cache_slc = cache_kv_slc.transpose(0, 1, 3, 4, 5, 2)
    win_buf = state_kv_win.transpose(0, 1, 3, 4, 5, 2)

    y_p = x_prompt.reshape(bp * t, D_MODEL)
    y_s = jnp.pad(x_sample, ((0, 0), (0, SLOT - n_tok), (0, 0))).reshape(bs * SLOT, D_MODEL)
    st_p, st_s = [], []
    for l in range(depth):
        y_p, sp = _prompt_layer(y_p, wts, l, bp, t)
        y_s, ss = _sample_layer(y_s, wts, l, bs, n_tok, past, cache_cmp, cache_slc, win_buf, state_gla, state_conv,
                                page_table)
        st_p.append(sp)
        st_s.append(ss)
    stack = lambda sts, k: jnp.stack([s[k] for s in sts])
    sample_states = [stack(st_s, k) for k in range(5)]
    sample_states[2] = jnp.concatenate([state_kv_win[:, :, n_tok:], sample_states[2]], axis=2)
    return ((y_p.reshape(bp, t, D_MODEL), y_s.reshape(bs, SLOT, D_MODEL)[:, :n_tok])
            + tuple(stack(st_p, k) for k in range(5)) + tuple(sample_states))
```

```python
import functools
import math

import numpy as np
import jax
import jax.numpy as jnp
from jax import lax
from jax.experimental import pallas as pl
from jax.experimental.pallas import tpu as pltpu

F32 = jnp.float32
BF16 = jnp.bfloat16

D_MODEL = 1024
HEAD_DIM = 64
NSA_HEADS = 8
NSA_KV_GROUPS = 2
NSA_GROUP_SIZE = NSA_HEADS // NSA_KV_GROUPS
CMP_STRIDE = 16
CMP_BLOCK = 2 * CMP_STRIDE
SEL_BLOCK = 64
SEL_TOPK = 16
WINDOW = 512
GLA_HEADS = 4
GLA_DK = 64
GLA_DV = 128
GLA_GATE_RANK = 16
GLA_GATE_NORM = 16.0
GLA_CHUNK = 64
FFN_DIM = 2816
CONV_W = 3
EPS = 1e-6
NEG_INF = -1e30
FORCE_SCORE = 1e9
BELOW_ALL_SCORES = -3e38

A_WIDTH = NSA_HEADS * HEAD_DIM
B_WIDTH = GLA_HEADS * GLA_DV
KV_WIDTH = 2 * NSA_KV_GROUPS * HEAD_DIM
GK_WIDTH = GLA_HEADS * GLA_DK
LANES = 128
SUBLANES = 8
SLOT = SUBLANES
ALIBI = tuple(float(2.0 ** (-8.0 * (h + 1) / NSA_HEADS)) for h in range(NSA_HEADS))

C_Q, C_CMP, C_SLC, C_WIN, C_GATE = 0, 512, 768, 1024, 1280
C_QB, C_KB, C_VB, C_RB, C_LR, C_END = 1408, 1664, 1920, 2432, 2944, 3072
NORM_W = A_WIDTH + 2 * LANES

VMEM_LIMIT = 56 * 1024 * 1024


def _dot(a, b):
    return jnp.dot(a.astype(BF16), b.astype(BF16), preferred_element_type=F32)


def _dot_nt(a, b):
    return lax.dot_general(a.astype(BF16), b.astype(BF16), (((1,), (1,)), ((), ())),
                           preferred_element_type=F32)


def _dot_tn(a, b):
    return lax.dot_general(a.astype(BF16), b.astype(BF16), (((0,), (0,)), ((), ())),
                           preferred_element_type=F32)


def _split_bf16(x, parts):
    out = []
    for _ in range(parts):
        p = x.astype(BF16)
        out.append(p)
        x = x - p.astype(F32)
    return out


def _dot_wide(x, m, parts, dot=None):
    dot = dot or (lambda a, b: jnp.dot(a, b, preferred_element_type=F32))
    acc = None
    for p in _split_bf16(x, parts):
        t = dot(p, m)
        acc = t if acc is None else acc + t
    return acc


def _rms_rows(x, g):
    return x * lax.rsqrt(jnp.mean(x * x, axis=-1, keepdims=True) + EPS) * g


def _rms_groups(x, ones_blockdiag, g):
    sq = x * x
    ms = jnp.concatenate([_dot_wide(sq[:, j:j + LANES], ones_blockdiag, 2)
                          for j in range(0, x.shape[1], LANES)], axis=1) * (1.0 / HEAD_DIM)
    return x * lax.rsqrt(ms + EPS) * g


def _sigmoid(x):
    return 1.0 / (1.0 + jnp.exp(-x))


def _gelu_tanh(x):
    return 0.5 * x * (1.0 + jnp.tanh(math.sqrt(2.0 / math.pi) * (x + 0.044715 * (x * x * x))))


def _log_sigmoid(x):
    return jnp.minimum(x, 0.0) - jnp.log1p(jnp.exp(-jnp.abs(x)))


def _topk_mask(score, n_cand, k):
    lane = lax.broadcasted_iota(jnp.int32, score.shape, 1)
    ahead = jnp.zeros(score.shape, F32)
    for j in range(n_cand):
        col = score[:, j:j + 1]
        ahead = ahead + jnp.where(col > score, 1.0, jnp.where(col == score, jnp.where(lane > j, 1.0, 0.0), 0.0))
    return jnp.where(lane < n_cand, jnp.where(ahead < k, 1.0, 0.0), 0.0)


def _masked_softmax(s, mask):
    s = jnp.where(mask, s, NEG_INF)
    m = jnp.max(s, axis=-1, keepdims=True)
    e = jnp.where(mask, jnp.exp(s - m), 0.0)
    l = jnp.sum(e, axis=-1, keepdims=True)
    return e * jnp.where(l > 0.0, 1.0 / l, 0.0)


def _stack_heads(q, g):
    base = g * NSA_GROUP_SIZE
    return jnp.concatenate([q[:, (base + r) * HEAD_DIM:(base + r + 1) * HEAD_DIM]
                            for r in range(NSA_GROUP_SIZE)], axis=0)


def _head_slopes(g, n):
    row = lax.broadcasted_iota(jnp.int32, (NSA_GROUP_SIZE * n, 1), 0)
    slope = jnp.full((NSA_GROUP_SIZE * n, 1), ALIBI[g * NSA_GROUP_SIZE + NSA_GROUP_SIZE - 1], F32)
    for r in reversed(range(NSA_GROUP_SIZE - 1)):
        slope = jnp.where(row < (r + 1) * n, ALIBI[g * NSA_GROUP_SIZE + r], slope)
    return slope


def _gate_cols(ga, branch, g):
    c0 = branch * NSA_HEADS + g * NSA_GROUP_SIZE
    return jnp.concatenate([ga[:, c0 + r:c0 + r + 1] for r in range(NSA_GROUP_SIZE)], axis=0)


def _block_scores(blk_imp, q_pos, lane_i, n_sel):
    forced = (lane_i == 0) | (lane_i == lax.shift_right_logical(q_pos, int(math.log2(SEL_BLOCK))))
    valid = lane_i * SEL_BLOCK <= q_pos
    score = jnp.where(forced, FORCE_SCORE, jnp.where(valid, blk_imp, NEG_INF))
    return jnp.where(lane_i < n_sel, score, BELOW_ALL_SCORES)


def _compress_rows(load_rows, w1_ref, w2_ref, ones128_ref, gk_ref):
    out = []
    for c in range(2):
        acc = None
        for l in range(CMP_BLOCK):
            t = jnp.dot(load_rows(c, l).astype(BF16), w1_ref[l, c], preferred_element_type=F32)
            acc = t if acc is None else acc + t
        out.append(jnp.dot(_gelu_tanh(acc).astype(BF16), w2_ref[c], preferred_element_type=F32))
    return _rms_groups(out[0], ones128_ref[...], gk_ref[...]), out[1]


def _proj_in_kernel(x_ref, gattn_ref, w_ref, gnorm_ref, ones_ref, wgu_ref, bg_ref,
                    q_ref, cmp_ref, slc_ref, win_ref, ga_ref, qg_ref, kb_ref, vb_ref, rb_ref, la_ref):
    h = _rms_rows(x_ref[...], gattn_ref[...])
    p = jnp.dot(h.astype(BF16), w_ref[...], preferred_element_type=F32)
    to_norm = jnp.concatenate([p[:, C_Q:C_CMP], p[:, C_SLC:C_SLC + LANES], p[:, C_WIN:C_WIN + LANES]], axis=1)
    y = _rms_groups(to_norm, ones_ref[...], gnorm_ref[...])
    q_ref[...] = y[:, :A_WIDTH] * HEAD_DIM ** -0.5
    cmp_ref[...] = p[:, C_CMP:C_SLC]
    slc_ref[...] = jnp.concatenate([y[:, A_WIDTH:A_WIDTH + LANES], p[:, C_SLC + LANES:C_WIN]], axis=1)
    win_ref[...] = jnp.concatenate([y[:, A_WIDTH + LANES:], p[:, C_WIN + LANES:C_GATE]], axis=1)
    ga_ref[...] = _sigmoid(p[:, C_GATE:C_QB])
    qg_ref[...] = p[:, C_QB:C_KB] * GLA_DK ** -0.5
    kb_ref[...] = p[:, C_KB:C_VB]
    vb_ref[...] = p[:, C_VB:C_RB]
    rb_ref[...] = p[:, C_RB:C_LR]
    z = jnp.dot(p[:, C_LR:C_END].astype(BF16), wgu_ref[...], preferred_element_type=F32) + bg_ref[...]
    la_ref[...] = _log_sigmoid(z) * (1.0 / GLA_GATE_NORM)


def _const_spec(shape, layer=None):
    nd = len(shape)
    if layer is None:
        return pl.BlockSpec(shape, lambda *_: (0,) * nd)
    return pl.BlockSpec((None,) + shape, lambda *_: (layer,) + (0,) * nd)


def _proj_in(x, wts, layer, tm):
    m = x.shape[0]
    widths = (A_WIDTH, KV_WIDTH, KV_WIDTH, KV_WIDTH, LANES, GK_WIDTH, GK_WIDTH, B_WIDTH, B_WIDTH, GK_WIDTH)
    row = lambda w: pl.BlockSpec((tm, w), lambda i: (i, 0))
    return pl.pallas_call(
        _proj_in_kernel,
        grid=(m // tm,),
        in_specs=[row(D_MODEL),
                  _const_spec((1, D_MODEL), layer), _const_spec((D_MODEL, C_END), layer),
                  _const_spec((1, NORM_W), layer), _const_spec((LANES, LANES)),
                  _const_spec((LANES, GK_WIDTH), layer), _const_spec((1, GK_WIDTH), layer)],
        out_specs=[row(w) for w in widths],
        out_shape=[jax.ShapeDtypeStruct((m, w), F32) for w in widths],
        compiler_params=pltpu.CompilerParams(dimension_semantics=("arbitrary",), vmem_limit_bytes=VMEM_LIMIT),
        name="proj_in",
    )(x, wts["g_attn"], wts["w_a"], wts["g_norm"], wts["ones128"], wts["w_gate_up"], wts["b_gate"])


def _compress_kernel(kv_ref, w1_ref, w2_ref, ones_ref, gk_ref, out_ref, pad_ref, *, t):
    for c in range(2):
        pad_ref[c, 0:t, :] = kv_ref[0, :, c * LANES:(c + 1) * LANES]
        pad_ref[c, t:t + CMP_STRIDE, :] = jnp.zeros((CMP_STRIDE, LANES), F32)
    n_blk = t // CMP_STRIDE
    kc, vc = _compress_rows(lambda c, l: pad_ref[c, pl.ds(l, n_blk, stride=CMP_STRIDE), :],
                            w1_ref, w2_ref, ones_ref, gk_ref)
    out_ref[0] = jnp.concatenate([kc, vc], axis=1)


def _compress(kv_cmp, wts, layer):
    b, t, _ = kv_cmp.shape
    n_blk = t // CMP_STRIDE
    return pl.pallas_call(
        functools.partial(_compress_kernel, t=t),
        grid=(b,),
        in_specs=[pl.BlockSpec((1, t, KV_WIDTH), lambda i: (i, 0, 0)),
                  _const_spec((CMP_BLOCK, 2, LANES, LANES), layer), _const_spec((2, LANES, LANES), layer),
                  _const_spec((LANES, LANES)), _const_spec((1, LANES), layer)],
        out_specs=pl.BlockSpec((1, n_blk, KV_WIDTH), lambda i: (i, 0, 0)),
        out_shape=jax.ShapeDtypeStruct((b, n_blk, KV_WIDTH), F32),
        scratch_shapes=[pltpu.VMEM((2, t + CMP_STRIDE, LANES), F32)],
        compiler_params=pltpu.CompilerParams(dimension_semantics=("arbitrary",), vmem_limit_bytes=VMEM_LIMIT),
        name="compress",
    )(kv_cmp, wts["w_c1"], wts["w_c2"], wts["ones128"], wts["g_k_cmp"])


MASK_BIG = 2.0 ** 100
LANE_HI, LANE_LO, LANE_ONE, LANE_HOT = HEAD_DIM, HEAD_DIM + 1, HEAD_DIM + 2, HEAD_DIM + 3
SEL_SHIFT = int(math.log2(SEL_BLOCK))


def _group_halves(x, g):
    return x if g == 0 else pltpu.roll(x, HEAD_DIM, axis=1)


def _build_prompt_operands(kvc_ref, slc_ref, win_ref, kc_ref, vc_ref, ks_ref, vs_ref, kw_ref, vw_ref):
    t = slc_ref.shape[1]
    lane = lax.broadcasted_iota(jnp.int32, (t, LANES), 1)
    pos = lax.broadcasted_iota(jnp.int32, (t, LANES), 0)
    blk = lax.shift_right_logical(pos, SEL_SHIFT)
    pos_lanes = jnp.where(lane == LANE_HI, blk, jnp.where(lane == LANE_LO, pos & (SEL_BLOCK - 1),
                                                          jnp.where(lane == LANE_ONE, 1, 0)))
    aux_win = pos_lanes.astype(F32)
    aux_sel = jnp.where(lane - LANE_HOT == blk, 1, pos_lanes).astype(F32)
    n_cmp = kvc_ref.shape[1]
    lane_c = lax.broadcasted_iota(jnp.int32, (n_cmp, LANES), 1)
    blk_c = lax.broadcasted_iota(jnp.int32, (n_cmp, LANES), 0)
    aux_cmp = jnp.where(lane_c == LANE_HI, blk_c, jnp.where((lane_c == LANE_LO) | (lane_c == LANE_ONE), 1, 0)).astype(F32)
    lane_p = lax.broadcasted_iota(jnp.int32, (WINDOW, LANES), 1)
    before_seq = jnp.where(lane_p == LANE_HOT, -MASK_BIG, 0.0).astype(BF16)
    for g in range(NSA_KV_GROUPS):
        ks_ref[g] = jnp.where(lane < HEAD_DIM, _group_halves(slc_ref[0, :, 0:LANES], g), aux_sel).astype(BF16)
        vs_ref[g] = jnp.where(lane < HEAD_DIM, _group_halves(slc_ref[0, :, LANES:], g), 1.0).astype(BF16)
        kw_ref[g, 0:WINDOW, :] = before_seq
        vw_ref[g, 0:WINDOW, :] = jnp.zeros((WINDOW, LANES), BF16)
        kw_ref[g, WINDOW:, :] = jnp.where(lane < HEAD_DIM, _group_halves(win_ref[0, :, 0:LANES], g), aux_win).astype(BF16)
        vw_ref[g, WINDOW:, :] = jnp.where(lane < HEAD_DIM, _group_halves(win_ref[0, :, LANES:], g), 1.0).astype(BF16)
        kc_ref[g] = jnp.where(lane_c < HEAD_DIM, _group_halves(kvc_ref[0, :, 0:LANES], g), aux_cmp).astype(BF16)
        vc_ref[g] = jnp.where(lane_c < HEAD_DIM, _group_halves(kvc_ref[0, :, LANES:], g), 0.0).astype(BF16)


def _nsa_prompt_kernel(q_ref, ga_ref, kvc_ref, slc_ref, win_ref, fold_ref, gexp_ref, o_ref,
                       kc_ref, vc_ref, ks_ref, vs_ref, kw_ref, vw_ref, m_scr, acc_scr, *, tq, kw_sel):
    i = pl.program_id(1)

    @pl.when(i == 0)
    def _():
        _build_prompt_operands(kvc_ref, slc_ref, win_ref, kc_ref, vc_ref, ks_ref, vs_ref, kw_ref, vw_ref)

    q0 = i * tq
    q0f = q0.astype(F32)
    q = q_ref[0]
    rows = NSA_GROUP_SIZE * tq
    n_cmp = kvc_ref.shape[1]
    n_sel = slc_ref.shape[1] // SEL_BLOCK
    k_top = min(SEL_TOPK, n_sel)
    lane = lax.broadcasted_iota(jnp.int32, (tq, LANES), 1)
    lane4 = lax.broadcasted_iota(jnp.int32, (rows, LANES), 1)
    t_w = lax.broadcasted_iota(jnp.int32, (rows, tq), 0) & (tq - 1)
    lane_w = lax.broadcasted_iota(jnp.int32, (rows, tq), 1)
    causal_bias = jnp.where(lane_w <= t_w, 0.0, -MASK_BIG)
    far_bias = jnp.where(lane_w > t_w, 0.0, -MASK_BIG)
    cmp_valid = (lax.broadcasted_iota(jnp.int32, (rows, n_cmp), 1) * CMP_STRIDE + (CMP_BLOCK - 1)
                 <= q0 + (lax.broadcasted_iota(jnp.int32, (rows, n_cmp), 0) & (tq - 1)))
    hot_rows = slice(HEAD_DIM, HEAD_DIM + 8 * (-(-(LANE_HOT - HEAD_DIM + n_sel) // 8)))
    n_hot = hot_rows.stop - hot_rows.start
    blk_t = lax.broadcasted_iota(jnp.int32, (n_hot, tq), 0) - (LANE_HOT - HEAD_DIM)
    qpos_t = q0 + lax.broadcasted_iota(jnp.int32, (n_hot, tq), 1)
    is_blk = (blk_t >= 0) & (blk_t < n_sel)

    def lane_tiles_max(s):
        m = s[:, 0:LANES]
        for j in range(1, s.shape[1] // LANES):
            m = jnp.maximum(m, s[:, j * LANES:(j + 1) * LANES])
        return m

    def normalised(acc):
        return acc * (1.0 / pltpu.roll(acc, HEAD_DIM, axis=1))

    def attend_selected(qa, g):
        n_full = q0 // kw_sel
        k_last = pl.multiple_of(n_full * kw_sel, kw_sel)
        visible = lax.broadcasted_iota(jnp.int32, (rows, kw_sel), 1) <= (
            (q0 - k_last) + (lax.broadcasted_iota(jnp.int32, (rows, kw_sel), 0) & (tq - 1)))
        s = _dot_nt(qa, ks_ref[g, pl.ds(k_last, kw_sel), :]) + jnp.where(visible, 0.0, -MASK_BIG)
        m = jnp.max(lane_tiles_max(s), axis=-1, keepdims=True)
        m_scr[...] = jnp.broadcast_to(m, (rows, LANES))
        acc_scr[...] = _dot(jnp.exp(s - m), vs_ref[g, pl.ds(k_last, kw_sel), :])

        def body(c, _):
            rows_c = pl.ds(pl.multiple_of(c * kw_sel, kw_sel), kw_sel)
            s = _dot_nt(qa, ks_ref[g, rows_c, :])
            m_old = m_scr[...]
            m_new = jnp.maximum(m_old, jnp.max(lane_tiles_max(s), axis=-1, keepdims=True))
            p = jnp.exp(s - jnp.concatenate([m_new] * (kw_sel // LANES), axis=1))
            acc_scr[...] = jnp.exp(m_old - m_new) * acc_scr[...] + _dot(p, vs_ref[g, rows_c, :])
            m_scr[...] = m_new
            return 0

        lax.fori_loop(0, n_full, body, 0)
        return normalised(acc_scr[...])

    def attend_window(qa, g):
        k0 = pl.multiple_of(q0, tq)
        s = _dot_nt(qa, kw_ref[g, pl.ds(k0, WINDOW + tq), :])
        s = jnp.concatenate([s[:, 0:tq] + far_bias, s[:, tq:WINDOW], s[:, WINDOW:] + causal_bias], axis=1)
        m = jnp.max(lane_tiles_max(s), axis=-1, keepdims=True)
        return normalised(_dot(jnp.exp(s - m), vw_ref[g, pl.ds(k0, WINDOW + tq), :]))

    branch_out = [[], [], []]
    for g in range(NSA_KV_GROUPS):
        q_rows, base_rows, cbase_rows = [], [], []
        for r in range(NSA_GROUP_SIZE):
            h = g * NSA_GROUP_SIZE + r
            tile = q[:, (h // 2) * LANES:(h // 2 + 1) * LANES]
            q_rows.append(tile if h % 2 == 0 else pltpu.roll(tile, HEAD_DIM, axis=1))
            sl = ALIBI[h]
            base_rows.append(jnp.where(lane == LANE_HI, SEL_BLOCK * sl, jnp.where(
                lane == LANE_LO, sl, jnp.where(lane == LANE_ONE, -sl * q0f, 0.0))))
            cbase_rows.append(jnp.where(lane == LANE_HI, CMP_STRIDE * sl, jnp.where(
                lane == LANE_LO, (CMP_BLOCK - 1) * sl, jnp.where(lane == LANE_ONE, -sl * q0f, 0.0))))
        q4 = jnp.concatenate(q_rows, axis=0)
        base4 = jnp.concatenate(base_rows, axis=0)
        head_lanes = lane4 < HEAD_DIM

        qa = jnp.where(head_lanes, q4, jnp.concatenate(cbase_rows, axis=0)).astype(BF16)
        s = jnp.where(cmp_valid, _dot_nt(qa, kc_ref[g]), -MASK_BIG)
        m = jnp.max(s, axis=-1, keepdims=True)
        e = jnp.where(cmp_valid, jnp.exp(s - m), 0.0)
        l = jnp.sum(e, axis=-1, keepdims=True)
        p = e * jnp.where(l > 0.0, 1.0 / l, 0.0)
        branch_out[0].append(_dot(p, vc_ref[g]))
        imp = p[0:tq]
        for r in range(1, NSA_GROUP_SIZE):
            imp = imp + p[r * tq:(r + 1) * tq]

        blk_imp = _dot_wide(imp, fold_ref[...], 3, dot=lambda a, mt: _dot_nt(mt, a))[hot_rows]
        forced = (blk_t == 0) | (blk_t == lax.shift_right_logical(qpos_t, SEL_SHIFT))
        score = jnp.where(forced, FORCE_SCORE, jnp.where(blk_t * SEL_BLOCK <= qpos_t, blk_imp, NEG_INF))
        score = jnp.where(is_blk, score, BELOW_ALL_SCORES)
        ahead = jnp.zeros((n_hot, tq), F32)
        for j in range(n_sel):
            row_j = score[LANE_HOT - HEAD_DIM + j:LANE_HOT - HEAD_DIM + j + 1, :]
            ahead = ahead + jnp.where(row_j > score, 1.0, jnp.where(row_j == score, jnp.where(blk_t > j, 1.0, 0.0), 0.0))
        sel_bias_t = jnp.where(is_blk & (ahead >= k_top), -MASK_BIG, 0.0)
        sel_bias = jnp.transpose(jnp.concatenate(
            [jnp.zeros((hot_rows.start, tq), F32), sel_bias_t, jnp.zeros((LANES - hot_rows.stop, tq), F32)], axis=0))
        sel_bias4 = jnp.concatenate([sel_bias] * NSA_GROUP_SIZE, axis=0)

        qa = jnp.where(head_lanes, q4, base4 + sel_bias4).astype(BF16)
        branch_out[1].append(attend_selected(qa, g))
        qa = jnp.where(head_lanes, q4, jnp.where(lane4 == LANE_HOT, 1.0, base4)).astype(BF16)
        branch_out[2].append(attend_window(qa, g))

    ga = ga_ref[0]
    out = None
    for br in range(3):
        tiles = []
        for g in range(NSA_KV_GROUPS):
            o4 = branch_out[br][g]
            for r in range(0, NSA_GROUP_SIZE, 2):
                even, odd = o4[r * tq:(r + 1) * tq], o4[(r + 1) * tq:(r + 2) * tq]
                tiles.append(jnp.where(lane < HEAD_DIM, even, pltpu.roll(odd, HEAD_DIM, axis=1)))
        term = _dot_wide(ga, gexp_ref[br], 3) * jnp.concatenate(tiles, axis=1)
        out = term if out is None else out + term
    o_ref[0] = out


def _nsa_prompt(q, ga, kvc, kv_slc, kv_win, wts, tq):
    b, t, _ = q.shape
    n_cmp = kvc.shape[1]
    tile = lambda w: pl.BlockSpec((1, tq, w), lambda bi, i: (bi, i, 0))
    full = lambda n, w: pl.BlockSpec((1, n, w), lambda bi, i: (bi, 0, 0))
    kw_sel = min(max(512, tq), t)
    assert tq % LANES == 0 and WINDOW % tq == 0 and tq < WINDOW and kw_sel % tq == 0 and t % kw_sel == 0
    assert LANE_HOT + t // SEL_BLOCK <= LANES
    aug = lambda n: pltpu.VMEM((NSA_KV_GROUPS, n, LANES), BF16)
    stat = pltpu.VMEM((NSA_GROUP_SIZE * tq, LANES), F32)
    return pl.pallas_call(
        functools.partial(_nsa_prompt_kernel, tq=tq, kw_sel=kw_sel),
        grid=(b, t // tq),
        in_specs=[tile(A_WIDTH), tile(LANES), full(n_cmp, KV_WIDTH), full(t, KV_WIDTH), full(t, KV_WIDTH),
                  _const_spec((LANES, n_cmp)), _const_spec((3, LANES, A_WIDTH))],
        out_specs=tile(A_WIDTH),
        out_shape=jax.ShapeDtypeStruct((b, t, A_WIDTH), F32),
        scratch_shapes=[aug(n_cmp), aug(n_cmp), aug(t), aug(t), aug(t + WINDOW), aug(t + WINDOW), stat, stat],
        compiler_params=pltpu.CompilerParams(dimension_semantics=("arbitrary", "arbitrary"),
                                             vmem_limit_bytes=VMEM_LIMIT),
        name="nsa_prompt",
    )(q, ga, kvc, kv_slc, kv_win, wts["fold_prompt"], wts["gate_expand"])


def _nsa_sample_kernel(pt_ref, q_ref, ga_ref, ncmp_ref, nslc_ref, nwin_ref, wbuf_ref, cmp_hbm, slc_hbm,
                       w1_ref, w2_ref, ones_ref, gk_ref, msel_ref, esel_ref, regroup_ref,
                       o_ref, xcn_ref, xc_ref, xs_ref, half_scr, sem_ref, *, layer, past, n_tok, tk):
    b = pl.program_id(0)
    nb = pl.num_programs(0)
    n_pages = pt_ref.shape[1]
    page = past // n_pages
    hpp = page // CMP_STRIDE
    n_cmp = past // CMP_STRIDE
    slot = b % 2

    def page_copies(bb, sl):
        out = []
        for p in range(n_pages):
            pg = pt_ref[bb, p]
            out.append(pltpu.make_async_copy(cmp_hbm.at[layer, pg], xcn_ref.at[sl, :, pl.ds(p * page, page)],
                                             sem_ref.at[0, sl]))
            out.append(pltpu.make_async_copy(slc_hbm.at[layer, pg], xs_ref.at[sl, :, :, :, pl.ds(p * page, page)],
                                             sem_ref.at[1, sl]))
        return out

    def rows_on_lanes(new_ref):
        new = jnp.where(is_tok, new_ref[0], 0.0)
        return jnp.transpose(jnp.concatenate([new, jnp.zeros((LANES - SLOT, KV_WIDTH), F32)], axis=0))

    @pl.when(b == 0)
    def _():
        for cp in page_copies(0, 0):
            cp.start()

    @pl.when(b + 1 < nb)
    def _():
        for cp in page_copies(b + 1, 1 - slot):
            cp.start()

    is_tok = lax.broadcasted_iota(jnp.int32, (SLOT, 1), 0) < n_tok
    xc_ref[:, n_cmp:n_cmp + SUBLANES, :] = jnp.zeros((CMP_STRIDE, SUBLANES, KV_WIDTH), F32)
    for l in range(n_tok):
        xc_ref[l, n_cmp:n_cmp + 1, :] = ncmp_ref[0, l:l + 1, :]
    new_slc = rows_on_lanes(nslc_ref)
    new_win = rows_on_lanes(nwin_ref)
    for c in range(2):
        for g in range(NSA_KV_GROUPS):
            r0 = (c * NSA_KV_GROUPS + g) * HEAD_DIM
            xs_ref[slot, c, g, :, past:past + LANES] = new_slc[r0:r0 + HEAD_DIM, :]
    n_buf = wbuf_ref.shape[-1]

    for cp in page_copies(b, slot):
        cp.wait()

    q = q_ref[0]
    ga = ga_ref[0]
    rows = NSA_GROUP_SIZE * SLOT
    n_sel = past // SEL_BLOCK + 1
    q_pos = past + lax.broadcasted_iota(jnp.int32, (SLOT, 1), 0)
    q_pos4 = jnp.concatenate([q_pos] * NSA_GROUP_SIZE, axis=0)
    sel_lanes = msel_ref.shape[1]
    lane_i = lax.broadcasted_iota(jnp.int32, (1, sel_lanes), 1)
    ends = lax.broadcasted_iota(jnp.int32, (1, n_cmp), 1) * CMP_STRIDE + (CMP_BLOCK - 1)

    for p in range(n_pages):
        regrouped = _dot_nt(regroup_ref[...], xcn_ref[slot, :, p * page:(p + 1) * page])
        for l in range(CMP_STRIDE):
            xc_ref[l, p * hpp:(p + 1) * hpp, :] = regrouped[l * hpp:(l + 1) * hpp, :]

    ab = None
    for l in range(CMP_STRIDE):
        t = jnp.dot(xc_ref[l].astype(BF16), w1_ref[l], preferred_element_type=F32)
        ab = t if ab is None else ab + t
    half_scr[...] = ab[:, KV_WIDTH:]
    hid = ab[0:n_cmp, :KV_WIDTH] + half_scr[1:n_cmp + 1, :]
    kvc = jnp.dot(_gelu_tanh(hid).astype(BF16), w2_ref[...], preferred_element_type=F32)
    kc = _rms_groups(kvc[:, :LANES], ones_ref[...], gk_ref[...])
    vc = kvc[:, LANES:]
    pieces = []
    for g in range(NSA_KV_GROUPS):
        slope = _head_slopes(g, SLOT)
        qg = _stack_heads(q, g).astype(BF16)
        kcol = slice(g * HEAD_DIM, (g + 1) * HEAD_DIM)
        vcol = slice(LANES + g * HEAD_DIM, LANES + (g + 1) * HEAD_DIM)

        dist = q_pos4 - ends
        s = _dot_nt(qg, kc[:, kcol]) - slope * dist.astype(F32)
        p = _masked_softmax(s, dist >= 0)
        o_cmp = _dot(p, vc[:, kcol])
        imp = p[0:SLOT]
        for r in range(1, NSA_GROUP_SIZE):
            imp = imp + p[r * SLOT:(r + 1) * SLOT]

        blk_imp = _dot_wide(imp, msel_ref[...], 3)
        sel = _topk_mask(_block_scores(blk_imp, q_pos, lane_i, n_sel), n_sel, min(SEL_TOPK, n_sel))
        sel4 = jnp.concatenate([sel] * NSA_GROUP_SIZE, axis=0).astype(BF16)

        chunks_per_slab = esel_ref.shape[0]
        bounds = [(c * tk, tk if c * tk < past else LANES) for c in range(past // tk + 1)]
        scores = []
        for c, (k0, n_k) in enumerate(bounds):
            dist = q_pos4 - (k0 + lax.broadcasted_iota(jnp.int32, (1, n_k), 1))
            s = _dot(qg, xs_ref[slot, 0, g, :, k0:k0 + n_k]) - slope * dist.astype(F32)
            if k0 < past:
                slab = c // chunks_per_slab
                in_sel = _dot(sel4[:, slab * LANES:(slab + 1) * LANES], esel_ref[c % chunks_per_slab])
            else:
                in_sel = sel4[:, n_sel - 1:n_sel].astype(F32)
            scores.append(jnp.where((in_sel > 0.5) & (dist >= 0), s, NEG_INF))
        m = functools.reduce(jnp.maximum, [jnp.max(s, axis=-1, keepdims=True) for s in scores])
        l = jnp.zeros((rows, 1), F32)
        o_slc = jnp.zeros((rows, HEAD_DIM), F32)
        for s, (k0, n_k) in zip(scores, bounds):
            p = jnp.exp(s - m)
            l = l + jnp.sum(p, axis=-1, keepdims=True)
            o_slc = o_slc + _dot_nt(p, xs_ref[slot, 1, g, :, k0:k0 + n_k])
        o_slc = o_slc * (1.0 / l)

        k_pos = (past - n_buf) + lax.broadcasted_iota(jnp.int32, (1, n_buf + LANES), 1)
        dist = q_pos4 - k_pos
        s = jnp.concatenate([_dot(qg, wbuf_ref[0, 0, g]), _dot(qg, new_win[kcol, :])], axis=1) - slope * dist.astype(F32)
        p = _masked_softmax(s, (dist >= 0) & (dist < WINDOW))
        o_win = _dot_nt(p[:, :n_buf], wbuf_ref[0, 1, g]) + _dot_nt(p[:, n_buf:], new_win[vcol, :])

        og = _gate_cols(ga, 0, g) * o_cmp + _gate_cols(ga, 1, g) * o_slc + _gate_cols(ga, 2, g) * o_win
        pieces += [og[r * SLOT:(r + 1) * SLOT] for r in range(NSA_GROUP_SIZE)]
    o_ref[0] = jnp.concatenate(pieces, axis=1)


def _nsa_sample(q, ga, ncmp, nslc, nwin, win_buf, cache_cmp, cache_slc, page_table, wts, layer, past, n_tok):
    bs = q.shape[0]
    n_buf = win_buf.shape[-1]
    n_cmp = past // CMP_STRIDE
    tk = wts["e_sel_sample"].shape[2]
    slot_spec = lambda w: pl.BlockSpec((1, SLOT, w), lambda b, pt: (b, 0, 0))
    const = lambda shape, lyr=None: (
        pl.BlockSpec(shape, lambda b, pt: (0,) * len(shape), pipeline_mode=pl.Buffered(1)) if lyr is None else
        pl.BlockSpec((None,) + shape, lambda b, pt: (lyr,) + (0,) * len(shape), pipeline_mode=pl.Buffered(1)))
    grid_spec = pltpu.PrefetchScalarGridSpec(
        num_scalar_prefetch=1,
        grid=(bs,),
        in_specs=[slot_spec(A_WIDTH), slot_spec(LANES), slot_spec(KV_WIDTH), slot_spec(KV_WIDTH), slot_spec(KV_WIDTH),
                  pl.BlockSpec((None, 1, 2, NSA_KV_GROUPS, HEAD_DIM, n_buf), lambda b, pt: (layer, b, 0, 0, 0, 0)),
                  pl.BlockSpec(memory_space=pl.ANY), pl.BlockSpec(memory_space=pl.ANY),
                  const((CMP_STRIDE, KV_WIDTH, 2 * KV_WIDTH), layer), const((KV_WIDTH, KV_WIDTH), layer),
                  const((LANES, LANES)), const((1, LANES), layer),
                  const(wts["m_sel_sample"].shape), const(wts["e_sel_sample"].shape), const(wts["regroup"].shape)],
        out_specs=slot_spec(A_WIDTH),
        scratch_shapes=[pltpu.VMEM((2, KV_WIDTH, past), F32),
                        pltpu.VMEM((CMP_STRIDE, n_cmp + SUBLANES, KV_WIDTH), F32),
                        pltpu.VMEM((2, 2, NSA_KV_GROUPS, HEAD_DIM, past + LANES), F32),
                        pltpu.VMEM((n_cmp + SUBLANES, KV_WIDTH), F32),
                        pltpu.SemaphoreType.DMA((2, 2))],
    )
    return pl.pallas_call(
        functools.partial(_nsa_sample_kernel, layer=layer, past=past, n_tok=n_tok, tk=tk),
        grid_spec=grid_spec,
        out_shape=jax.ShapeDtypeStruct((bs, SLOT, A_WIDTH), F32),
        compiler_params=pltpu.CompilerParams(dimension_semantics=("arbitrary",), vmem_limit_bytes=VMEM_LIMIT),
        name="nsa_sample",
    )(page_table, q, ga, ncmp, nslc, nwin, win_buf, cache_cmp, cache_slc,
      wts["w_c1_halves"], wts["w_c2_full"], wts["ones128"], wts["g_k_cmp"], wts["m_sel_sample"], wts["e_sel_sample"],
      wts["regroup"])


def _gla_kernel(qg_ref, kb_ref, vb_ref, la_ref, rb_ref, s0_ref, ggla_ref, o_ref, sout_ref, s_scr, *, chunk, n_valid):
    j = pl.program_id(1)

    @pl.when(j == 0)
    def _():
        s_scr[...] = s0_ref[0]

    tt = qg_ref.shape[1]
    n_ch = tt // chunk
    shift = int(math.log2(chunk))
    ri = lax.broadcasted_iota(jnp.int32, (tt, tt), 0)
    ci = lax.broadcasted_iota(jnp.int32, (tt, tt), 1)
    same_chunk = lax.shift_right_logical(ri, shift) == lax.shift_right_logical(ci, shift)
    causal = same_chunk & (ri >= ci)
    chunk_of_row = lax.shift_right_logical(lax.broadcasted_iota(jnp.int32, (tt, LANES), 0), shift)
    chunk_lane = lax.broadcasted_iota(jnp.int32, (tt, LANES), 1)
    as_b = lambda mask: jnp.where(mask, 1.0, 0.0).astype(BF16)
    left = lambda a, m: jnp.dot(m, a, preferred_element_type=F32)
    g_gla = ggla_ref[...]

    q, k, v, la = qg_ref[0], kb_ref[0], vb_ref[0], la_ref[0]
    if n_valid < chunk:
        ok = (lax.broadcasted_iota(jnp.int32, (tt, 1), 0) & (chunk - 1)) < n_valid
        q, k, v, la = (jnp.where(ok, a, 0.0) for a in (q, k, v, la))
    bcum = _dot_wide(la, as_b(causal), 3, dot=left)
    bend = jnp.concatenate([jnp.broadcast_to(bcum[(c + 1) * chunk - 1:(c + 1) * chunk, :], (chunk, GK_WIDTH))
                            for c in range(n_ch)], axis=0)
    bend_t = _dot_wide(la, as_b(chunk_of_row == chunk_lane), 3, dot=lambda a, m: lax.dot_general(
        a, m, (((0,), (0,)), ((), ())), preferred_element_type=F32))
    qe = q * jnp.exp(bcum)
    kd = k * jnp.exp(-bcum)
    kl = k * jnp.exp(bend - bcum)
    for h in range(GLA_HEADS):
        kc = slice(h * GLA_DK, (h + 1) * GLA_DK)
        vc = slice(h * GLA_DV, (h + 1) * GLA_DV)
        o_intra = _dot(jnp.where(causal, _dot_nt(qe[:, kc], kd[:, kc]), 0.0), v[:, vc])
        s_h = s_scr[h]
        o_inter = []
        for c in range(n_ch):
            rows = slice(c * chunk, (c + 1) * chunk)
            o_inter.append(_dot(qe[rows, kc], s_h))
            s_h = jnp.exp(bend_t[kc, c:c + 1]) * s_h + _dot_tn(kl[rows, kc], v[rows, vc])
        s_scr[h] = s_h
        o = o_intra + jnp.concatenate(o_inter, axis=0)
        r = rb_ref[0, :, vc]
        o_ref[0, :, vc] = _rms_rows(o, g_gla) * (r * _sigmoid(r))
    sout_ref[0] = s_scr[...]


def _gla(qg, kb, vb, la, rb, s0, wts, layer, tt, chunk, n_valid):
    b, t, _ = qg.shape
    tile = lambda w: pl.BlockSpec((1, tt, w), lambda bi, j: (bi, j, 0))
    st = pl.BlockSpec((1, GLA_HEADS, GLA_DK, GLA_DV), lambda bi, j: (bi, 0, 0, 0))
    return pl.pallas_call(
        functools.partial(_gla_kernel, chunk=chunk, n_valid=n_valid),
        grid=(b, t // tt),
        in_specs=[tile(GK_WIDTH), tile(GK_WIDTH), tile(B_WIDTH), tile(GK_WIDTH), tile(B_WIDTH), st,
                  pl.BlockSpec((None, 1, GLA_DV), lambda bi, j: (layer, 0, 0))],
        out_specs=[tile(B_WIDTH), st],
        out_shape=[jax.ShapeDtypeStruct((b, t, B_WIDTH), F32),
                   jax.ShapeDtypeStruct((b, GLA_HEADS, GLA_DK, GLA_DV), F32)],
        scratch_shapes=[pltpu.VMEM((GLA_HEADS, GLA_DK, GLA_DV), F32)],
        compiler_params=pltpu.CompilerParams(dimension_semantics=("arbitrary", "arbitrary"),
                                             vmem_limit_bytes=VMEM_LIMIT),
        name="gla",
    )(qg, kb, vb, la, rb, s0, wts["g_gla"])


def _mix_ffn_kernel(*refs, tm, f_chunk, slots):
    if slots:
        (x_ref, oa_ref, ob_ref, gattn_ref, gffn_ref, wm_ref, wpa_ref, wpb_ref, wout_ref, wup_ref, wconv_ref,
         bconv_ref, wdown_ref, cb1_ref, cb2_ref, y_ref, u_ref, u_scr) = refs
    else:
        (x_ref, oa_ref, ob_ref, gattn_ref, gffn_ref, wm_ref, wpa_ref, wpb_ref, wout_ref, wup_ref, wconv_ref,
         bconv_ref, wdown_ref, y_ref, u_ref, u_scr) = refs
    x = x_ref[...]
    h = _rms_rows(x, gattn_ref[...])
    mg = jnp.dot(h.astype(BF16), wm_ref[...], preferred_element_type=F32)
    mix = (_sigmoid(mg[:, :D_MODEL]) * jnp.dot(oa_ref[...].astype(BF16), wpa_ref[...], preferred_element_type=F32)
           + _sigmoid(mg[:, D_MODEL:]) * jnp.dot(ob_ref[...].astype(BF16), wpb_ref[...], preferred_element_type=F32))
    x1 = x + jnp.dot(mix.astype(BF16), wout_ref[...], preferred_element_type=F32)
    h2 = _rms_rows(x1, gffn_ref[...]).astype(BF16)

    if slots:
        u_scr[0:SUBLANES, :] = jnp.zeros((SUBLANES, FFN_DIM), F32)
        row_in_slot = lax.broadcasted_iota(jnp.int32, (tm, 1), 0) & (SLOT - 1)
    else:
        @pl.when(pl.program_id(1) == 0)
        def _():
            u_scr[0:SUBLANES, :] = jnp.zeros((SUBLANES, FFN_DIM), F32)

    y = x1
    for c0 in range(0, FFN_DIM, f_chunk):
        fc = slice(c0, c0 + f_chunk)
        u = jnp.dot(h2, wup_ref[:, fc], preferred_element_type=F32)
        gt = jnp.dot(h2, wup_ref[:, FFN_DIM + c0:FFN_DIM + c0 + f_chunk], preferred_element_type=F32)
        u_scr[SUBLANES:SUBLANES + tm, fc] = u
        u1 = u_scr[SUBLANES - 1:SUBLANES - 1 + tm, fc]
        u2 = u_scr[SUBLANES - 2:SUBLANES - 2 + tm, fc]
        if slots:
            u1 = jnp.where(row_in_slot == 0, cb1_ref[:, fc], u1)
            u2 = jnp.where(row_in_slot < 2, cb2_ref[:, fc], u2)
            u_ref[:, fc] = u
        acc = bconv_ref[:, fc] + wconv_ref[0:1, fc] * u2 + wconv_ref[1:2, fc] * u1 + wconv_ref[2:3, fc] * u
        y = y + jnp.dot((_gelu_tanh(acc) * gt).astype(BF16), wdown_ref[fc, :], preferred_element_type=F32)
    y_ref[...] = y
    if not slots:
        tail = u_scr[tm:tm + SUBLANES, :]
        u_scr[0:SUBLANES, :] = tail
        u_ref[0] = tail


def _mix_ffn(x, oa, ob, wts, layer, tm, seq_tiles, cb=None):
    m = x.shape[0]
    slots = cb is not None
    n_seq = m // (tm * seq_tiles)
    if slots:
        idx = lambda i: (i, 0)
        grid = (m // tm,)
        sem = ("arbitrary",)
    else:
        idx = lambda b, i: (b * seq_tiles + i, 0)
        grid = (n_seq, seq_tiles)
        sem = ("arbitrary", "arbitrary")
    nidx = len(grid)

    def wspec(shape):
        return pl.BlockSpec((None,) + shape, lambda *_: (layer,) + (0,) * len(shape),
                            pipeline_mode=pl.Buffered(1))

    row = lambda w: pl.BlockSpec((tm, w), idx)
    in_specs = [row(D_MODEL), row(A_WIDTH), row(B_WIDTH),
                wspec((1, D_MODEL)), wspec((1, D_MODEL)), wspec((D_MODEL, 2 * D_MODEL)),
                wspec((A_WIDTH, D_MODEL)), wspec((B_WIDTH, D_MODEL)), wspec((D_MODEL, D_MODEL)),
                wspec((D_MODEL, 2 * FFN_DIM)), wspec((SUBLANES, FFN_DIM)), wspec((1, FFN_DIM)),
                wspec((FFN_DIM, D_MODEL))]
    args = [x, oa, ob, wts["g_attn"], wts["g_ffn"], wts["w_merge"], wts["w_pa"], wts["w_pb"], wts["w_out"],
            wts["w_up"], wts["w_conv"], wts["b_conv"], wts["w_down"]]
    if slots:
        in_specs += [row(FFN_DIM), row(FFN_DIM)]
        args += list(cb)
        u_spec = row(FFN_DIM)
        u_shape = jax.ShapeDtypeStruct((m, FFN_DIM), F32)
    else:
        u_spec = pl.BlockSpec((1, SUBLANES, FFN_DIM), lambda b, i: (b, 0, 0))
        u_shape = jax.ShapeDtypeStruct((n_seq, SUBLANES, FFN_DIM), F32)
    return pl.pallas_call(
        functools.partial(_mix_ffn_kernel, tm=tm, f_chunk=FFN_DIM, slots=slots),
        grid=grid,
        in_specs=in_specs,
        out_specs=[row(D_MODEL), u_spec],
        out_shape=[jax.ShapeDtypeStruct((m, D_MODEL), F32), u_shape],
        scratch_shapes=[pltpu.VMEM((tm + SUBLANES, FFN_DIM), F32)],
        compiler_params=pltpu.CompilerParams(dimension_semantics=sem, vmem_limit_bytes=VMEM_LIMIT),
        name="mix_ffn",
    )(*args)


def _sel_fold_matrix(n_cmp_valid, n_cmp_rows, n_sel_lanes):
    per = SEL_BLOCK // CMP_STRIDE
    m = np.zeros((n_cmp_rows, n_sel_lanes), np.float32)
    for n in range(n_cmp_valid):
        m[n, n // per] += 1.0
        if n + 1 < n_cmp_valid:
            m[n, (n + 1) // per] += 1.0
    return jnp.asarray(m, BF16)


def _pack_weights(w_in, w_cmp1, w_cmp2, g_q, g_k_cmp, g_k_slc, g_k_win, w_gate_up, b_gate, g_gla, w_pa, w_pb,
                  w_out, g_attn, g_ffn, w_up, w_conv, b_conv, w_down):
    depth = w_in.shape[0]
    splits = (A_WIDTH, KV_WIDTH, KV_WIDTH, KV_WIDTH, 3 * NSA_HEADS, GK_WIDTH, GK_WIDTH, B_WIDTH, B_WIDTH,
              GLA_GATE_RANK, 2 * D_MODEL)
    offs = np.concatenate([[0], np.cumsum(splits)])
    seg = lambda k: w_in[:, :, int(offs[k]):int(offs[k + 1])]
    padc = lambda a, w: jnp.pad(a, ((0, 0), (0, 0), (0, w - a.shape[-1])))
    w_a = jnp.concatenate([seg(0), seg(1), seg(2), seg(3), padc(seg(4), LANES), seg(5), seg(6), seg(7), seg(8),
                           padc(seg(9), LANES)], axis=-1).astype(BF16)
    same_g = jnp.eye(NSA_KV_GROUPS, dtype=F32)
    w_c1 = jnp.einsum("Lclde,gG->LlcgdGe", w_cmp1, same_g).reshape(depth, CMP_BLOCK, 2, LANES, LANES).astype(BF16)
    w_c2 = jnp.einsum("Lcef,gG->LcgeGf", w_cmp2, same_g).reshape(depth, 2, LANES, LANES).astype(BF16)
    w1h = w_cmp1.reshape(depth, 2, 2, CMP_STRIDE, HEAD_DIM, HEAD_DIM).astype(BF16)
    n_cg = 2 * NSA_KV_GROUPS
    zero1 = jnp.zeros((depth, CMP_STRIDE, HEAD_DIM, HEAD_DIM), BF16)
    zero2 = jnp.zeros((depth, HEAD_DIM, HEAD_DIM), BF16)
    w_c1_halves = jnp.concatenate([jnp.concatenate(
        [w1h[:, cg // NSA_KV_GROUPS, half] if cg == row_cg else zero1 for half in range(2) for cg in range(n_cg)],
        axis=-1) for row_cg in range(n_cg)], axis=-2)
    w_c2_full = jnp.concatenate([jnp.concatenate(
        [w_cmp2[:, cg // NSA_KV_GROUPS].astype(BF16) if cg == row_cg else zero2 for cg in range(n_cg)],
        axis=-1) for row_cg in range(n_cg)], axis=-2)
    grp = lambda n: jnp.asarray(np.kron(np.eye(n // HEAD_DIM), np.ones((HEAD_DIM, HEAD_DIM))), BF16)
    g_norm = jnp.concatenate([jnp.tile(g_q, (1, NSA_HEADS)), jnp.tile(g_k_slc, (1, NSA_KV_GROUPS)),
                              jnp.tile(g_k_win, (1, NSA_KV_GROUPS))], axis=-1)
    return {
        "w_a": w_a,
        "w_merge": w_in[:, :, int(offs[10]):].astype(BF16),
        "g_attn": g_attn[:, None, :], "g_ffn": g_ffn[:, None, :],
        "g_norm": g_norm[:, None, :], "ones128": grp(LANES),
        "g_k_cmp": jnp.tile(g_k_cmp, (1, NSA_KV_GROUPS))[:, None, :],
        "w_gate_up": jnp.pad(w_gate_up, ((0, 0), (0, LANES - GLA_GATE_RANK), (0, 0))).astype(BF16),
        "b_gate": b_gate[:, None, :], "g_gla": g_gla[:, None, :],
        "w_c1": w_c1, "w_c2": w_c2, "w_c1_halves": w_c1_halves, "w_c2_full": w_c2_full,
        "w_pa": w_pa.astype(BF16), "w_pb": w_pb.astype(BF16), "w_out": w_out.astype(BF16),
        "w_up": w_up.astype(BF16), "w_down": w_down.astype(BF16),
        "w_conv": jnp.pad(w_conv, ((0, 0), (0, SUBLANES - CONV_W), (0, 0))), "b_conv": b_conv[:, None, :],
    }


def _expand_matrix(n_blocks, n_keys):
    return jnp.asarray(np.arange(n_keys)[None, :] // SEL_BLOCK == np.arange(n_blocks)[:, None], BF16)


def _prompt_layer(x, wts, layer, b, t):
    tm = min(256, t)
    tq = min(256, t)
    q, kv_cmp, kv_slc, kv_win, ga, qg, kb, vb, rb, la = _proj_in(x, wts, layer, min(512, t))
    r3 = lambda a: a.reshape(b, t, a.shape[-1])
    kvc = _compress(r3(kv_cmp), wts, layer)
    o_a = _nsa_prompt(r3(q), r3(ga), kvc, r3(kv_slc), r3(kv_win), wts, tq)
    o_b, s_new = _gla(r3(qg), r3(kb), r3(vb), r3(la), r3(rb),
                      jnp.zeros((b, GLA_HEADS, GLA_DK, GLA_DV), F32), wts, layer,
                      tt=min(512, t), chunk=math.gcd(t, GLA_CHUNK), n_valid=math.gcd(t, GLA_CHUNK))
    y, u_tail = _mix_ffn(x, o_a.reshape(b * t, A_WIDTH), o_b.reshape(b * t, B_WIDTH), wts, layer, tm, t // tm)
    kv6 = lambda a: a.reshape(b, t, 2, NSA_KV_GROUPS, HEAD_DIM)
    n_win = min(WINDOW, t)
    state = (kv6(kv_cmp), kv6(kv_slc), kv6(kv_win)[:, t - n_win:], s_new, u_tail[:, SUBLANES - (CONV_W - 1):])
    return y, state


def _sample_layer(x, wts, layer, bs, n_tok, past, cache_cmp, cache_slc, win_buf, gla_state, conv_buf, page_table):
    m = bs * SLOT
    q, kv_cmp, kv_slc, kv_win, ga, qg, kb, vb, rb, la = _proj_in(x, wts, layer, m)
    r3 = lambda a: a.reshape(bs, SLOT, a.shape[-1])
    o_a = _nsa_sample(r3(q), r3(ga), r3(kv_cmp), r3(kv_slc), r3(kv_win), win_buf, cache_cmp, cache_slc,
                      page_table, wts, layer, past, n_tok)
    o_b, s_new = _gla(r3(qg), r3(kb), r3(vb), r3(la), r3(rb), gla_state[layer], wts, layer,
                      tt=SLOT, chunk=SLOT, n_valid=n_tok)
    cb = conv_buf[layer]
    zeros = lambda n: jnp.zeros((bs, n, FFN_DIM), F32)
    cb1 = jnp.concatenate([cb[:, 1:2], zeros(SLOT - 1)], axis=1).reshape(m, FFN_DIM)
    cb2 = jnp.concatenate([cb, zeros(SLOT - 2)], axis=1).reshape(m, FFN_DIM)
    y, u = _mix_ffn(x, o_a.reshape(m, A_WIDTH), o_b.reshape(m, B_WIDTH), wts, layer, m, 1, cb=(cb1, cb2))
    kv6 = lambda a: r3(a)[:, :n_tok].reshape(bs, n_tok, 2, NSA_KV_GROUPS, HEAD_DIM)
    u_ext = jnp.concatenate([cb, u.reshape(bs, SLOT, FFN_DIM)[:, :n_tok]], axis=1)
    state = (kv6(kv_cmp), kv6(kv_slc), kv6(kv_win), s_new, u_ext[:, -(CONV_W - 1):])
    return y, state


def kernel(x_prompt, x_sample, cache_kv_cmp, cache_kv_slc, state_kv_win, state_gla, state_conv, page_table,
           w_in, w_cmp1, w_cmp2, g_q, g_k_cmp, g_k_slc, g_k_win, w_gate_up, b_gate, g_gla, w_pa, w_pb, w_out,
           g_attn, g_ffn, w_up, w_conv, b_conv, w_down):
    depth = w_in.shape[0]
    bp, t, _ = x_prompt.shape
    bs, n_tok, _ = x_sample.shape
    page = cache_kv_cmp.shape[2]
    past = page_table.shape[1] * page
    assert n_tok <= SLOT and t % CMP_STRIDE == 0 and past % SEL_BLOCK == 0

    wts = _pack_weights(w_in, w_cmp1, w_cmp2, g_q, g_k_cmp, g_k_slc, g_k_win, w_gate_up, b_gate, g_gla, w_pa,
                        w_pb, w_out, g_attn, g_ffn, w_up, w_conv, b_conv, w_down)
    n_cmp_p = t // CMP_STRIDE
    fold = _sel_fold_matrix(n_cmp_p - 1, n_cmp_p, LANES - LANE_HOT).T
    wts["fold_prompt"] = jnp.pad(fold, ((LANE_HOT, 0), (0, 0)))
    gate_expand = np.zeros((3, LANES, A_WIDTH), np.float32)
    for br in range(3):
        for h in range(NSA_HEADS):
            gate_expand[br, br * NSA_HEADS + h, h * HEAD_DIM:(h + 1) * HEAD_DIM] = 1.0
    wts["gate_expand"] = jnp.asarray(gate_expand, BF16)
    n_cmp_s = (past + CMP_STRIDE) // CMP_STRIDE - 1
    sel_lanes = -(-(past // SEL_BLOCK + 1) // LANES) * LANES
    tk_s = min(1024, past)
    wts["m_sel_sample"] = _sel_fold_matrix(n_cmp_s, n_cmp_s, sel_lanes)
    blk_per_chunk = tk_s // SEL_BLOCK
    wts["e_sel_sample"] = jnp.stack([_expand_matrix(LANES, LANES * SEL_BLOCK)[:, j * tk_s:(j + 1) * tk_s]
                                     for j in range(LANES // blk_per_chunk)])

    cache_cmp = cache_kv_cmp.transpose(0, 1, 3, 4, 5, 2).reshape(depth, -1, KV_WIDTH, page)
    regroup = np.zeros((page, page), np.float32)
    hpp = page // CMP_STRIDE
    for l in range(CMP_STRIDE):
        for h in range(hpp):
            regroup[l * hpp + h, h * CMP_STRIDE + l] = 1.0
    wts["regroup"] = jnp.asarray(regroup, BF16)
    cache_slc = cache_kv_slc.transpose(0, 1, 3, 4, 5, 2)
    win_buf = state_kv_win.transpose(0, 1, 3, 4, 5, 2)

    y_p = x_prompt.reshape(bp * t, D_MODEL)
    y_s = jnp.pad(x_sample, ((0, 0), (0, SLOT - n_tok), (0, 0))).reshape(bs * SLOT, D_MODEL)
    st_p, st_s = [], []
    for l in range(depth):
        y_p, sp = _prompt_layer(y_p, wts, l, bp, t)
        y_s, ss = _sample_layer(y_s, wts, l, bs, n_tok, past, cache_cmp, cache_slc, win_buf, state_gla, state_conv,
                                page_table)
        st_p.append(sp)
        st_s.append(ss)
    stack = lambda sts, k: jnp.stack([s[k] for s in sts])
    sample_states = [stack(st_s, k) for k in range(5)]
    sample_states[2] = jnp.concatenate([state_kv_win[:, :, n_tok:], sample_states[2]], axis=2)
    return ((y_p.reshape(bp, t, D_MODEL), y_s.reshape(bs, SLOT, D_MODEL)[:, :n_tok])
            + tuple(stack(st_p, k) for k in range(5)) + tuple(sample_states))
```

```python
import functools
import math

import numpy as np
import jax
import jax.numpy as jnp
from jax import lax
from jax.experimental import pallas as pl
from jax.experimental.pallas import tpu as pltpu

F32 = jnp.float32
BF16 = jnp.bfloat16

D_MODEL = 1024
HEAD_DIM = 64
NSA_HEADS = 8
NSA_KV_GROUPS = 2
NSA_GROUP_SIZE = NSA_HEADS // NSA_KV_GROUPS
CMP_STRIDE = 16
CMP_BLOCK = 2 * CMP_STRIDE
SEL_BLOCK = 64
SEL_TOPK = 16
WINDOW = 512
GLA_HEADS = 4
GLA_DK = 64
GLA_DV = 128
GLA_GATE_RANK = 16
GLA_GATE_NORM = 16.0
GLA_CHUNK = 64
FFN_DIM = 2816
CONV_W = 3
EPS = 1e-6
NEG_INF = -1e30
FORCE_SCORE = 1e9
BELOW_ALL_SCORES = -3e38

A_WIDTH = NSA_HEADS * HEAD_DIM
B_WIDTH = GLA_HEADS * GLA_DV
KV_WIDTH = 2 * NSA_KV_GROUPS * HEAD_DIM
GK_WIDTH = GLA_HEADS * GLA_DK
LANES = 128
SUBLANES = 8
SLOT = SUBLANES
ALIBI = tuple(float(2.0 ** (-8.0 * (h + 1) / NSA_HEADS)) for h in range(NSA_HEADS))

C_Q, C_CMP, C_SLC, C_WIN, C_GATE = 0, 512, 768, 1024, 1280
C_QB, C_KB, C_VB, C_RB, C_LR, C_END = 1408, 1664, 1920, 2432, 2944, 3072
NORM_W = A_WIDTH + 2 * LANES

VMEM_LIMIT = 56 * 1024 * 1024


def _dot(a, b):
    return jnp.dot(a.astype(BF16), b.astype(BF16), preferred_element_type=F32)


def _dot_nt(a, b):
    return lax.dot_general(a.astype(BF16), b.astype(BF16), (((1,), (1,)), ((), ())),
                           preferred_element_type=F32)


def _dot_tn(a, b):
    return lax.dot_general(a.astype(BF16), b.astype(BF16), (((0,), (0,)), ((), ())),
                           preferred_element_type=F32)


def _split_bf16(x, parts):
    out = []
    for _ in range(parts):
        p = x.astype(BF16)
        out.append(p)
        x = x - p.astype(F32)
    return out


def _dot_wide(x, m, parts, dot=None):
    dot = dot or (lambda a, b: jnp.dot(a, b, preferred_element_type=F32))
    acc = None
    for p in _split_bf16(x, parts):
        t = dot(p, m)
        acc = t if acc is None else acc + t
    return acc


def _rms_rows(x, g):
    return x * lax.rsqrt(jnp.mean(x * x, axis=-1, keepdims=True) + EPS) * g


def _rms_groups(x, ones_blockdiag, g):
    sq = x * x
    ms = jnp.concatenate([_dot_wide(sq[:, j:j + LANES], ones_blockdiag, 2)
                          for j in range(0, x.shape[1], LANES)], axis=1) * (1.0 / HEAD_DIM)
    return x * lax.rsqrt(ms + EPS) * g


def _sigmoid(x):
    return 1.0 / (1.0 + jnp.exp(-x))


def _gelu_tanh(x):
    return 0.5 * x * (1.0 + jnp.tanh(math.sqrt(2.0 / math.pi) * (x + 0.044715 * (x * x * x))))


def _log_sigmoid(x):
    return jnp.minimum(x, 0.0) - jnp.log1p(jnp.exp(-jnp.abs(x)))


def _topk_mask(score, n_cand, k):
    lane = lax.broadcasted_iota(jnp.int32, score.shape, 1)
    ahead = jnp.zeros(score.shape, F32)
    for j in range(n_cand):
        col = score[:, j:j + 1]
        ahead = ahead + jnp.where(col > score, 1.0, jnp.where(col == score, jnp.where(lane > j, 1.0, 0.0), 0.0))
    return jnp.where(lane < n_cand, jnp.where(ahead < k, 1.0, 0.0), 0.0)


def _masked_softmax(s, mask):
    s = jnp.where(mask, s, NEG_INF)
    m = jnp.max(s, axis=-1, keepdims=True)
    e = jnp.where(mask, jnp.exp(s - m), 0.0)
    l = jnp.sum(e, axis=-1, keepdims=True)
    return e * jnp.where(l > 0.0, 1.0 / l, 0.0)


def _stack_heads(q, g):
    base = g * NSA_GROUP_SIZE
    return jnp.concatenate([q[:, (base + r) * HEAD_DIM:(base + r + 1) * HEAD_DIM]
                            for r in range(NSA_GROUP_SIZE)], axis=0)


def _head_slopes(g, n):
    row = lax.broadcasted_iota(jnp.int32, (NSA_GROUP_SIZE * n, 1), 0)
    slope = jnp.full((NSA_GROUP_SIZE * n, 1), ALIBI[g * NSA_GROUP_SIZE + NSA_GROUP_SIZE - 1], F32)
    for r in reversed(range(NSA_GROUP_SIZE - 1)):
        slope = jnp.where(row < (r + 1) * n, ALIBI[g * NSA_GROUP_SIZE + r], slope)
    return slope


def _gate_cols(ga, branch, g):
    c0 = branch * NSA_HEADS + g * NSA_GROUP_SIZE
    return jnp.concatenate([ga[:, c0 + r:c0 + r + 1] for r in range(NSA_GROUP_SIZE)], axis=0)


def _block_scores(blk_imp, q_pos, lane_i, n_sel):
    forced = (lane_i == 0) | (lane_i == lax.shift_right_logical(q_pos, int(math.log2(SEL_BLOCK))))
    valid = lane_i * SEL_BLOCK <= q_pos
    score = jnp.where(forced, FORCE_SCORE, jnp.where(valid, blk_imp, NEG_INF))
    return jnp.where(lane_i < n_sel, score, BELOW_ALL_SCORES)


def _compress_rows(load_rows, w1_ref, w2_ref, ones128_ref, gk_ref):
    out = []
    for c in range(2):
        acc = None
        for l in range(CMP_BLOCK):
            t = jnp.dot(load_rows(c, l).astype(BF16), w1_ref[l, c], preferred_element_type=F32)
            acc = t if acc is None else acc + t
        out.append(jnp.dot(_gelu_tanh(acc).astype(BF16), w2_ref[c], preferred_element_type=F32))
    return _rms_groups(out[0], ones128_ref[...], gk_ref[...]), out[1]


def _proj_in_kernel(x_ref, gattn_ref, w_ref, gnorm_ref, ones_ref, wgu_ref, bg_ref,
                    q_ref, cmp_ref, slc_ref, win_ref, ga_ref, qg_ref, kb_ref, vb_ref, rb_ref, la_ref):
    h = _rms_rows(x_ref[...], gattn_ref[...])
    p = jnp.dot(h.astype(BF16), w_ref[...], preferred_element_type=F32)
    to_norm = jnp.concatenate([p[:, C_Q:C_CMP], p[:, C_SLC:C_SLC + LANES], p[:, C_WIN:C_WIN + LANES]], axis=1)
    y = _rms_groups(to_norm, ones_ref[...], gnorm_ref[...])
    q_ref[...] = y[:, :A_WIDTH] * HEAD_DIM ** -0.5
    cmp_ref[...] = p[:, C_CMP:C_SLC]
    slc_ref[...] = jnp.concatenate([y[:, A_WIDTH:A_WIDTH + LANES], p[:, C_SLC + LANES:C_WIN]], axis=1)
    win_ref[...] = jnp.concatenate([y[:, A_WIDTH + LANES:], p[:, C_WIN + LANES:C_GATE]], axis=1)
    ga_ref[...] = _sigmoid(p[:, C_GATE:C_QB])
    qg_ref[...] = p[:, C_QB:C_KB] * GLA_DK ** -0.5
    kb_ref[...] = p[:, C_KB:C_VB]
    vb_ref[...] = p[:, C_VB:C_RB]
    rb_ref[...] = p[:, C_RB:C_LR]
    z = jnp.dot(p[:, C_LR:C_END].astype(BF16), wgu_ref[...], preferred_element_type=F32) + bg_ref[...]
    la_ref[...] = _log_sigmoid(z) * (1.0 / GLA_GATE_NORM)


def _const_spec(shape, layer=None):
    nd = len(shape)
    if layer is None:
        return pl.BlockSpec(shape, lambda *_: (0,) * nd)
    return pl.BlockSpec((None,) + shape, lambda *_: (layer,) + (0,) * nd)


def _proj_in(x, wts, layer, tm):
    m = x.shape[0]
    widths = (A_WIDTH, KV_WIDTH, KV_WIDTH, KV_WIDTH, LANES, GK_WIDTH, GK_WIDTH, B_WIDTH, B_WIDTH, GK_WIDTH)
    row = lambda w: pl.BlockSpec((tm, w), lambda i: (i, 0))
    return pl.pallas_call(
        _proj_in_kernel,
        grid=(m // tm,),
        in_specs=[row(D_MODEL),
                  _const_spec((1, D_MODEL), layer), _const_spec((D_MODEL, C_END), layer),
                  _const_spec((1, NORM_W), layer), _const_spec((LANES, LANES)),
                  _const_spec((LANES, GK_WIDTH), layer), _const_spec((1, GK_WIDTH), layer)],
        out_specs=[row(w) for w in widths],
        out_shape=[jax.ShapeDtypeStruct((m, w), F32) for w in widths],
        compiler_params=pltpu.CompilerParams(dimension_semantics=("arbitrary",), vmem_limit_bytes=VMEM_LIMIT),
        name="proj_in",
    )(x, wts["g_attn"], wts["w_a"], wts["g_norm"], wts["ones128"], wts["w_gate_up"], wts["b_gate"])


def _compress_kernel(kv_ref, w1_ref, w2_ref, ones_ref, gk_ref, out_ref, pad_ref, *, t):
    for c in range(2):
        pad_ref[c, 0:t, :] = kv_ref[0, :, c * LANES:(c + 1) * LANES]
        pad_ref[c, t:t + CMP_STRIDE, :] = jnp.zeros((CMP_STRIDE, LANES), F32)
    n_blk = t // CMP_STRIDE
    kc, vc = _compress_rows(lambda c, l: pad_ref[c, pl.ds(l, n_blk, stride=CMP_STRIDE), :],
                            w1_ref, w2_ref, ones_ref, gk_ref)
    out_ref[0] = jnp.concatenate([kc, vc], axis=1)


def _compress(kv_cmp, wts, layer):
    b, t, _ = kv_cmp.shape
    n_blk = t // CMP_STRIDE
    return pl.pallas_call(
        functools.partial(_compress_kernel, t=t),
        grid=(b,),
        in_specs=[pl.BlockSpec((1, t, KV_WIDTH), lambda i: (i, 0, 0)),
                  _const_spec((CMP_BLOCK, 2, LANES, LANES), layer), _const_spec((2, LANES, LANES), layer),
                  _const_spec((LANES, LANES)), _const_spec((1, LANES), layer)],
        out_specs=pl.BlockSpec((1, n_blk, KV_WIDTH), lambda i: (i, 0, 0)),
        out_shape=jax.ShapeDtypeStruct((b, n_blk, KV_WIDTH), F32),
        scratch_shapes=[pltpu.VMEM((2, t + CMP_STRIDE, LANES), F32)],
        compiler_params=pltpu.CompilerParams(dimension_semantics=("arbitrary",), vmem_limit_bytes=VMEM_LIMIT),
        name="compress",
    )(kv_cmp, wts["w_c1"], wts["w_c2"], wts["ones128"], wts["g_k_cmp"])


MASK_BIG = 2.0 ** 100
LANE_HI, LANE_LO, LANE_ONE, LANE_HOT = HEAD_DIM, HEAD_DIM + 1, HEAD_DIM + 2, HEAD_DIM + 3
SEL_SHIFT = int(math.log2(SEL_BLOCK))


def _group_halves(x, g):
    return x if g == 0 else pltpu.roll(x, HEAD_DIM, axis=1)


def _build_prompt_operands(kvc_ref, slc_ref, win_ref, kc_ref, vc_ref, ks_ref, vs_ref, kw_ref, vw_ref):
    t = slc_ref.shape[1]
    lane = lax.broadcasted_iota(jnp.int32, (t, LANES), 1)
    pos = lax.broadcasted_iota(jnp.int32, (t, LANES), 0)
    blk = lax.shift_right_logical(pos, SEL_SHIFT)
    pos_lanes = jnp.where(lane == LANE_HI, blk, jnp.where(lane == LANE_LO, pos & (SEL_BLOCK - 1),
                                                          jnp.where(lane == LANE_ONE, 1, 0)))
    aux_win = pos_lanes.astype(F32)
    aux_sel = jnp.where(lane - LANE_HOT == blk, 1, pos_lanes).astype(F32)
    n_cmp = kvc_ref.shape[1]
    lane_c = lax.broadcasted_iota(jnp.int32, (n_cmp, LANES), 1)
    blk_c = lax.broadcasted_iota(jnp.int32, (n_cmp, LANES), 0)
    aux_cmp = jnp.where(lane_c == LANE_HI, blk_c, jnp.where((lane_c == LANE_LO) | (lane_c == LANE_ONE), 1, 0)).astype(F32)
    lane_p = lax.broadcasted_iota(jnp.int32, (WINDOW, LANES), 1)
    before_seq = jnp.where(lane_p == LANE_HOT, -MASK_BIG, 0.0).astype(BF16)
    for g in range(NSA_KV_GROUPS):
        ks_ref[g] = jnp.where(lane < HEAD_DIM, _group_halves(slc_ref[0, :, 0:LANES], g), aux_sel).astype(BF16)
        vs_ref[g] = jnp.where(lane < HEAD_DIM, _group_halves(slc_ref[0, :, LANES:], g), 1.0).astype(BF16)
        kw_ref[g, 0:WINDOW, :] = before_seq
        vw_ref[g, 0:WINDOW, :] = jnp.zeros((WINDOW, LANES), BF16)
        kw_ref[g, WINDOW:, :] = jnp.where(lane < HEAD_DIM, _group_halves(win_ref[0, :, 0:LANES], g), aux_win).astype(BF16)
        vw_ref[g, WINDOW:, :] = jnp.where(lane < HEAD_DIM, _group_halves(win_ref[0, :, LANES:], g), 1.0).astype(BF16)
        kc_ref[g] = jnp.where(lane_c < HEAD_DIM, _group_halves(kvc_ref[0, :, 0:LANES], g), aux_cmp).astype(BF16)
        vc_ref[g] = jnp.where(lane_c < HEAD_DIM, _group_halves(kvc_ref[0, :, LANES:], g), 0.0).astype(BF16)


def _nsa_prompt_kernel(q_ref, ga_ref, kvc_ref, slc_ref, win_ref, fold_ref, gexp_ref, o_ref,
                       kc_ref, vc_ref, ks_ref, vs_ref, kw_ref, vw_ref, m_scr, acc_scr, *, tq, kw_sel):
    i = pl.program_id(1)

    @pl.when(i == 0)
    def _():
        _build_prompt_operands(kvc_ref, slc_ref, win_ref, kc_ref, vc_ref, ks_ref, vs_ref, kw_ref, vw_ref)

    q0 = i * tq
    q0f = q0.astype(F32)
    q = q_ref[0]
    rows = NSA_GROUP_SIZE * tq
    n_cmp = kvc_ref.shape[1]
    n_sel = slc_ref.shape[1] // SEL_BLOCK
    k_top = min(SEL_TOPK, n_sel)
    lane = lax.broadcasted_iota(jnp.int32, (tq, LANES), 1)
    lane4 = lax.broadcasted_iota(jnp.int32, (rows, LANES), 1)
    t_w = lax.broadcasted_iota(jnp.int32, (rows, tq), 0) & (tq - 1)
    lane_w = lax.broadcasted_iota(jnp.int32, (rows, tq), 1)
    causal_bias = jnp.where(lane_w <= t_w, 0.0, -MASK_BIG)
    far_bias = jnp.where(lane_w > t_w, 0.0, -MASK_BIG)
    cmp_valid = (lax.broadcasted_iota(jnp.int32, (rows, n_cmp), 1) * CMP_STRIDE + (CMP_BLOCK - 1)
                 <= q0 + (lax.broadcasted_iota(jnp.int32, (rows, n_cmp), 0) & (tq - 1)))
    hot_rows = slice(HEAD_DIM, HEAD_DIM + 8 * (-(-(LANE_HOT - HEAD_DIM + n_sel) // 8)))
    n_hot = hot_rows.stop - hot_rows.start
    blk_t = lax.broadcasted_iota(jnp.int32, (n_hot, tq), 0) - (LANE_HOT - HEAD_DIM)
    qpos_t = q0 + lax.broadcasted_iota(jnp.int32, (n_hot, tq), 1)
    is_blk = (blk_t >= 0) & (blk_t < n_sel)

    def lane_tiles_max(s):
        m = s[:, 0:LANES]
        for j in range(1, s.shape[1] // LANES):
            m = jnp.maximum(m, s[:, j * LANES:(j + 1) * LANES])
        return m

    def normalised(acc):
        return acc * (1.0 / pltpu.roll(acc, HEAD_DIM, axis=1))

    def attend_selected(qa, g):
        n_full = q0 // kw_sel
        k_last = pl.multiple_of(n_full * kw_sel, kw_sel)
        visible = lax.broadcasted_iota(jnp.int32, (rows, kw_sel), 1) <= (
            (q0 - k_last) + (lax.broadcasted_iota(jnp.int32, (rows, kw_sel), 0) & (tq - 1)))
        s = _dot_nt(qa, ks_ref[g, pl.ds(k_last, kw_sel), :]) + jnp.where(visible, 0.0, -MASK_BIG)
        m = jnp.max(lane_tiles_max(s), axis=-1, keepdims=True)
        m_scr[...] = jnp.broadcast_to(m, (rows, LANES))
        acc_scr[...] = _dot(jnp.exp(s - m), vs_ref[g, pl.ds(k_last, kw_sel), :])

        def body(c, _):
            rows_c = pl.ds(pl.multiple_of(c * kw_sel, kw_sel), kw_sel)
            s = _dot_nt(qa, ks_ref[g, rows_c, :])
            m_old = m_scr[...]
            m_new = jnp.maximum(m_old, jnp.max(lane_tiles_max(s), axis=-1, keepdims=True))
            p = jnp.exp(s - jnp.concatenate([m_new] * (kw_sel // LANES), axis=1))
            acc_scr[...] = jnp.exp(m_old - m_new) * acc_scr[...] + _dot(p, vs_ref[g, rows_c, :])
            m_scr[...] = m_new
            return 0

        lax.fori_loop(0, n_full, body, 0)
        return normalised(acc_scr[...])

    def attend_window(qa, g):
        k0 = pl.multiple_of(q0, tq)
        s = _dot_nt(qa, kw_ref[g, pl.ds(k0, WINDOW + tq), :])
        s = jnp.concatenate([s[:, 0:tq] + far_bias, s[:, tq:WINDOW], s[:, WINDOW:] + causal_bias], axis=1)
        m = jnp.max(lane_tiles_max(s), axis=-1, keepdims=True)
        return normalised(_dot(jnp.exp(s - m), vw_ref[g, pl.ds(k0, WINDOW + tq), :]))

    branch_out = [[], [], []]
    for g in range(NSA_KV_GROUPS):
        q_rows, base_rows, cbase_rows = [], [], []
        for r in range(NSA_GROUP_SIZE):
            h = g * NSA_GROUP_SIZE + r
            tile = q[:, (h // 2) * LANES:(h // 2 + 1) * LANES]
            q_rows.append(tile if h % 2 == 0 else pltpu.roll(tile, HEAD_DIM, axis=1))
            sl = ALIBI[h]
            base_rows.append(jnp.where(lane == LANE_HI, SEL_BLOCK * sl, jnp.where(
                lane == LANE_LO, sl, jnp.where(lane == LANE_ONE, -sl * q0f, 0.0))))
            cbase_rows.append(jnp.where(lane == LANE_HI, CMP_STRIDE * sl, jnp.where(
                lane == LANE_LO, (CMP_BLOCK - 1) * sl, jnp.where(lane == LANE_ONE, -sl * q0f, 0.0))))
        q4 = jnp.concatenate(q_rows, axis=0)
        base4 = jnp.concatenate(base_rows, axis=0)
        head_lanes = lane4 < HEAD_DIM

        qa = jnp.where(head_lanes, q4, jnp.where(lane4 == LANE_HOT, 1.0, base4)).astype(BF16)
        branch_out[2].append(attend_window(qa, g))

        qa = jnp.where(head_lanes, q4, jnp.concatenate(cbase_rows, axis=0)).astype(BF16)
        s = jnp.where(cmp_valid, _dot_nt(qa, kc_ref[g]), -MASK_BIG)
        m = jnp.max(s, axis=-1, keepdims=True)
        e = jnp.where(cmp_valid, jnp.exp(s - m), 0.0)
        l = jnp.sum(e, axis=-1, keepdims=True)
        p = e * jnp.where(l > 0.0, 1.0 / l, 0.0)
        branch_out[0].append(_dot(p, vc_ref[g]))
        imp = p[0:tq]
        for r in range(1, NSA_GROUP_SIZE):
            imp = imp + p[r * tq:(r + 1) * tq]

        blk_imp = _dot_wide(imp, fold_ref[...], 3, dot=lambda a, mt: _dot_nt(mt, a))[hot_rows]
        forced = (blk_t == 0) | (blk_t == lax.shift_right_logical(qpos_t, SEL_SHIFT))
        score = jnp.where(forced, FORCE_SCORE, jnp.where(blk_t * SEL_BLOCK <= qpos_t, blk_imp, NEG_INF))
        score = jnp.where(is_blk, score, BELOW_ALL_SCORES)
        ahead = jnp.zeros((n_hot, tq), F32)
        for j in range(n_sel):
            row_j = score[LANE_HOT - HEAD_DIM + j:LANE_HOT - HEAD_DIM + j + 1, :]
            ahead = ahead + jnp.where(row_j > score, 1.0, jnp.where(row_j == score, jnp.where(blk_t > j, 1.0, 0.0), 0.0))
        sel_bias_t = jnp.where(is_blk & (ahead >= k_top), -MASK_BIG, 0.0)
        sel_bias = jnp.transpose(jnp.concatenate(
            [jnp.zeros((hot_rows.start, tq), F32), sel_bias_t, jnp.zeros((LANES - hot_rows.stop, tq), F32)], axis=0))
        sel_bias4 = jnp.concatenate([sel_bias] * NSA_GROUP_SIZE, axis=0)

        qa = jnp.where(head_lanes, q4, base4 + sel_bias4).astype(BF16)
        branch_out[1].append(attend_selected(qa, g))

    ga = ga_ref[0]
    out = None
    for br in range(3):
        tiles = []
        for g in range(NSA_KV_GROUPS):
            o4 = branch_out[br][g]
            for r in range(0, NSA_GROUP_SIZE, 2):
                even, odd = o4[r * tq:(r + 1) * tq], o4[(r + 1) * tq:(r + 2) * tq]
                tiles.append(jnp.where(lane < HEAD_DIM, even, pltpu.roll(odd, HEAD_DIM, axis=1)))
        term = _dot_wide(ga, gexp_ref[br], 3) * jnp.concatenate(tiles, axis=1)
        out = term if out is None else out + term
    o_ref[0] = out


def _nsa_prompt(q, ga, kvc, kv_slc, kv_win, wts, tq):
    b, t, _ = q.shape
    n_cmp = kvc.shape[1]
    tile = lambda w: pl.BlockSpec((1, tq, w), lambda bi, i: (bi, i, 0))
    full = lambda n, w: pl.BlockSpec((1, n, w), lambda bi, i: (bi, 0, 0))
    kw_sel = min(max(512, tq), t)
    assert tq % LANES == 0 and WINDOW % tq == 0 and tq < WINDOW and kw_sel % tq == 0 and t % kw_sel == 0
    assert LANE_HOT + t // SEL_BLOCK <= LANES
    aug = lambda n: pltpu.VMEM((NSA_KV_GROUPS, n, LANES), BF16)
    stat = pltpu.VMEM((NSA_GROUP_SIZE * tq, LANES), F32)
    return pl.pallas_call(
        functools.partial(_nsa_prompt_kernel, tq=tq, kw_sel=kw_sel),
        grid=(b, t // tq),
        in_specs=[tile(A_WIDTH), tile(LANES), full(n_cmp, KV_WIDTH), full(t, KV_WIDTH), full(t, KV_WIDTH),
                  _const_spec((LANES, n_cmp)), _const_spec((3, LANES, A_WIDTH))],
        out_specs=tile(A_WIDTH),
        out_shape=jax.ShapeDtypeStruct((b, t, A_WIDTH), F32),
        scratch_shapes=[aug(n_cmp), aug(n_cmp), aug(t), aug(t), aug(t + WINDOW), aug(t + WINDOW), stat, stat],
        compiler_params=pltpu.CompilerParams(dimension_semantics=("arbitrary", "arbitrary"),
                                             vmem_limit_bytes=VMEM_LIMIT),
        name="nsa_prompt",
    )(q, ga, kvc, kv_slc, kv_win, wts["fold_prompt"], wts["gate_expand"])


def _nsa_sample_kernel(pt_ref, q_ref, ga_ref, ncmp_ref, nslc_ref, nwin_ref, wbuf_ref, cmp_hbm, slc_hbm,
                       w1_ref, w2_ref, ones_ref, gk_ref, msel_ref, esel_ref, regroup_ref,
                       o_ref, xcn_ref, xc_ref, xs_ref, half_scr, sem_ref, *, layer, past, n_tok, tk):
    b = pl.program_id(0)
    nb = pl.num_programs(0)
    n_pages = pt_ref.shape[1]
    page = past // n_pages
    hpp = page // CMP_STRIDE
    n_cmp = past // CMP_STRIDE
    slot = b % 2

    def page_copies(bb, sl):
        out = []
        for p in range(n_pages):
            pg = pt_ref[bb, p]
            out.append(pltpu.make_async_copy(cmp_hbm.at[layer, pg], xcn_ref.at[sl, :, pl.ds(p * page, page)],
                                             sem_ref.at[0, sl]))
            out.append(pltpu.make_async_copy(slc_hbm.at[layer, pg], xs_ref.at[sl, :, :, :, pl.ds(p * page, page)],
                                             sem_ref.at[1, sl]))
        return out

    def rows_on_lanes(new_ref):
        new = jnp.where(is_tok, new_ref[0], 0.0)
        return jnp.transpose(jnp.concatenate([new, jnp.zeros((LANES - SLOT, KV_WIDTH), F32)], axis=0))

    @pl.when(b == 0)
    def _():
        for cp in page_copies(0, 0):
            cp.start()

    @pl.when(b + 1 < nb)
    def _():
        for cp in page_copies(b + 1, 1 - slot):
            cp.start()

    is_tok = lax.broadcasted_iota(jnp.int32, (SLOT, 1), 0) < n_tok
    xc_ref[:, n_cmp:n_cmp + SUBLANES, :] = jnp.zeros((CMP_STRIDE, SUBLANES, KV_WIDTH), F32)
    for l in range(n_tok):
        xc_ref[l, n_cmp:n_cmp + 1, :] = ncmp_ref[0, l:l + 1, :]
    new_slc = rows_on_lanes(nslc_ref)
    new_win = rows_on_lanes(nwin_ref)
    for c in range(2):
        for g in range(NSA_KV_GROUPS):
            r0 = (c * NSA_KV_GROUPS + g) * HEAD_DIM
            xs_ref[slot, c, g, :, past:past + LANES] = new_slc[r0:r0 + HEAD_DIM, :]
    n_buf = wbuf_ref.shape[-1]

    for cp in page_copies(b, slot):
        cp.wait()

    q = q_ref[0]
    ga = ga_ref[0]
    rows = NSA_GROUP_SIZE * SLOT
    n_sel = past // SEL_BLOCK + 1
    q_pos = past + lax.broadcasted_iota(jnp.int32, (SLOT, 1), 0)
    q_pos4 = jnp.concatenate([q_pos] * NSA_GROUP_SIZE, axis=0)
    sel_lanes = msel_ref.shape[1]
    lane_i = lax.broadcasted_iota(jnp.int32, (1, sel_lanes), 1)
    ends = lax.broadcasted_iota(jnp.int32, (1, n_cmp), 1) * CMP_STRIDE + (CMP_BLOCK - 1)

    for p in range(n_pages):
        regrouped = _dot_nt(regroup_ref[...], xcn_ref[slot, :, p * page:(p + 1) * page])
        for l in range(CMP_STRIDE):
            xc_ref[l, p * hpp:(p + 1) * hpp, :] = regrouped[l * hpp:(l + 1) * hpp, :]

    ab = None
    for l in range(CMP_STRIDE):
        t = jnp.dot(xc_ref[l].astype(BF16), w1_ref[l], preferred_element_type=F32)
        ab = t if ab is None else ab + t
    half_scr[...] = ab[:, KV_WIDTH:]
    hid = ab[0:n_cmp, :KV_WIDTH] + half_scr[1:n_cmp + 1, :]
    kvc = jnp.dot(_gelu_tanh(hid).astype(BF16), w2_ref[...], preferred_element_type=F32)
    kc = _rms_groups(kvc[:, :LANES], ones_ref[...], gk_ref[...])
    vc = kvc[:, LANES:]
    pieces = []
    for g in range(NSA_KV_GROUPS):
        slope = _head_slopes(g, SLOT)
        qg = _stack_heads(q, g).astype(BF16)
        kcol = slice(g * HEAD_DIM, (g + 1) * HEAD_DIM)
        vcol = slice(LANES + g * HEAD_DIM, LANES + (g + 1) * HEAD_DIM)

        dist = q_pos4 - ends
        s = _dot_nt(qg, kc[:, kcol]) - slope * dist.astype(F32)
        p = _masked_softmax(s, dist >= 0)
        o_cmp = _dot(p, vc[:, kcol])
        imp = p[0:SLOT]
        for r in range(1, NSA_GROUP_SIZE):
            imp = imp + p[r * SLOT:(r + 1) * SLOT]

        blk_imp = _dot_wide(imp, msel_ref[...], 3)
        sel = _topk_mask(_block_scores(blk_imp, q_pos, lane_i, n_sel), n_sel, min(SEL_TOPK, n_sel))
        sel4 = jnp.concatenate([sel] * NSA_GROUP_SIZE, axis=0).astype(BF16)

        chunks_per_slab = esel_ref.shape[0]
        bounds = [(c * tk, tk if c * tk < past else LANES) for c in range(past // tk + 1)]
        scores = []
        for c, (k0, n_k) in enumerate(bounds):
            dist = q_pos4 - (k0 + lax.broadcasted_iota(jnp.int32, (1, n_k), 1))
            s = _dot(qg, xs_ref[slot, 0, g, :, k0:k0 + n_k]) - slope * dist.astype(F32)
            if k0 < past:
                slab = c // chunks_per_slab
                in_sel = _dot(sel4[:, slab * LANES:(slab + 1) * LANES], esel_ref[c % chunks_per_slab])
            else:
                in_sel = sel4[:, n_sel - 1:n_sel].astype(F32)
            scores.append(jnp.where((in_sel > 0.5) & (dist >= 0), s, NEG_INF))
        m = functools.reduce(jnp.maximum, [jnp.max(s, axis=-1, keepdims=True) for s in scores])
        l = jnp.zeros((rows, 1), F32)
        o_slc = jnp.zeros((rows, HEAD_DIM), F32)
        for s, (k0, n_k) in zip(scores, bounds):
            p = jnp.exp(s - m)
            l = l + jnp.sum(p, axis=-1, keepdims=True)
            o_slc = o_slc + _dot_nt(p, xs_ref[slot, 1, g, :, k0:k0 + n_k])
        o_slc = o_slc * (1.0 / l)

        k_pos = (past - n_buf) + lax.broadcasted_iota(jnp.int32, (1, n_buf + LANES), 1)
        dist = q_pos4 - k_pos
        s = jnp.concatenate([_dot(qg, wbuf_ref[0, 0, g]), _dot(qg, new_win[kcol, :])], axis=1) - slope * dist.astype(F32)
        p = _masked_softmax(s, (dist >= 0) & (dist < WINDOW))
        o_win = _dot_nt(p[:, :n_buf], wbuf_ref[0, 1, g]) + _dot_nt(p[:, n_buf:], new_win[vcol, :])

        og = _gate_cols(ga, 0, g) * o_cmp + _gate_cols(ga, 1, g) * o_slc + _gate_cols(ga, 2, g) * o_win
        pieces += [og[r * SLOT:(r + 1) * SLOT] for r in range(NSA_GROUP_SIZE)]
    o_ref[0] = jnp.concatenate(pieces, axis=1)


def _nsa_sample(q, ga, ncmp, nslc, nwin, win_buf, cache_cmp, cache_slc, page_table, wts, layer, past, n_tok):
    bs = q.shape[0]
    n_buf = win_buf.shape[-1]
    n_cmp = past // CMP_STRIDE
    tk = wts["e_sel_sample"].shape[2]
    slot_spec = lambda w: pl.BlockSpec((1, SLOT, w), lambda b, pt: (b, 0, 0))
    const = lambda shape, lyr=None: (
        pl.BlockSpec(shape, lambda b, pt: (0,) * len(shape), pipeline_mode=pl.Buffered(1)) if lyr is None else
        pl.BlockSpec((None,) + shape, lambda b, pt: (lyr,) + (0,) * len(shape), pipeline_mode=pl.Buffered(1)))
    grid_spec = pltpu.PrefetchScalarGridSpec(
        num_scalar_prefetch=1,
        grid=(bs,),
        in_specs=[slot_spec(A_WIDTH), slot_spec(LANES), slot_spec(KV_WIDTH), slot_spec(KV_WIDTH), slot_spec(KV_WIDTH),
                  pl.BlockSpec((None, 1, 2, NSA_KV_GROUPS, HEAD_DIM, n_buf), lambda b, pt: (layer, b, 0, 0, 0, 0)),
                  pl.BlockSpec(memory_space=pl.ANY), pl.BlockSpec(memory_space=pl.ANY),
                  const((CMP_STRIDE, KV_WIDTH, 2 * KV_WIDTH), layer), const((KV_WIDTH, KV_WIDTH), layer),
                  const((LANES, LANES)), const((1, LANES), layer),
                  const(wts["m_sel_sample"].shape), const(wts["e_sel_sample"].shape), const(wts["regroup"].shape)],
        out_specs=slot_spec(A_WIDTH),
        scratch_shapes=[pltpu.VMEM((2, KV_WIDTH, past), F32),
                        pltpu.VMEM((CMP_STRIDE, n_cmp + SUBLANES, KV_WIDTH), F32),
                        pltpu.VMEM((2, 2, NSA_KV_GROUPS, HEAD_DIM, past + LANES), F32),
                        pltpu.VMEM((n_cmp + SUBLANES, KV_WIDTH), F32),
                        pltpu.SemaphoreType.DMA((2, 2))],
    )
    return pl.pallas_call(
        functools.partial(_nsa_sample_kernel, layer=layer, past=past, n_tok=n_tok, tk=tk),
        grid_spec=grid_spec,
        out_shape=jax.ShapeDtypeStruct((bs, SLOT, A_WIDTH), F32),
        compiler_params=pltpu.CompilerParams(dimension_semantics=("arbitrary",), vmem_limit_bytes=VMEM_LIMIT),
        name="nsa_sample",
    )(page_table, q, ga, ncmp, nslc, nwin, win_buf, cache_cmp, cache_slc,
      wts["w_c1_halves"], wts["w_c2_full"], wts["ones128"], wts["g_k_cmp"], wts["m_sel_sample"], wts["e_sel_sample"],
      wts["regroup"])


def _gla_kernel(qg_ref, kb_ref, vb_ref, la_ref, rb_ref, s0_ref, ggla_ref, o_ref, sout_ref, s_scr, *, chunk, n_valid):
    j = pl.program_id(1)

    @pl.when(j == 0)
    def _():
        s_scr[...] = s0_ref[0]

    tt = qg_ref.shape[1]
    n_ch = tt // chunk
    shift = int(math.log2(chunk))
    ri = lax.broadcasted_iota(jnp.int32, (tt, tt), 0)
    ci = lax.broadcasted_iota(jnp.int32, (tt, tt), 1)
    same_chunk = lax.shift_right_logical(ri, shift) == lax.shift_right_logical(ci, shift)
    causal = same_chunk & (ri >= ci)
    chunk_of_row = lax.shift_right_logical(lax.broadcasted_iota(jnp.int32, (tt, LANES), 0), shift)
    chunk_lane = lax.broadcasted_iota(jnp.int32, (tt, LANES), 1)
    as_b = lambda mask: jnp.where(mask, 1.0, 0.0).astype(BF16)
    left = lambda a, m: jnp.dot(m, a, preferred_element_type=F32)
    g_gla = ggla_ref[...]

    q, k, v, la = qg_ref[0], kb_ref[0], vb_ref[0], la_ref[0]
    if n_valid < chunk:
        ok = (lax.broadcasted_iota(jnp.int32, (tt, 1), 0) & (chunk - 1)) < n_valid
        q, k, v, la = (jnp.where(ok, a, 0.0) for a in (q, k, v, la))
    bcum = _dot_wide(la, as_b(causal), 3, dot=left)
    bend = jnp.concatenate([jnp.broadcast_to(bcum[(c + 1) * chunk - 1:(c + 1) * chunk, :], (chunk, GK_WIDTH))
                            for c in range(n_ch)], axis=0)
    bend_t = _dot_wide(la, as_b(chunk_of_row == chunk_lane), 3, dot=lambda a, m: lax.dot_general(
        a, m, (((0,), (0,)), ((), ())), preferred_element_type=F32))
    qe = q * jnp.exp(bcum)
    kd = k * jnp.exp(-bcum)
    kl = k * jnp.exp(bend - bcum)
    for h in range(GLA_HEADS):
        kc = slice(h * GLA_DK, (h + 1) * GLA_DK)
        vc = slice(h * GLA_DV, (h + 1) * GLA_DV)
        o_intra = _dot(jnp.where(causal, _dot_nt(qe[:, kc], kd[:, kc]), 0.0), v[:, vc])
        s_h = s_scr[h]
        o_inter = []
        for c in range(n_ch):
            rows = slice(c * chunk, (c + 1) * chunk)
            o_inter.append(_dot(qe[rows, kc], s_h))
            s_h = jnp.exp(bend_t[kc, c:c + 1]) * s_h + _dot_tn(kl[rows, kc], v[rows, vc])
        s_scr[h] = s_h
        o = o_intra + jnp.concatenate(o_inter, axis=0)
        r = rb_ref[0, :, vc]
        o_ref[0, :, vc] = _rms_rows(o, g_gla) * (r * _sigmoid(r))
    sout_ref[0] = s_scr[...]


def _gla(qg, kb, vb, la, rb, s0, wts, layer, tt, chunk, n_valid):
    b, t, _ = qg.shape
    tile = lambda w: pl.BlockSpec((1, tt, w), lambda bi, j: (bi, j, 0))
    st = pl.BlockSpec((1, GLA_HEADS, GLA_DK, GLA_DV), lambda bi, j: (bi, 0, 0, 0))
    return pl.pallas_call(
        functools.partial(_gla_kernel, chunk=chunk, n_valid=n_valid),
        grid=(b, t // tt),
        in_specs=[tile(GK_WIDTH), tile(GK_WIDTH), tile(B_WIDTH), tile(GK_WIDTH), tile(B_WIDTH), st,
                  pl.BlockSpec((None, 1, GLA_DV), lambda bi, j: (layer, 0, 0))],
        out_specs=[tile(B_WIDTH), st],
        out_shape=[jax.ShapeDtypeStruct((b, t, B_WIDTH), F32),
                   jax.ShapeDtypeStruct((b, GLA_HEADS, GLA_DK, GLA_DV), F32)],
        scratch_shapes=[pltpu.VMEM((GLA_HEADS, GLA_DK, GLA_DV), F32)],
        compiler_params=pltpu.CompilerParams(dimension_semantics=("arbitrary", "arbitrary"),
                                             vmem_limit_bytes=VMEM_LIMIT),
        name="gla",
    )(qg, kb, vb, la, rb, s0, wts["g_gla"])


def _mix_ffn_kernel(*refs, tm, f_chunk, slots):
    if slots:
        (x_ref, oa_ref, ob_ref, gattn_ref, gffn_ref, wm_ref, wpa_ref, wpb_ref, wout_ref, wup_ref, wconv_ref,
         bconv_ref, wdown_ref, cb1_ref, cb2_ref, y_ref, u_ref, u_scr) = refs
    else:
        (x_ref, oa_ref, ob_ref, gattn_ref, gffn_ref, wm_ref, wpa_ref, wpb_ref, wout_ref, wup_ref, wconv_ref,
         bconv_ref, wdown_ref, y_ref, u_ref, u_scr) = refs
    x = x_ref[...]
    h = _rms_rows(x, gattn_ref[...])
    mg = jnp.dot(h.astype(BF16), wm_ref[...], preferred_element_type=F32)
    mix = (_sigmoid(mg[:, :D_MODEL]) * jnp.dot(oa_ref[...].astype(BF16), wpa_ref[...], preferred_element_type=F32)
           + _sigmoid(mg[:, D_MODEL:]) * jnp.dot(ob_ref[...].astype(BF16), wpb_ref[...], preferred_element_type=F32))
    x1 = x + jnp.dot(mix.astype(BF16), wout_ref[...], preferred_element_type=F32)
    h2 = _rms_rows(x1, gffn_ref[...]).astype(BF16)

    if slots:
        u_scr[0:SUBLANES, :] = jnp.zeros((SUBLANES, FFN_DIM), F32)
        row_in_slot = lax.broadcasted_iota(jnp.int32, (tm, 1), 0) & (SLOT - 1)
    else:
        @pl.when(pl.program_id(1) == 0)
        def _():
            u_scr[0:SUBLANES, :] = jnp.zeros((SUBLANES, FFN_DIM), F32)

    y = x1
    for c0 in range(0, FFN_DIM, f_chunk):
        fc = slice(c0, c0 + f_chunk)
        u = jnp.dot(h2, wup_ref[:, fc], preferred_element_type=F32)
        gt = jnp.dot(h2, wup_ref[:, FFN_DIM + c0:FFN_DIM + c0 + f_chunk], preferred_element_type=F32)
        u_scr[SUBLANES:SUBLANES + tm, fc] = u
        u1 = u_scr[SUBLANES - 1:SUBLANES - 1 + tm, fc]
        u2 = u_scr[SUBLANES - 2:SUBLANES - 2 + tm, fc]
        if slots:
            u1 = jnp.where(row_in_slot == 0, cb1_ref[:, fc], u1)
            u2 = jnp.where(row_in_slot < 2, cb2_ref[:, fc], u2)
            u_ref[:, fc] = u
        acc = bconv_ref[:, fc] + wconv_ref[0:1, fc] * u2 + wconv_ref[1:2, fc] * u1 + wconv_ref[2:3, fc] * u
        y = y + jnp.dot((_gelu_tanh(acc) * gt).astype(BF16), wdown_ref[fc, :], preferred_element_type=F32)
    y_ref[...] = y
    if not slots:
        tail = u_scr[tm:tm + SUBLANES, :]
        u_scr[0:SUBLANES, :] = tail
        u_ref[0] = tail


def _mix_ffn(x, oa, ob, wts, layer, tm, seq_tiles, cb=None):
    m = x.shape[0]
    slots = cb is not None
    n_seq = m // (tm * seq_tiles)
    if slots:
        idx = lambda i: (i, 0)
        grid = (m // tm,)
        sem = ("arbitrary",)
    else:
        idx = lambda b, i: (b * seq_tiles + i, 0)
        grid = (n_seq, seq_tiles)
        sem = ("arbitrary", "arbitrary")
    nidx = len(grid)

    def wspec(shape):
        return pl.BlockSpec((None,) + shape, lambda *_: (layer,) + (0,) * len(shape),
                            pipeline_mode=pl.Buffered(1))

    row = lambda w: pl.BlockSpec((tm, w), idx)
    in_specs = [row(D_MODEL), row(A_WIDTH), row(B_WIDTH),
                wspec((1, D_MODEL)), wspec((1, D_MODEL)), wspec((D_MODEL, 2 * D_MODEL)),
                wspec((A_WIDTH, D_MODEL)), wspec((B_WIDTH, D_MODEL)), wspec((D_MODEL, D_MODEL)),
                wspec((D_MODEL, 2 * FFN_DIM)), wspec((SUBLANES, FFN_DIM)), wspec((1, FFN_DIM)),
                wspec((FFN_DIM, D_MODEL))]
    args = [x, oa, ob, wts["g_attn"], wts["g_ffn"], wts["w_merge"], wts["w_pa"], wts["w_pb"], wts["w_out"],
            wts["w_up"], wts["w_conv"], wts["b_conv"], wts["w_down"]]
    if slots:
        in_specs += [row(FFN_DIM), row(FFN_DIM)]
        args += list(cb)
        u_spec = row(FFN_DIM)
        u_shape = jax.ShapeDtypeStruct((m, FFN_DIM), F32)
    else:
        u_spec = pl.BlockSpec((1, SUBLANES, FFN_DIM), lambda b, i: (b, 0, 0))
        u_shape = jax.ShapeDtypeStruct((n_seq, SUBLANES, FFN_DIM), F32)
    return pl.pallas_call(
        functools.partial(_mix_ffn_kernel, tm=tm, f_chunk=FFN_DIM, slots=slots),
        grid=grid,
        in_specs=in_specs,
        out_specs=[row(D_MODEL), u_spec],
        out_shape=[jax.ShapeDtypeStruct((m, D_MODEL), F32), u_shape],
        scratch_shapes=[pltpu.VMEM((tm + SUBLANES, FFN_DIM), F32)],
        compiler_params=pltpu.CompilerParams(dimension_semantics=sem, vmem_limit_bytes=VMEM_LIMIT),
        name="mix_ffn",
    )(*args)


def _sel_fold_matrix(n_cmp_valid, n_cmp_rows, n_sel_lanes):
    per = SEL_BLOCK // CMP_STRIDE
    m = np.zeros((n_cmp_rows, n_sel_lanes), np.float32)
    for n in range(n_cmp_valid):
        m[n, n // per] += 1.0
        if n + 1 < n_cmp_valid:
            m[n, (n + 1) // per] += 1.0
    return jnp.asarray(m, BF16)


def _pack_weights(w_in, w_cmp1, w_cmp2, g_q, g_k_cmp, g_k_slc, g_k_win, w_gate_up, b_gate, g_gla, w_pa, w_pb,
                  w_out, g_attn, g_ffn, w_up, w_conv, b_conv, w_down):
    depth = w_in.shape[0]
    splits = (A_WIDTH, KV_WIDTH, KV_WIDTH, KV_WIDTH, 3 * NSA_HEADS, GK_WIDTH, GK_WIDTH, B_WIDTH, B_WIDTH,
              GLA_GATE_RANK, 2 * D_MODEL)
    offs = np.concatenate([[0], np.cumsum(splits)])
    seg = lambda k: w_in[:, :, int(offs[k]):int(offs[k + 1])]
    padc = lambda a, w: jnp.pad(a, ((0, 0), (0, 0), (0, w - a.shape[-1])))
    w_a = jnp.concatenate([seg(0), seg(1), seg(2), seg(3), padc(seg(4), LANES), seg(5), seg(6), seg(7), seg(8),
                           padc(seg(9), LANES)], axis=-1).astype(BF16)
    same_g = jnp.eye(NSA_KV_GROUPS, dtype=F32)
    w_c1 = jnp.einsum("Lclde,gG->LlcgdGe", w_cmp1, same_g).reshape(depth, CMP_BLOCK, 2, LANES, LANES).astype(BF16)
    w_c2 = jnp.einsum("Lcef,gG->LcgeGf", w_cmp2, same_g).reshape(depth, 2, LANES, LANES).astype(BF16)
    w1h = w_cmp1.reshape(depth, 2, 2, CMP_STRIDE, HEAD_DIM, HEAD_DIM).astype(BF16)
    n_cg = 2 * NSA_KV_GROUPS
    zero1 = jnp.zeros((depth, CMP_STRIDE, HEAD_DIM, HEAD_DIM), BF16)
    zero2 = jnp.zeros((depth, HEAD_DIM, HEAD_DIM), BF16)
    w_c1_halves = jnp.concatenate([jnp.concatenate(
        [w1h[:, cg // NSA_KV_GROUPS, half] if cg == row_cg else zero1 for half in range(2) for cg in range(n_cg)],
        axis=-1) for row_cg in range(n_cg)], axis=-2)
    w_c2_full = jnp.concatenate([jnp.concatenate(
        [w_cmp2[:, cg // NSA_KV_GROUPS].astype(BF16) if cg == row_cg else zero2 for cg in range(n_cg)],
        axis=-1) for row_cg in range(n_cg)], axis=-2)
    grp = lambda n: jnp.asarray(np.kron(np.eye(n // HEAD_DIM), np.ones((HEAD_DIM, HEAD_DIM))), BF16)
    g_norm = jnp.concatenate([jnp.tile(g_q, (1, NSA_HEADS)), jnp.tile(g_k_slc, (1, NSA_KV_GROUPS)),
                              jnp.tile(g_k_win, (1, NSA_KV_GROUPS))], axis=-1)
    return {
        "w_a": w_a,
        "w_merge": w_in[:, :, int(offs[10]):].astype(BF16),
        "g_attn": g_attn[:, None, :], "g_ffn": g_ffn[:, None, :],
        "g_norm": g_norm[:, None, :], "ones128": grp(LANES),
        "g_k_cmp": jnp.tile(g_k_cmp, (1, NSA_KV_GROUPS))[:, None, :],
        "w_gate_up": jnp.pad(w_gate_up, ((0, 0), (0, LANES - GLA_GATE_RANK), (0, 0))).astype(BF16),
        "b_gate": b_gate[:, None, :], "g_gla": g_gla[:, None, :],
        "w_c1": w_c1, "w_c2": w_c2, "w_c1_halves": w_c1_halves, "w_c2_full": w_c2_full,
        "w_pa": w_pa.astype(BF16), "w_pb": w_pb.astype(BF16), "w_out": w_out.astype(BF16),
        "w_up": w_up.astype(BF16), "w_down": w_down.astype(BF16),
        "w_conv": jnp.pad(w_conv, ((0, 0), (0, SUBLANES - CONV_W), (0, 0))), "b_conv": b_conv[:, None, :],
    }


def _expand_matrix(n_blocks, n_keys):
    return jnp.asarray(np.arange(n_keys)[None, :] // SEL_BLOCK == np.arange(n_blocks)[:, None], BF16)


def _prompt_layer(x, wts, layer, b, t):
    tm = min(256, t)
    tq = min(256, t)
    q, kv_cmp, kv_slc, kv_win, ga, qg, kb, vb, rb, la = _proj_in(x, wts, layer, min(512, t))
    r3 = lambda a: a.reshape(b, t, a.shape[-1])
    kvc = _compress(r3(kv_cmp), wts, layer)
    o_a = _nsa_prompt(r3(q), r3(ga), kvc, r3(kv_slc), r3(kv_win), wts, tq)
    o_b, s_new = _gla(r3(qg), r3(kb), r3(vb), r3(la), r3(rb),
                      jnp.zeros((b, GLA_HEADS, GLA_DK, GLA_DV), F32), wts, layer,
                      tt=min(512, t), chunk=math.gcd(t, GLA_CHUNK), n_valid=math.gcd(t, GLA_CHUNK))
    y, u_tail = _mix_ffn(x, o_a.reshape(b * t, A_WIDTH), o_b.reshape(b * t, B_WIDTH), wts, layer, tm, t // tm)
    kv6 = lambda a: a.reshape(b, t, 2, NSA_KV_GROUPS, HEAD_DIM)
    n_win = min(WINDOW, t)
    state = (kv6(kv_cmp), kv6(kv_slc), kv6(kv_win)[:, t - n_win:], s_new, u_tail[:, SUBLANES - (CONV_W - 1):])
    return y, state


def _sample_layer(x, wts, layer, bs, n_tok, past, cache_cmp, cache_slc, win_buf, gla_state, conv_buf, page_table):
    m = bs * SLOT
    q, kv_cmp, kv_slc, kv_win, ga, qg, kb, vb, rb, la = _proj_in(x, wts, layer, m)
    r3 = lambda a: a.reshape(bs, SLOT, a.shape[-1])
    o_a = _nsa_sample(r3(q), r3(ga), r3(kv_cmp), r3(kv_slc), r3(kv_win), win_buf, cache_cmp, cache_slc,
                      page_table, wts, layer, past, n_tok)
    o_b, s_new = _gla(r3(qg), r3(kb), r3(vb), r3(la), r3(rb), gla_state[layer], wts, layer,
                      tt=SLOT, chunk=SLOT, n_valid=n_tok)
    cb = conv_buf[layer]
    zeros = lambda n: jnp.zeros((bs, n, FFN_DIM), F32)
    cb1 = jnp.concatenate([cb[:, 1:2], zeros(SLOT - 1)], axis=1).reshape(m, FFN_DIM)
    cb2 = jnp.concatenate([cb, zeros(SLOT - 2)], axis=1).reshape(m, FFN_DIM)
    y, u = _mix_ffn(x, o_a.reshape(m, A_WIDTH), o_b.reshape(m, B_WIDTH), wts, layer, m, 1, cb=(cb1, cb2))
    kv6 = lambda a: r3(a)[:, :n_tok].reshape(bs, n_tok, 2, NSA_KV_GROUPS, HEAD_DIM)
    u_ext = jnp.concatenate([cb, u.reshape(bs, SLOT, FFN_DIM)[:, :n_tok]], axis=1)
    state = (kv6(kv_cmp), kv6(kv_slc), kv6(kv_win), s_new, u_ext[:, -(CONV_W - 1):])
    return y, state


def kernel(x_prompt, x_sample, cache_kv_cmp, cache_kv_slc, state_kv_win, state_gla, state_conv, page_table,
           w_in, w_cmp1, w_cmp2, g_q, g_k_cmp, g_k_slc, g_k_win, w_gate_up, b_gate, g_gla, w_pa, w_pb, w_out,
           g_attn, g_ffn, w_up, w_conv, b_conv, w_down):
    depth = w_in.shape[0]
    bp, t, _ = x_prompt.shape
    bs, n_tok, _ = x_sample.shape
    page = cache_kv_cmp.shape[2]
    past = page_table.shape[1] * page
    assert n_tok <= SLOT and t % CMP_STRIDE == 0 and past % SEL_BLOCK == 0

    wts = _pack_weights(w_in, w_cmp1, w_cmp2, g_q, g_k_cmp, g_k_slc, g_k_win, w_gate_up, b_gate, g_gla, w_pa,
                        w_pb, w_out, g_attn, g_ffn, w_up, w_conv, b_conv, w_down)
    n_cmp_p = t // CMP_STRIDE
    fold = _sel_fold_matrix(n_cmp_p - 1, n_cmp_p, LANES - LANE_HOT).T
    wts["fold_prompt"] = jnp.pad(fold, ((LANE_HOT, 0), (0, 0)))
    gate_expand = np.zeros((3, LANES, A_WIDTH), np.float32)
    for br in range(3):
        for h in range(NSA_HEADS):
            gate_expand[br, br * NSA_HEADS + h, h * HEAD_DIM:(h + 1) * HEAD_DIM] = 1.0
    wts["gate_expand"] = jnp.asarray(gate_expand, BF16)
    n_cmp_s = (past + CMP_STRIDE) // CMP_STRIDE - 1
    sel_lanes = -(-(past // SEL_BLOCK + 1) // LANES) * LANES
    tk_s = min(1024, past)
    wts["m_sel_sample"] = _sel_fold_matrix(n_cmp_s, n_cmp_s, sel_lanes)
    blk_per_chunk = tk_s // SEL_BLOCK
    wts["e_sel_sample"] = jnp.stack([_expand_matrix(LANES, LANES * SEL_BLOCK)[:, j * tk_s:(j + 1) * tk_s]
                                     for j in range(LANES // blk_per_chunk)])

    cache_cmp = cache_kv_cmp.transpose(0, 1, 3, 4, 5, 2).reshape(depth, -1, KV_WIDTH, page)
    regroup = np.zeros((page, page), np.float32)
    hpp = page // CMP_STRIDE
    for l in range(CMP_STRIDE):
        for h in range(hpp):
            regroup[l * hpp + h, h * CMP_STRIDE + l] = 1.0
    wts["regroup"] = jnp.asarray(regroup, BF16)
    cache_slc = cache_kv_slc.transpose(0, 1, 3, 4, 5, 2)
    win_buf = state_kv_win.transpose(0, 1, 3, 4, 5, 2)

    y_p = x_prompt.reshape(bp * t, D_MODEL)
    y_s = jnp.pad(x_sample, ((0, 0), (0, SLOT - n_tok), (0, 0))).reshape(bs * SLOT, D_MODEL)
    st_p, st_s = [], []
    for l in range(depth):
        y_p, sp = _prompt_layer(y_p, wts, l, bp, t)
        y_s, ss = _sample_layer(y_s, wts, l, bs, n_tok, past, cache_cmp, cache_slc, win_buf, state_gla, state_conv,
                                page_table)
        st_p.append(sp)
        st_s.append(ss)
    stack = lambda sts, k: jnp.stack([s[k] for s in sts])
    sample_states = [stack(st_s, k) for k in range(5)]
    sample_states[2] = jnp.concatenate([state_kv_win[:, :, n_tok:], sample_states[2]], axis=2)
    return ((y_p.reshape(bp, t, D_MODEL), y_s.reshape(bs, SLOT, D_MODEL)[:, :n_tok])
            + tuple(stack(st_p, k) for k in range(5)) + tuple(sample_states))
```
